```python
import jax, jax.numpy as jnp
from jax import lax
import numpy as np

D_MODEL = 1024
BATCH = 4
SEQ = 8192
DEPTH = 4

EPS = 1e-6
HEAD_DIM = 64
LRU_WIDTH = 384
LRU_BLOCKS = 6
LRU_BLOCK = LRU_WIDTH // LRU_BLOCKS
CONV_WIDTH = 4
LRU_C = 8.0
FOX_HEADS = 6
FOX_WIDTH = FOX_HEADS * HEAD_DIM
Q_BLOCK = 128
GLA_HEADS = 4
GLA_DK = 64
GLA_DV = 64
GLA_KW = GLA_HEADS * GLA_DK
GLA_VW = GLA_HEADS * GLA_DV
GLA_RANK = 16
GLA_TAU = 16.0
GLA_CHUNK = 64
D_MIX = LRU_WIDTH + FOX_WIDTH + GLA_VW
SPLIT_SIZES = (LRU_WIDTH, LRU_WIDTH, FOX_WIDTH, FOX_WIDTH, FOX_WIDTH, FOX_HEADS,
               GLA_KW, GLA_KW, GLA_VW, GLA_RANK, GLA_VW)
D_IN = sum(SPLIT_SIZES)
D_FF = -(-8 * D_MODEL // (3 * 256)) * 256
N_MOD = 6

kernel_name = 'hybrid_rglru_fox_gla_block'


def rms_norm(x, gain):
    xf = x.astype(jnp.float32)
    y = xf * lax.rsqrt(jnp.mean(xf * xf, axis=-1, keepdims=True) + EPS)
    return (y * gain.astype(jnp.float32)).astype(x.dtype)


def causal_depthwise_conv(x, w, b):
    S = x.shape[1]
    K = w.shape[0]
    xp = jnp.pad(x, ((0, 0), (K - 1, 0), (0, 0)))
    out = xp[:, K - 1:K - 1 + S] * w[K - 1] + b
    for j in range(K - 1):
        out = out + xp[:, j:j + S] * w[j]
    return out


def rg_lru(x, w_r, b_r, w_i, b_i, lam):
    B, S, _ = x.shape
    xb = x.reshape(B, S, LRU_BLOCKS, LRU_BLOCK)
    r = jax.nn.sigmoid(jnp.einsum('bsnd,nde->bsne', xb, w_r).reshape(B, S, LRU_WIDTH) + b_r)
    i = jax.nn.sigmoid(jnp.einsum('bsnd,nde->bsne', xb, w_i).reshape(B, S, LRU_WIDTH) + b_i)
    log_a = (-LRU_C * r.astype(jnp.float32) * jax.nn.softplus(-lam.astype(jnp.float32)))
    a = jnp.exp(log_a)
    mult = jnp.sqrt(-jnp.expm1(2.0 * log_a))
    u = mult * (i * x).astype(jnp.float32)

    def combine(left, right):
        a1, b1 = left
        a2, b2 = right
        return a1 * a2, a2 * b1 + b2

    _, h = lax.associative_scan(combine, (a, u), axis=1)
    return h.astype(x.dtype)


def forgetting_attention(q, k, v, f_logit, b_f, q_gain, k_gain):
    B, S, _ = q.shape
    q = rms_norm(q.reshape(B, S, FOX_HEADS, HEAD_DIM), q_gain).transpose(0, 2, 1, 3)
    k = rms_norm(k.reshape(B, S, FOX_HEADS, HEAD_DIM), k_gain).transpose(0, 2, 1, 3)
    v = v.reshape(B, S, FOX_HEADS, HEAD_DIM).transpose(0, 2, 1, 3)
    log_f = jax.nn.log_sigmoid((f_logit + b_f).astype(jnp.float32))
    cum = jnp.cumsum(log_f, axis=1).transpose(0, 2, 1)
    scale = HEAD_DIM ** -0.5
    k_pos = jnp.arange(S)

    def one_block(start):
        qb = lax.dynamic_slice_in_dim(q, start, Q_BLOCK, axis=2)
        cb = lax.dynamic_slice_in_dim(cum, start, Q_BLOCK, axis=2)
        s = (jnp.einsum('bhqd,bhkd->bhqk', qb, k).astype(jnp.float32) * scale
             + cb[..., :, None] - cum[..., None, :])
        q_pos = start + jnp.arange(Q_BLOCK)
        s = jnp.where(k_pos[None, :] <= q_pos[:, None], s, -jnp.inf)
        p = jax.nn.softmax(s, axis=-1)
        return jnp.einsum('bhqk,bhkd->bhqd', p.astype(v.dtype), v)

    starts = jnp.arange(S // Q_BLOCK) * Q_BLOCK
    o = lax.map(one_block, starts)
    return o.transpose(1, 0, 3, 2, 4).reshape(B, S, FOX_WIDTH)


def gated_linear_attention(q, k, v, low, g, w_alpha, b_alpha, out_gain):
    B, S, _ = q.shape
    nc = S // GLA_CHUNK

    def to_chunks(t, d):
        return t.reshape(B, nc, GLA_CHUNK, GLA_HEADS, d).transpose(0, 3, 1, 2, 4).astype(jnp.float32)

    log_alpha = jax.nn.log_sigmoid((low @ w_alpha + b_alpha).astype(jnp.float32)) / GLA_TAU
    qh = to_chunks(q, GLA_DK) * (GLA_DK ** -0.5)
    kh = to_chunks(k, GLA_DK)
    vh = to_chunks(v, GLA_DV)
    la = to_chunks(log_alpha, GLA_DK)
    bcum = jnp.cumsum(la, axis=3)
    b_last = bcum[..., -1:, :]
    q_dec = qh * jnp.exp(bcum)
    k_dec = kh * jnp.exp(-bcum)
    k_to_end = kh * jnp.exp(b_last - bcum)
    att = jnp.einsum('bhnid,bhnjd->bhnij', q_dec, k_dec)
    att = jnp.where(jnp.tril(jnp.ones((GLA_CHUNK, GLA_CHUNK), dtype=bool)), att, 0.0)
    o_intra = jnp.einsum('bhnij,bhnjv->bhniv', att, vh)
    kv = jnp.einsum('bhnjd,bhnjv->bhndv', k_to_end, vh)
    decay = jnp.exp(b_last[..., 0, :])

    def step(state, inp):
        dec, kv_n = inp
        return dec[..., None] * state + kv_n, state

    init = jnp.zeros((B, GLA_HEADS, GLA_DK, GLA_DV), jnp.float32)
    _, s_prev = lax.scan(step, init, (decay.transpose(2, 0, 1, 3), kv.transpose(2, 0, 1, 3, 4)))
    s_prev = s_prev.transpose(1, 2, 0, 3, 4)
    o_inter = jnp.einsum('bhnid,bhndv->bhniv', q_dec, s_prev)
    o = (o_intra + o_inter).transpose(0, 2, 3, 1, 4).reshape(B, S, GLA_HEADS, GLA_DV)
    o = rms_norm(o, out_gain).reshape(B, S, GLA_VW)
    return (o * jax.nn.silu(g.astype(jnp.float32))).astype(q.dtype)


def hybrid_layer(x, c_act, norm1_gain, norm2_gain, w_mod, b_mod, w_in, conv_w, conv_b,
                 lru_w_r, lru_b_r, lru_w_i, lru_b_i, lru_lambda, fox_b_f, fox_q_gain,
                 fox_k_gain, gla_w_alpha, gla_b_alpha, gla_out_gain, w_out,
                 ffn_w_gate_up, ffn_w_down):
    mod = (c_act @ w_mod + b_mod)[:, None, :]
    shift1, scale1, gate1, shift2, scale2, gate2 = jnp.split(mod, N_MOD, axis=-1)

    h = rms_norm(x, norm1_gain) * (1.0 + scale1) + shift1
    proj = h @ w_in
    (xa, ya, fq, fk, fv, ff, gq, gk, gv, glow, gg) = jnp.split(
        proj, np.cumsum(SPLIT_SIZES)[:-1].tolist(), axis=-1)
    xa = causal_depthwise_conv(xa, conv_w, conv_b)
    out_a = rg_lru(xa, lru_w_r, lru_b_r, lru_w_i, lru_b_i, lru_lambda) * jax.nn.gelu(ya)
    out_b = forgetting_attention(fq, fk, fv, ff, fox_b_f, fox_q_gain, fox_k_gain)
    out_c = gated_linear_attention(gq, gk, gv, glow, gg, gla_w_alpha, gla_b_alpha, gla_out_gain)
    mix = jnp.concatenate([out_a, out_b, out_c], axis=-1) @ w_out
    x = x + gate1 * mix

    h = rms_norm(x, norm2_gain) * (1.0 + scale2) + shift2
    gt, up = jnp.split(h @ ffn_w_gate_up, 2, axis=-1)
    x = x + gate2 * ((jax.nn.silu(gt) * up) @ ffn_w_down)
    return x


def setup_inputs(seed: int = 0) -> dict:
    key = jax.random.key(seed)
    ks = jax.random.split(key, 24)
    f32 = jnp.float32
    L = DEPTH

    def nrm(k, shape, fan_in, mult=1.0):
        return (jax.random.normal(k, shape, f32) * (mult * fan_in ** -0.5)).astype(f32)

    def gain(k, shape):
        return 1.0 + 0.02 * jax.random.normal(k, shape, f32)

    def small(k, shape):
        return 0.01 * jax.random.normal(k, shape, f32)

    a0 = jax.random.uniform(ks[12], (L, LRU_WIDTH), f32, minval=0.9, maxval=0.999)
    return {
        'x': jax.random.normal(ks[0], (BATCH, SEQ, D_MODEL), f32),
        'c': jax.random.normal(ks[1], (BATCH, D_MODEL), f32),
        'norm1_gain': gain(ks[2], (L, D_MODEL)),
        'norm2_gain': gain(ks[3], (L, D_MODEL)),
        'w_mod': nrm(ks[4], (L, D_MODEL, N_MOD * D_MODEL), D_MODEL, 0.5),
        'b_mod': small(ks[5], (L, N_MOD * D_MODEL)),
        'w_in': nrm(ks[6], (L, D_MODEL, D_IN), D_MODEL),
        'conv_w': nrm(ks[7], (L, CONV_WIDTH, LRU_WIDTH), CONV_WIDTH),
        'conv_b': small(ks[8], (L, LRU_WIDTH)),
        'lru_w_r': nrm(ks[9], (L, LRU_BLOCKS, LRU_BLOCK, LRU_BLOCK), LRU_BLOCK),
        'lru_b_r': small(ks[10], (L, LRU_WIDTH)),
        'lru_w_i': nrm(ks[11], (L, LRU_BLOCKS, LRU_BLOCK, LRU_BLOCK), LRU_BLOCK),
        'lru_b_i': small(ks[13], (L, LRU_WIDTH)),
        'lru_lambda': jnp.log(a0) - jnp.log1p(-a0),
        'fox_b_f': jax.random.uniform(ks[14], (L, FOX_HEADS), f32, minval=1.0, maxval=4.0),
        'fox_q_gain': gain(ks[15], (L, HEAD_DIM)),
        'fox_k_gain': gain(ks[16], (L, HEAD_DIM)),
        'gla_w_alpha': nrm(ks[17], (L, GLA_RANK, GLA_KW), GLA_RANK),
        'gla_b_alpha': small(ks[18], (L, GLA_KW)),
        'gla_out_gain': gain(ks[19], (L, GLA_DV)),
        'w_out': nrm(ks[20], (L, D_MIX, D_MODEL), D_MIX),
        'ffn_w_gate_up': nrm(ks[21], (L, D_MODEL, 2 * D_FF), D_MODEL),
        'ffn_w_down': nrm(ks[22], (L, D_FF, D_MODEL), D_FF),
    }


def reference(x, c, norm1_gain, norm2_gain, w_mod, b_mod, w_in, conv_w, conv_b,
              lru_w_r, lru_b_r, lru_w_i, lru_b_i, lru_lambda, fox_b_f, fox_q_gain,
              fox_k_gain, gla_w_alpha, gla_b_alpha, gla_out_gain, w_out,
              ffn_w_gate_up, ffn_w_down):
    c_act = jax.nn.silu(c)
    for l in range(DEPTH):
        x = hybrid_layer(x, c_act, norm1_gain[l], norm2_gain[l], w_mod[l], b_mod[l], w_in[l],
                         conv_w[l], conv_b[l], lru_w_r[l], lru_b_r[l], lru_w_i[l], lru_b_i[l],
                         lru_lambda[l], fox_b_f[l], fox_q_gain[l], fox_k_gain[l],
                         gla_w_alpha[l], gla_b_alpha[l], gla_out_gain[l], w_out[l],
                         ffn_w_gate_up[l], ffn_w_down[l])
    return x
```

```python
import functools

import numpy as np
import jax
import jax.numpy as jnp
from jax import lax
from jax.experimental import pallas as pl
from jax.experimental.pallas import tpu as pltpu

F32 = jnp.float32
BF16 = jnp.bfloat16

EPS = 1e-6
HEAD_DIM = 64
LRU_WIDTH = 384
LRU_BLOCKS = 6
CONV_WIDTH = 4
LRU_C = 8.0
FOX_HEADS = 6
FOX_WIDTH = FOX_HEADS * HEAD_DIM
GLA_HEADS = 4
GLA_DK = 64
GLA_KW = GLA_HEADS * GLA_DK
GLA_VW = GLA_KW
GLA_RANK = 16
GLA_TAU = 16.0
GLA_CHUNK = 64
N_MOD = 6

LANES = 128
SUBLANES = 8
VMEM_LIMIT_BYTES = 56 * 1024 * 1024

LRU_COLS = 2 * LRU_WIDTH
FOX_COLS = 3 * FOX_WIDTH
GLA_COLS = 3 * GLA_KW + GLA_VW
SMALL_COLS = LANES
GLOW_LANE0 = 8
NP_COLS = LRU_COLS + FOX_COLS + GLA_COLS + SMALL_COLS

AUG = LANES
N_SPLIT = 3
ONE_LANE = N_SPLIT * FOX_HEADS
NEG_BIG = -1e30


def _sigmoid(x):
    return 1.0 / (1.0 + jnp.exp(-x))


def _log_sigmoid(x):
    return jnp.minimum(x, 0.0) - jnp.log(1.0 + jnp.exp(-jnp.abs(x)))


def _softplus(x):
    return jnp.maximum(x, 0.0) + jnp.log(1.0 + jnp.exp(-jnp.abs(x)))


def _gelu_tanh(x):
    c = np.float32(np.sqrt(2.0 / np.pi))
    return 0.5 * x * (1.0 + jnp.tanh(c * (x + 0.044715 * (x * x * x))))


def _dot(a, b):
    return jnp.dot(a, b, preferred_element_type=F32)


def _cparams(sem):
    return pltpu.CompilerParams(dimension_semantics=sem, vmem_limit_bytes=VMEM_LIMIT_BYTES)


def _const_spec(shape):
    nd = len(shape)
    return pl.BlockSpec(shape, lambda *_: (0,) * nd)


def _resident_spec(shape):
    nd = len(shape)
    return pl.BlockSpec(shape, lambda *_: (0,) * nd, pipeline_mode=pl.Buffered(1))


def _mod_kernel(c_ref, w_ref, b_ref, o_ref):
    c = c_ref[...]
    ca = (c * _sigmoid(c)).astype(BF16)
    o_ref[0] = _dot(ca, w_ref[0].astype(BF16)) + b_ref[0]


def _modulation(c, w_mod, b_mod):
    L, D, N = w_mod.shape
    B = c.shape[0]
    bp = -(-B // SUBLANES) * SUBLANES
    cp = jnp.pad(c, ((0, bp - B), (0, 0)))
    tn = 1024
    out = pl.pallas_call(
        _mod_kernel,
        grid=(L, N // tn),
        in_specs=[
            pl.BlockSpec((bp, D), lambda l, n: (0, 0)),
            pl.BlockSpec((1, D, tn), lambda l, n: (l, 0, n)),
            pl.BlockSpec((1, 1, tn), lambda l, n: (l, 0, n)),
        ],
        out_specs=pl.BlockSpec((1, bp, tn), lambda l, n: (l, 0, n)),
        out_shape=jax.ShapeDtypeStruct((L, bp, N), F32),
        compiler_params=_cparams(("arbitrary", "arbitrary")),
        name="adaln_mod",
    )(cp, w_mod, b_mod.reshape(L, 1, N))
    return out[:, :B].reshape(L, B, N_MOD, D)


def _rms_mod(x, gain, scale, shift):
    ms = jnp.mean(x * x, axis=-1, keepdims=True)
    y = x * lax.rsqrt(ms + EPS) * gain
    return y * (1.0 + scale) + shift


def _in_proj_kernel(x_ref, mod_ref, g_ref, w_ref, lru_ref, fox_ref, gla_ref, small_ref):
    h = _rms_mod(x_ref[0], g_ref[...], mod_ref[0, 1:2, :], mod_ref[0, 0:1, :]).astype(BF16)
    c0 = 0
    for ref, n in ((lru_ref, LRU_COLS), (fox_ref, FOX_COLS), (gla_ref, GLA_COLS), (small_ref, SMALL_COLS)):
        ref[0] = _dot(h, w_ref[:, c0:c0 + n]).astype(ref.dtype)
        c0 += n


def _in_proj(x, mod, gain, w, tm):
    B, S, D = x.shape
    tok = lambda n: pl.BlockSpec((1, tm, n), lambda b, i: (b, i, 0))
    return pl.pallas_call(
        _in_proj_kernel,
        grid=(B, S // tm),
        in_specs=[
            tok(D),
            pl.BlockSpec((1, N_MOD, D), lambda b, i: (b, 0, 0)),
            _const_spec((1, D)),
            _resident_spec((D, NP_COLS)),
        ],
        out_specs=[tok(LRU_COLS), tok(FOX_COLS), tok(GLA_COLS), tok(SMALL_COLS)],
        out_shape=[
            jax.ShapeDtypeStruct((B, S, LRU_COLS), BF16),
            jax.ShapeDtypeStruct((B, S, FOX_COLS), BF16),
            jax.ShapeDtypeStruct((B, S, GLA_COLS), BF16),
            jax.ShapeDtypeStruct((B, S, SMALL_COLS), F32),
        ],
        compiler_params=_cparams(("arbitrary", "arbitrary")),
        name="in_proj",
    )(x, mod, gain, w)


def _scan_add(v):
    n = v.shape[0]
    row = lax.broadcasted_iota(jnp.int32, v.shape, 0)
    d = 1
    while d < n:
        v = v + jnp.where(row >= d, pltpu.roll(v, d, axis=0), 0.0)
        d *= 2
    return v


def _scan_linear(a, u):
    n = a.shape[0]
    row = lax.broadcasted_iota(jnp.int32, a.shape, 0)
    d = 1
    while d < n:
        m = row >= d
        u = u + jnp.where(m, a * pltpu.roll(u, d, axis=0), 0.0)
        a = a * jnp.where(m, pltpu.roll(a, d, axis=0), 1.0)
        d *= 2
    return a, u


def _lru_kernel(lru_ref, cw_ref, cb_ref, wr_ref, br_ref, wi_ref, bi_ref, lam_ref, o_ref,
                xs_ref, h_ref):
    ts = o_ref.shape[1]
    W = LRU_WIDTH

    @pl.when(pl.program_id(1) == 0)
    def _():
        xs_ref[0:SUBLANES, :] = jnp.zeros((SUBLANES, W), F32)
        h_ref[...] = jnp.zeros_like(h_ref)

    xa = lru_ref[0, :, 0:W].astype(F32)
    ya = lru_ref[0, :, W:2 * W].astype(F32)
    xs_ref[SUBLANES:SUBLANES + ts, :] = xa
    xc = xa * cw_ref[CONV_WIDTH - 1:CONV_WIDTH, :] + cb_ref[...]
    for k in range(1, CONV_WIDTH):
        j = CONV_WIDTH - 1 - k
        xc = xc + xs_ref[pl.ds(SUBLANES - k, ts), :] * cw_ref[j:j + 1, :]
    xs_ref[0:SUBLANES, :] = xs_ref[ts:ts + SUBLANES, :]

    xcb = xc.astype(BF16)
    r = _sigmoid(_dot(xcb, wr_ref[...]) + br_ref[...])
    i = _sigmoid(_dot(xcb, wi_ref[...]) + bi_ref[...])
    log_a = (-LRU_C) * r * _softplus(-lam_ref[...])
    a = jnp.exp(log_a)
    mult = jnp.sqrt(1.0 - a * a)
    u = mult * (i * xc)
    acum, hloc = _scan_linear(a, u)
    h = acum * h_ref[0:1, :] + hloc
    h_ref[...] = jnp.broadcast_to(h[ts - 1:ts, :], h_ref.shape)
    o_ref[0] = (h * _gelu_tanh(ya)).astype(o_ref.dtype)


def _lru_mixer(lru, cw, cb, wr, br, wi, bi, lam, ts):
    B, S, _ = lru.shape
    W = LRU_WIDTH
    return pl.pallas_call(
        _lru_kernel,
        grid=(B, S // ts),
        in_specs=[
            pl.BlockSpec((1, ts, LRU_COLS), lambda b, i: (b, i, 0)),
            _const_spec((CONV_WIDTH, W)), _const_spec((1, W)),
            _const_spec((W, W)), _const_spec((1, W)),
            _const_spec((W, W)), _const_spec((1, W)),
            _const_spec((1, W)),
        ],
        out_specs=pl.BlockSpec((1, ts, W), lambda b, i: (b, i, 0)),
        out_shape=jax.ShapeDtypeStruct((B, S, W), BF16),
        scratch_shapes=[pltpu.VMEM((ts + SUBLANES, W), F32), pltpu.VMEM((SUBLANES, W), F32)],
        compiler_params=_cparams(("arbitrary", "arbitrary")),
        name="lru_mixer",
    )(lru, cw, cb, wr, br, wi, bi, lam)


def _fox_prep_kernel(fox_ref, small_ref, bf_ref, qg_ref, kg_ref, ones_ref, place_ref,
                     rq_ref, rk_ref, kaug_ref, qaugt_ref, vt_ref, cum_ref):
    tm = fox_ref.shape[1]
    W = FOX_WIDTH

    @pl.when(pl.program_id(1) == 0)
    def _():
        cum_ref[...] = jnp.zeros_like(cum_ref)

    q = fox_ref[0, :, 0:W].astype(F32)
    k = fox_ref[0, :, W:2 * W].astype(F32)
    v = fox_ref[0, :, 2 * W:3 * W].astype(F32)

    def head_norm(t, gain):
        ms = _dot((t * t).astype(BF16), ones_ref[...]) * (1.0 / HEAD_DIM)
        return (t * lax.rsqrt(ms + EPS) * gain).astype(BF16)

    qn = head_norm(q, qg_ref[...])
    kn = head_norm(k, kg_ref[...])

    lf = _log_sigmoid(small_ref[0] + bf_ref[...])
    cum = _scan_add(lf) + cum_ref[0:1, :]
    cum_ref[...] = jnp.broadcast_to(cum[tm - 1:tm, :], cum_ref.shape)

    c1 = cum.astype(BF16).astype(F32)
    r1 = cum - c1
    c2 = r1.astype(BF16).astype(F32)
    c3 = (r1 - c2).astype(BF16).astype(F32)
    lane = lax.broadcasted_iota(jnp.int32, cum.shape, 1)
    H = FOX_HEADS
    packed = jnp.where(lane < H, c1, 0.0)
    packed = jnp.where((lane >= H) & (lane < 2 * H), pltpu.roll(c2, H, axis=1), packed)
    packed = jnp.where((lane >= 2 * H) & (lane < 3 * H), pltpu.roll(c3, 2 * H, axis=1), packed)
    packed = jnp.where(lane == ONE_LANE, 1.0, packed).astype(BF16)

    kpad = _dot(kn, place_ref[...]) + _dot(packed, rk_ref[...])
    qpad = _dot(qn, place_ref[...]) + _dot(packed, rq_ref[...])
    vt = v.T
    tk = vt_ref.shape[4]
    for h in range(H):
        kaug_ref[0, h] = kpad[:, h * AUG:(h + 1) * AUG].astype(BF16)
        qaugt_ref[0, h] = qpad[:, h * AUG:(h + 1) * AUG].T.astype(BF16)
        for cb in range(tm // tk):
            vt_ref[0, h, cb] = vt[h * HEAD_DIM:(h + 1) * HEAD_DIM, cb * tk:(cb + 1) * tk].astype(BF16)


def _fox_prep(fox, small, bf, qg, kg, ones_bd, place, rq, rk, tm, tk):
    B, S, _ = fox.shape
    H, W = FOX_HEADS, FOX_WIDTH
    return pl.pallas_call(
        _fox_prep_kernel,
        grid=(B, S // tm),
        in_specs=[
            pl.BlockSpec((1, tm, FOX_COLS), lambda b, i: (b, i, 0)),
            pl.BlockSpec((1, tm, SMALL_COLS), lambda b, i: (b, i, 0)),
            _const_spec((1, LANES)), _const_spec((1, W)), _const_spec((1, W)),
            _const_spec((W, W)), _const_spec((W, H * AUG)),
            _const_spec((LANES, H * AUG)), _const_spec((LANES, H * AUG)),
        ],
        out_specs=[
            pl.BlockSpec((1, H, tm, AUG), lambda b, i: (b, 0, i, 0)),
            pl.BlockSpec((1, H, AUG, tm), lambda b, i: (b, 0, 0, i)),
            pl.BlockSpec((1, H, tm // tk, HEAD_DIM, tk), lambda b, i: (b, 0, i, 0, 0)),
        ],
        out_shape=[
            jax.ShapeDtypeStruct((B, H, S, AUG), BF16),
            jax.ShapeDtypeStruct((B, H, AUG, S), BF16),
            jax.ShapeDtypeStruct((B, H, S // tk, HEAD_DIM, tk), BF16),
        ],
        scratch_shapes=[pltpu.VMEM((SUBLANES, LANES), F32)],
        compiler_params=_cparams(("arbitrary", "arbitrary")),
        name="fox_prep",
    )(fox, small, bf, qg, kg, ones_bd, place, rq, rk)


HEADS_PER_STEP = 2


def _fox_attn_kernel(k_ref, qt_ref, vt_ref, o_ref, *, tq, tk):
    qi = pl.program_id(2)
    n_full = (qi * tq) // tk
    n_diag = tq // tk
    outs = []
    for hh in range(HEADS_PER_STEP):
        qt = qt_ref[0, hh]

        def step(j, carry, masked):
            m, l, acc = carry
            start = pl.multiple_of(j * tk, tk)
            s = _dot(k_ref[0, hh, pl.ds(start, tk), :], qt)
            if masked:
                kv_pos = start + lax.broadcasted_iota(jnp.int32, (tk, tq), 0)
                q_pos = qi * tq + lax.broadcasted_iota(jnp.int32, (tk, tq), 1)
                s = jnp.where(kv_pos <= q_pos, s, NEG_BIG)
            m_new = jnp.maximum(m, jnp.max(s, axis=0, keepdims=True))
            alpha = jnp.exp(m - m_new)
            p = jnp.exp(s - m_new)
            l = alpha * l + jnp.sum(p, axis=0, keepdims=True)
            acc = alpha * acc + _dot(vt_ref[0, hh, j], p.astype(BF16))
            return m_new, l, acc

        carry = (jnp.full((1, tq), NEG_BIG, F32), jnp.zeros((1, tq), F32),
                 jnp.zeros((HEAD_DIM, tq), F32))
        carry = lax.fori_loop(0, n_full, functools.partial(step, masked=False), carry)
        for d in range(n_diag):
            carry = step(n_full + d, carry, True)
        _, l, acc = carry
        outs.append(acc / l)
    o_ref[0] = jnp.concatenate(outs, axis=0).T.astype(o_ref.dtype)


def _fox_attention(kaug, qaugt, vt, tq, tk):
    B, H, S, _ = kaug.shape
    hp = HEADS_PER_STEP
    return pl.pallas_call(
        functools.partial(_fox_attn_kernel, tq=tq, tk=tk),
        grid=(B, H // hp, S // tq),
        in_specs=[
            pl.BlockSpec((1, hp, S, AUG), lambda b, h, i: (b, h, 0, 0)),
            pl.BlockSpec((1, hp, AUG, tq), lambda b, h, i: (b, h, 0, i)),
            pl.BlockSpec((1, hp, S // tk, HEAD_DIM, tk), lambda b, h, i: (b, h, 0, 0, 0)),
        ],
        out_specs=pl.BlockSpec((1, tq, hp * HEAD_DIM), lambda b, h, i: (b, i, h)),
        out_shape=jax.ShapeDtypeStruct((B, S, H * HEAD_DIM), BF16),
        compiler_params=_cparams(("arbitrary", "arbitrary", "arbitrary")),
        name="fox_attention",
    )(kaug, qaugt, vt)


GLA_ROWS = 256


def _gla_kernel(gla_ref, small_ref, wa_ref, ba_ref, gain_ref, tri_ref, hmask_ref, bdmask_ref,
                ones_ref, o_ref, st_ref):
    ts = o_ref.shape[1]
    KW, C, R = GLA_KW, GLA_CHUNK, GLA_ROWS

    @pl.when(pl.program_id(1) == 0)
    def _():
        st_ref[...] = jnp.zeros_like(st_ref)

    for g0 in range(0, ts, R):
        rows = slice(g0, g0 + R)
        q = gla_ref[0, rows, 0:KW].astype(F32) * (GLA_DK ** -0.5)
        k = gla_ref[0, rows, KW:2 * KW].astype(F32)
        v = gla_ref[0, rows, 2 * KW:3 * KW]
        g = gla_ref[0, rows, 3 * KW:4 * KW].astype(F32)
        low = small_ref[0, rows, :].astype(BF16)
        la = _log_sigmoid(_dot(low, wa_ref[...]) + ba_ref[...]) * (1.0 / GLA_TAU)
        la_hi = la.astype(BF16)
        la_lo = (la - la_hi.astype(F32)).astype(BF16)
        bcum = _dot(tri_ref[...], la_hi) + _dot(tri_ref[...], la_lo)
        b_last = jnp.concatenate(
            [jnp.broadcast_to(bcum[c * C + C - 1:c * C + C, :], (C, KW)) for c in range(R // C)], axis=0)
        q_dec = (q * jnp.exp(bcum)).astype(BF16)
        k_dec = (k * jnp.exp(-bcum)).astype(BF16)
        k_end = (k * jnp.exp(b_last - bcum)).astype(BF16)

        o_acc = jnp.zeros((R, KW), F32)
        for h in range(GLA_HEADS):
            hm = hmask_ref[h:h + 1, :]
            att = lax.dot_general(q_dec * hm.astype(BF16), k_dec, (((1,), (1,)), ((), ())),
                                  preferred_element_type=F32)
            att = (att * tri_ref[...].astype(F32)).astype(BF16)
            o_acc = o_acc + _dot(att, v * hm.astype(BF16))

        o_parts = []
        for c in range(R // C):
            cs = slice(c * C, (c + 1) * C)
            st = st_ref[...]
            o_parts.append(lax.dot_general(q_dec[cs], st.astype(BF16), (((1,), (1,)), ((), ())),
                                           preferred_element_type=F32))
            kv_t = lax.dot_general(v[cs], k_end[cs], (((0,), (0,)), ((), ())),
                                   preferred_element_type=F32)
            decay = jnp.exp(b_last[c * C:c * C + 1, :])
            st_ref[...] = st * decay + kv_t * bdmask_ref[...]
        o = o_acc + jnp.concatenate(o_parts, axis=0)

        ms = _dot((o * o).astype(BF16), ones_ref[...]) * (1.0 / GLA_DK)
        o = o * lax.rsqrt(ms + EPS) * gain_ref[...]
        o_ref[0, rows, :] = (o * (g * _sigmoid(g))).astype(o_ref.dtype)


def _gla_mixer(gla, small, wa, ba, gain, tri, hmask, bdmask, ones_bd, ts):
    B, S, _ = gla.shape
    KW = GLA_KW
    return pl.pallas_call(
        _gla_kernel,
        grid=(B, S // ts),
        in_specs=[
            pl.BlockSpec((1, ts, GLA_COLS), lambda b, i: (b, i, 0)),
            pl.BlockSpec((1, ts, SMALL_COLS), lambda b, i: (b, i, 0)),
            _const_spec((LANES, KW)), _const_spec((1, KW)), _const_spec((1, KW)),
            _const_spec((GLA_ROWS, GLA_ROWS)), _const_spec((GLA_HEADS, KW)),
            _const_spec((KW, KW)), _const_spec((KW, KW)),
        ],
        out_specs=pl.BlockSpec((1, ts, KW), lambda b, i: (b, i, 0)),
        out_shape=jax.ShapeDtypeStruct((B, S, KW), BF16),
        scratch_shapes=[pltpu.VMEM((KW, KW), F32)],
        compiler_params=_cparams(("arbitrary", "arbitrary")),
        name="gla_mixer",
    )(gla, small, wa, ba, gain, tri, hmask, bdmask, ones_bd)


def _ffn_chunks(d_ff):
    chunks, c0 = [], 0
    while c0 < d_ff:
        n = min(512, d_ff - c0)
        chunks.append((c0, n))
        c0 += n
    return chunks


def _out_ffn_kernel(x_ref, a_ref, b_ref, c_ref, mod_ref, g_ref, wo_ref, wgu_ref, wd_ref, o_ref,
                    acc_ref):
    d_ff = wd_ref.shape[0]
    wa, wb = a_ref.shape[2], b_ref.shape[2]
    mix = (_dot(a_ref[0], wo_ref[0:wa, :]) + _dot(b_ref[0], wo_ref[wa:wa + wb, :])
           + _dot(c_ref[0], wo_ref[wa + wb:, :]))
    x1 = x_ref[0] + mod_ref[0, 2:3, :] * mix
    h = _rms_mod(x1, g_ref[...], mod_ref[0, 4:5, :], mod_ref[0, 3:4, :]).astype(BF16)
    for idx, (c0, n) in enumerate(_ffn_chunks(d_ff)):
        gt = _dot(h, wgu_ref[:, c0:c0 + n])
        up = _dot(h, wgu_ref[:, d_ff + c0:d_ff + c0 + n])
        act = (gt * _sigmoid(gt) * up).astype(BF16)
        y = _dot(act, wd_ref[c0:c0 + n, :])
        if idx == 0:
            acc_ref[...] = y
        else:
            acc_ref[...] += y
    o_ref[0] = x1 + mod_ref[0, 5:6, :] * acc_ref[...]


def _out_ffn(x, a, b, c, mod, gain, wo, wgu, wd, tm):
    B, S, D = x.shape
    tok = lambda n: pl.BlockSpec((1, tm, n), lambda bb, i: (bb, i, 0))
    return pl.pallas_call(
        _out_ffn_kernel,
        grid=(B, S // tm),
        in_specs=[
            tok(D), tok(a.shape[2]), tok(b.shape[2]), tok(c.shape[2]),
            pl.BlockSpec((1, N_MOD, D), lambda bb, i: (bb, 0, 0)),
            _const_spec((1, D)),
            _resident_spec(wo.shape), _resident_spec(wgu.shape), _resident_spec(wd.shape),
        ],
        out_specs=tok(D),
        out_shape=jax.ShapeDtypeStruct((B, S, D), F32),
        scratch_shapes=[pltpu.VMEM((tm, D), F32)],
        compiler_params=_cparams(("arbitrary", "arbitrary")),
        name="out_proj_ffn",
    )(x, a, b, c, mod, gain, wo, wgu, wd)


def _block_diag(w):
    L, n, d, e = w.shape
    eye = jnp.eye(n, dtype=w.dtype)
    return jnp.einsum("lnde,nm->lndme", w, eye).reshape(L, n * d, n * e)


def _regroup_w_in(w_in):
    L, D, _ = w_in.shape
    sizes = (LRU_WIDTH, LRU_WIDTH, FOX_WIDTH, FOX_WIDTH, FOX_WIDTH, FOX_HEADS,
             GLA_KW, GLA_KW, GLA_VW, GLA_RANK, GLA_VW)
    offs = np.concatenate([[0], np.cumsum(sizes)])
    seg = lambda i: w_in[:, :, offs[i]:offs[i + 1]]
    z = lambda n: jnp.zeros((L, D, n), w_in.dtype)
    cols = [seg(0), seg(1), seg(2), seg(3), seg(4), seg(6), seg(7), seg(8), seg(10),
            seg(5), z(GLOW_LANE0 - FOX_HEADS), seg(9), z(SMALL_COLS - GLOW_LANE0 - GLA_RANK)]
    return jnp.concatenate(cols, axis=-1).astype(BF16)


def _head_ones(n_heads, dim):
    return jnp.asarray(np.kron(np.eye(n_heads), np.ones((dim, dim))), BF16)


def _fox_constants():
    H, W = FOX_HEADS, FOX_WIDTH
    place = np.zeros((W, H * AUG), np.float32)
    rq = np.zeros((LANES, H * AUG), np.float32)
    rk = np.zeros((LANES, H * AUG), np.float32)
    for h in range(H):
        for d in range(HEAD_DIM):
            place[h * HEAD_DIM + d, h * AUG + d] = 1.0
        for s in range(N_SPLIT):
            rq[s * H + h, h * AUG + HEAD_DIM + s] = 1.0
            rq[ONE_LANE, h * AUG + HEAD_DIM + N_SPLIT + s] = 1.0
            rk[ONE_LANE, h * AUG + HEAD_DIM + s] = 1.0
            rk[s * H + h, h * AUG + HEAD_DIM + N_SPLIT + s] = -1.0
    return jnp.asarray(place, BF16), jnp.asarray(rq, BF16), jnp.asarray(rk, BF16)


def _gla_constants():
    R, C, KW = GLA_ROWS, GLA_CHUNK, GLA_KW
    i = np.arange(R)
    tri = ((i[:, None] >= i[None, :]) & (i[:, None] // C == i[None, :] // C)).astype(np.float32)
    lane = np.arange(KW)
    hmask = (lane[None, :] // GLA_DK == np.arange(GLA_HEADS)[:, None]).astype(np.float32)
    bdmask = (lane[:, None] // GLA_DK == lane[None, :] // GLA_DK).astype(np.float32)
    return jnp.asarray(tri, BF16), jnp.asarray(hmask, F32), jnp.asarray(bdmask, F32)


def _pick_tile(n, want):
    t = min(n, want)
    while n % t:
        t //= 2
    return t


def kernel(x, c, norm1_gain, norm2_gain, w_mod, b_mod, w_in, conv_w, conv_b, lru_w_r, lru_b_r,
           lru_w_i, lru_b_i, lru_lambda, fox_b_f, fox_q_gain, fox_k_gain, gla_w_alpha, gla_b_alpha,
           gla_out_gain, w_out, ffn_w_gate_up, ffn_w_down):
    B, S, D = x.shape
    L = w_in.shape[0]
    tm = _pick_tile(S, 512)
    tq = _pick_tile(S, 512)
    tk = _pick_tile(S, 256)

    mod = _modulation(c, w_mod, b_mod)
    w_in_p = _regroup_w_in(w_in)
    wr_bd = _block_diag(lru_w_r).astype(BF16)
    wi_bd = _block_diag(lru_w_i).astype(BF16)
    w_out_b = w_out.astype(BF16)
    wgu_b = ffn_w_gate_up.astype(BF16)
    wd_b = ffn_w_down.astype(BF16)
    row = lambda a: a.reshape(L, 1, a.shape[-1])
    bf_pad = jnp.pad(fox_b_f, ((0, 0), (0, LANES - FOX_HEADS))).reshape(L, 1, LANES)
    qg = jnp.tile(fox_q_gain * (HEAD_DIM ** -0.5), (1, FOX_HEADS)).reshape(L, 1, FOX_WIDTH)
    kg = jnp.tile(fox_k_gain, (1, FOX_HEADS)).reshape(L, 1, FOX_WIDTH)
    wa_pad = jnp.pad(gla_w_alpha, ((0, 0), (GLOW_LANE0, LANES - GLOW_LANE0 - GLA_RANK), (0, 0))).astype(BF16)
    og = jnp.tile(gla_out_gain, (1, GLA_HEADS)).reshape(L, 1, GLA_VW)
    fox_ones = _head_ones(FOX_HEADS, HEAD_DIM)
    gla_ones = _head_ones(GLA_HEADS, GLA_DK)
    place, rq, rk = _fox_constants()
    tri, hmask, bdmask = _gla_constants()

    for l in range(L):
        lru, fox, gla, small = _in_proj(x, mod[l], row(norm1_gain)[l], w_in_p[l], tm)
        out_a = _lru_mixer(lru, conv_w[l], row(conv_b)[l], wr_bd[l], row(lru_b_r)[l], wi_bd[l],
                           row(lru_b_i)[l], row(lru_lambda)[l], tm)
        kaug, qaugt, vt = _fox_prep(fox, small, bf_pad[l], qg[l], kg[l], fox_ones, place, rq, rk, tm, tk)
        out_b = _fox_attention(kaug, qaugt, vt, tq, tk)
        out_c = _gla_mixer(gla, small, wa_pad[l], row(gla_b_alpha)[l], og[l], tri, hmask, bdmask,
                           gla_ones, tm)
        x = _out_ffn(x, out_a, out_b, out_c, mod[l], row(norm2_gain)[l], w_out_b[l], wgu_b[l],
                     wd_b[l], tm)
    return x
```

```python
import functools

import numpy as np
import jax
import jax.numpy as jnp
from jax import lax
from jax.experimental import pallas as pl
from jax.experimental.pallas import tpu as pltpu

F32 = jnp.float32
BF16 = jnp.bfloat16

EPS = 1e-6
HEAD_DIM = 64
LRU_WIDTH = 384
LRU_BLOCKS = 6
CONV_WIDTH = 4
LRU_C = 8.0
FOX_HEADS = 6
FOX_WIDTH = FOX_HEADS * HEAD_DIM
GLA_HEADS = 4
GLA_DK = 64
GLA_KW = GLA_HEADS * GLA_DK
GLA_VW = GLA_KW
GLA_RANK = 16
GLA_TAU = 16.0
GLA_CHUNK = 64
N_MOD = 6

LANES = 128
SUBLANES = 8
VMEM_LIMIT_BYTES = 56 * 1024 * 1024

LRU_COLS = 2 * LRU_WIDTH
FOX_COLS = 3 * FOX_WIDTH
GLA_COLS = 3 * GLA_KW + GLA_VW
SMALL_COLS = LANES
GLOW_LANE0 = 8
NP_COLS = LRU_COLS + FOX_COLS + GLA_COLS + SMALL_COLS

AUG = LANES
N_SPLIT = 3
ONE_LANE = N_SPLIT * FOX_HEADS
NEG_BIG = -1e30
V_ROWS = HEAD_DIM + 16
LOG2E = float(np.log2(np.e))


def _sigmoid(x):
    return 1.0 / (1.0 + jnp.exp(-x))


def _log_sigmoid(x):
    return jnp.minimum(x, 0.0) - jnp.log(1.0 + jnp.exp(-jnp.abs(x)))


def _softplus(x):
    return jnp.maximum(x, 0.0) + jnp.log(1.0 + jnp.exp(-jnp.abs(x)))


def _gelu_tanh(x):
    c = np.float32(np.sqrt(2.0 / np.pi))
    return 0.5 * x * (1.0 + jnp.tanh(c * (x + 0.044715 * (x * x * x))))


def _dot(a, b):
    return jnp.dot(a, b, preferred_element_type=F32)


def _cparams(sem):
    return pltpu.CompilerParams(dimension_semantics=sem, vmem_limit_bytes=VMEM_LIMIT_BYTES)


def _const_spec(shape):
    nd = len(shape)
    return pl.BlockSpec(shape, lambda *_: (0,) * nd)


def _resident_spec(shape):
    nd = len(shape)
    return pl.BlockSpec(shape, lambda *_: (0,) * nd, pipeline_mode=pl.Buffered(1))


def _mod_kernel(c_ref, w_ref, b_ref, o_ref):
    c = c_ref[...]
    ca = (c * _sigmoid(c)).astype(BF16)
    o_ref[0] = _dot(ca, w_ref[0].astype(BF16)) + b_ref[0]


def _modulation(c, w_mod, b_mod):
    L, D, N = w_mod.shape
    B = c.shape[0]
    bp = -(-B // SUBLANES) * SUBLANES
    cp = jnp.pad(c, ((0, bp - B), (0, 0)))
    tn = 1024
    out = pl.pallas_call(
        _mod_kernel,
        grid=(L, N // tn),
        in_specs=[
            pl.BlockSpec((bp, D), lambda l, n: (0, 0)),
            pl.BlockSpec((1, D, tn), lambda l, n: (l, 0, n)),
            pl.BlockSpec((1, 1, tn), lambda l, n: (l, 0, n)),
        ],
        out_specs=pl.BlockSpec((1, bp, tn), lambda l, n: (l, 0, n)),
        out_shape=jax.ShapeDtypeStruct((L, bp, N), F32),
        compiler_params=_cparams(("arbitrary", "arbitrary")),
        name="adaln_mod",
    )(cp, w_mod, b_mod.reshape(L, 1, N))
    return out[:, :B].reshape(L, B, N_MOD, D)


def _rms_mod(x, gain, scale, shift):
    ms = jnp.mean(x * x, axis=-1, keepdims=True)
    y = x * lax.rsqrt(ms + EPS) * gain
    return y * (1.0 + scale) + shift


def _in_proj_kernel(x_ref, mod_ref, g_ref, w_ref, lru_ref, fox_ref, gla_ref, small_ref):
    h = _rms_mod(x_ref[0], g_ref[...], mod_ref[0, 1:2, :], mod_ref[0, 0:1, :]).astype(BF16)
    c0 = 0
    for ref, n in ((lru_ref, LRU_COLS), (fox_ref, FOX_COLS), (gla_ref, GLA_COLS), (small_ref, SMALL_COLS)):
        ref[0] = _dot(h, w_ref[:, c0:c0 + n]).astype(ref.dtype)
        c0 += n


def _in_proj(x, mod, gain, w, tm):
    B, S, D = x.shape
    tok = lambda n: pl.BlockSpec((1, tm, n), lambda b, i: (b, i, 0))
    return pl.pallas_call(
        _in_proj_kernel,
        grid=(B, S // tm),
        in_specs=[
            tok(D),
            pl.BlockSpec((1, N_MOD, D), lambda b, i: (b, 0, 0)),
            _const_spec((1, D)),
            _resident_spec((D, NP_COLS)),
        ],
        out_specs=[tok(LRU_COLS), tok(FOX_COLS), tok(GLA_COLS), tok(SMALL_COLS)],
        out_shape=[
            jax.ShapeDtypeStruct((B, S, LRU_COLS), BF16),
            jax.ShapeDtypeStruct((B, S, FOX_COLS), BF16),
            jax.ShapeDtypeStruct((B, S, GLA_COLS), BF16),
            jax.ShapeDtypeStruct((B, S, SMALL_COLS), F32),
        ],
        compiler_params=_cparams(("arbitrary", "arbitrary")),
        name="in_proj",
    )(x, mod, gain, w)


def _scan_add(v):
    n = v.shape[0]
    row = lax.broadcasted_iota(jnp.int32, v.shape, 0)
    d = 1
    while d < n:
        v = v + jnp.where(row >= d, pltpu.roll(v, d, axis=0), 0.0)
        d *= 2
    return v


def _scan_linear(a, u):
    n = a.shape[0]
    row = lax.broadcasted_iota(jnp.int32, a.shape, 0)
    d = 1
    while d < n:
        m = row >= d
        u = u + jnp.where(m, a * pltpu.roll(u, d, axis=0), 0.0)
        a = a * jnp.where(m, pltpu.roll(a, d, axis=0), 1.0)
        d *= 2
    return a, u


def _lru_kernel(lru_ref, cw_ref, cb_ref, wr_ref, br_ref, wi_ref, bi_ref, lam_ref, o_ref,
                xs_ref, h_ref):
    ts = o_ref.shape[1]
    W = LRU_WIDTH

    @pl.when(pl.program_id(1) == 0)
    def _():
        xs_ref[0:SUBLANES, :] = jnp.zeros((SUBLANES, W), F32)
        h_ref[...] = jnp.zeros_like(h_ref)

    xa = lru_ref[0, :, 0:W].astype(F32)
    ya = lru_ref[0, :, W:2 * W].astype(F32)
    xs_ref[SUBLANES:SUBLANES + ts, :] = xa
    xc = xa * cw_ref[CONV_WIDTH - 1:CONV_WIDTH, :] + cb_ref[...]
    for k in range(1, CONV_WIDTH):
        j = CONV_WIDTH - 1 - k
        xc = xc + xs_ref[pl.ds(SUBLANES - k, ts), :] * cw_ref[j:j + 1, :]
    xs_ref[0:SUBLANES, :] = xs_ref[ts:ts + SUBLANES, :]

    xcb = xc.astype(BF16)
    r = _sigmoid(_dot(xcb, wr_ref[...]) + br_ref[...])
    i = _sigmoid(_dot(xcb, wi_ref[...]) + bi_ref[...])
    log_a = (-LRU_C) * r * _softplus(-lam_ref[...])
    a = jnp.exp(log_a)
    mult = jnp.sqrt(1.0 - a * a)
    u = mult * (i * xc)
    acum, hloc = _scan_linear(a, u)
    h = acum * h_ref[0:1, :] + hloc
    h_ref[...] = jnp.broadcast_to(h[ts - 1:ts, :], h_ref.shape)
    o_ref[0] = (h * _gelu_tanh(ya)).astype(o_ref.dtype)


def _lru_mixer(lru, cw, cb, wr, br, wi, bi, lam, ts):
    B, S, _ = lru.shape
    W = LRU_WIDTH
    return pl.pallas_call(
        _lru_kernel,
        grid=(B, S // ts),
        in_specs=[
            pl.BlockSpec((1, ts, LRU_COLS), lambda b, i: (b, i, 0)),
            _const_spec((CONV_WIDTH, W)), _const_spec((1, W)),
            _const_spec((W, W)), _const_spec((1, W)),
            _const_spec((W, W)), _const_spec((1, W)),
            _const_spec((1, W)),
        ],
        out_specs=pl.BlockSpec((1, ts, W), lambda b, i: (b, i, 0)),
        out_shape=jax.ShapeDtypeStruct((B, S, W), BF16),
        scratch_shapes=[pltpu.VMEM((ts + SUBLANES, W), F32), pltpu.VMEM((SUBLANES, W), F32)],
        compiler_params=_cparams(("arbitrary", "arbitrary")),
        name="lru_mixer",
    )(lru, cw, cb, wr, br, wi, bi, lam)


def _fox_prep_kernel(fox_ref, small_ref, bf_ref, qg_ref, kg_ref, ones_ref, place_ref,
                     rq_ref, rk_ref, kaug_ref, qaugt_ref, vt_ref, cum_ref):
    tm = fox_ref.shape[1]
    W = FOX_WIDTH

    @pl.when(pl.program_id(1) == 0)
    def _():
        cum_ref[...] = jnp.zeros_like(cum_ref)

    q = fox_ref[0, :, 0:W].astype(F32)
    k = fox_ref[0, :, W:2 * W].astype(F32)
    v = fox_ref[0, :, 2 * W:3 * W].astype(F32)

    def head_norm(t, gain):
        ms = _dot((t * t).astype(BF16), ones_ref[...]) * (1.0 / HEAD_DIM)
        return (t * lax.rsqrt(ms + EPS) * gain).astype(BF16)

    qn = head_norm(q, qg_ref[...])
    kn = head_norm(k, kg_ref[...])

    lf = _log_sigmoid(small_ref[0] + bf_ref[...])
    cum = _scan_add(lf) + cum_ref[0:1, :]
    cum_ref[...] = jnp.broadcast_to(cum[tm - 1:tm, :], cum_ref.shape)

    cs = cum * LOG2E
    c1 = cs.astype(BF16).astype(F32)
    r1 = cs - c1
    c2 = r1.astype(BF16).astype(F32)
    c3 = (r1 - c2).astype(BF16).astype(F32)
    lane = lax.broadcasted_iota(jnp.int32, cum.shape, 1)
    H = FOX_HEADS
    packed = jnp.where(lane < H, c1, 0.0)
    packed = jnp.where((lane >= H) & (lane < 2 * H), pltpu.roll(c2, H, axis=1), packed)
    packed = jnp.where((lane >= 2 * H) & (lane < 3 * H), pltpu.roll(c3, 2 * H, axis=1), packed)
    packed = jnp.where(lane == ONE_LANE, 1.0, packed).astype(BF16)

    kpad = _dot(kn, place_ref[...]) + _dot(packed, rk_ref[...])
    qpad = _dot(qn, place_ref[...]) + _dot(packed, rq_ref[...])
    vt = v.T
    tk = vt_ref.shape[4]
    for h in range(H):
        kaug_ref[0, h] = kpad[:, h * AUG:(h + 1) * AUG].astype(BF16)
        qaugt_ref[0, h] = qpad[:, h * AUG:(h + 1) * AUG].T.astype(BF16)
        for cb in range(tm // tk):
            vt_ref[0, h, cb, 0:HEAD_DIM, :] = (
                vt[h * HEAD_DIM:(h + 1) * HEAD_DIM, cb * tk:(cb + 1) * tk].astype(BF16))
            vt_ref[0, h, cb, HEAD_DIM:V_ROWS, :] = jnp.ones((V_ROWS - HEAD_DIM, tk), BF16)


def _fox_prep(fox, small, bf, qg, kg, ones_bd, place, rq, rk, tm, tk):
    B, S, _ = fox.shape
    H, W = FOX_HEADS, FOX_WIDTH
    return pl.pallas_call(
        _fox_prep_kernel,
        grid=(B, S // tm),
        in_specs=[
            pl.BlockSpec((1, tm, FOX_COLS), lambda b, i: (b, i, 0)),
            pl.BlockSpec((1, tm, SMALL_COLS), lambda b, i: (b, i, 0)),
            _const_spec((1, LANES)), _const_spec((1, W)), _const_spec((1, W)),
            _const_spec((W, W)), _const_spec((W, H * AUG)),
            _const_spec((LANES, H * AUG)), _const_spec((LANES, H * AUG)),
        ],
        out_specs=[
            pl.BlockSpec((1, H, tm, AUG), lambda b, i: (b, 0, i, 0)),
            pl.BlockSpec((1, H, AUG, tm), lambda b, i: (b, 0, 0, i)),
            pl.BlockSpec((1, H, tm // tk, V_ROWS, tk), lambda b, i: (b, 0, i, 0, 0)),
        ],
        out_shape=[
            jax.ShapeDtypeStruct((B, H, S, AUG), BF16),
            jax.ShapeDtypeStruct((B, H, AUG, S), BF16),
            jax.ShapeDtypeStruct((B, H, S // tk, V_ROWS, tk), BF16),
        ],
        scratch_shapes=[pltpu.VMEM((SUBLANES, LANES), F32)],
        compiler_params=_cparams(("arbitrary", "arbitrary")),
        name="fox_prep",
    )(fox, small, bf, qg, kg, ones_bd, place, rq, rk)


def _fox_attn_kernel(k_ref, qt_ref, vt_ref, o_ref, acc_ref, sa_ref, sb_ref, *, tq):
    qi = pl.program_id(1)
    H = FOX_HEADS
    last_block = vt_ref.shape[2] - 1
    acc_ref[...] = jnp.zeros_like(acc_ref)

    def scores(j, s_ref):
        start = pl.multiple_of(jnp.minimum(j, last_block) * tq, tq)
        for h in range(H):
            s_ref[h] = _dot(k_ref[0, h, pl.ds(start, tq), :], qt_ref[0, h])

    def softmax(s_ref, ms, mask):
        new, ps = [], []
        for h in range(H):
            s = s_ref[h]
            if mask is not None:
                s = jnp.where(mask, s, NEG_BIG)
            m_new = jnp.maximum(ms[h], jnp.max(s, axis=0, keepdims=True))
            new.append(m_new)
            ps.append((jnp.exp2(ms[h] - m_new), jnp.exp2(s - m_new).astype(BF16)))
        return tuple(new), ps

    def values(j, ps):
        jc = jnp.minimum(j, last_block)
        for h in range(H):
            alpha, p = ps[h]
            acc_ref[h] = alpha * acc_ref[h] + _dot(vt_ref[0, h, jc], p)

    def pair(t, ml, masks, prefetch):
        j0 = 2 * t
        ml, ps = softmax(sa_ref, ml, masks[0])
        scores(j0 + 1, sb_ref)
        values(j0, ps)
        ml, ps = softmax(sb_ref, ml, masks[1])
        if prefetch:
            scores(j0 + 2, sa_ref)
        values(j0 + 1, ps)
        return ml

    scores(0, sa_ref)
    ml = tuple(jnp.full((1, tq), NEG_BIG, F32) for _ in range(H))
    n_pairs = lax.div(qi, 2)
    ml = lax.fori_loop(0, n_pairs, lambda t, c: pair(t, c, (None, None), True), ml)
    rel = (lax.broadcasted_iota(jnp.int32, (tq, tq), 0)
           - lax.broadcasted_iota(jnp.int32, (tq, tq), 1))
    j0 = 2 * n_pairs
    masks = (rel <= (qi - j0) * tq, rel <= (qi - j0 - 1) * tq)
    ml = pair(n_pairs, ml, masks, False)
    out = jnp.concatenate(
        [acc_ref[h, 0:HEAD_DIM, :] * (1.0 / acc_ref[h, HEAD_DIM:HEAD_DIM + 1, :]) for h in range(H)], axis=0)
    o_ref[0] = out.T.astype(o_ref.dtype)


def _fox_attention(kaug, qaugt, vt, tq):
    B, H, S, _ = kaug.shape
    return pl.pallas_call(
        functools.partial(_fox_attn_kernel, tq=tq),
        grid=(B, S // tq),
        in_specs=[
            pl.BlockSpec((1, H, S, AUG), lambda b, i: (b, 0, 0, 0), pipeline_mode=pl.Buffered(1)),
            pl.BlockSpec((1, H, AUG, tq), lambda b, i: (b, 0, 0, i)),
            pl.BlockSpec((1, H, S // tq, V_ROWS, tq), lambda b, i: (b, 0, 0, 0, 0),
                         pipeline_mode=pl.Buffered(1)),
        ],
        out_specs=pl.BlockSpec((1, tq, H * HEAD_DIM), lambda b, i: (b, i, 0)),
        out_shape=jax.ShapeDtypeStruct((B, S, H * HEAD_DIM), BF16),
        scratch_shapes=[pltpu.VMEM((H, V_ROWS, tq), F32), pltpu.VMEM((H, tq, tq), F32),
                        pltpu.VMEM((H, tq, tq), F32)],
        compiler_params=_cparams(("arbitrary", "arbitrary")),
        name="fox_attention",
    )(kaug, qaugt, vt)


GLA_ROWS = 256


def _gla_kernel(gla_ref, small_ref, wa_ref, ba_ref, gain_ref, tri_ref, hmask_ref, bdmask_ref,
                ones_ref, o_ref, st_ref):
    ts = o_ref.shape[1]
    KW, C, R = GLA_KW, GLA_CHUNK, GLA_ROWS

    @pl.when(pl.program_id(1) == 0)
    def _():
        st_ref[...] = jnp.zeros_like(st_ref)

    for g0 in range(0, ts, R):
        rows = slice(g0, g0 + R)
        q = gla_ref[0, rows, 0:KW].astype(F32) * (GLA_DK ** -0.5)
        k = gla_ref[0, rows, KW:2 * KW].astype(F32)
        v = gla_ref[0, rows, 2 * KW:3 * KW]
        g = gla_ref[0, rows, 3 * KW:4 * KW].astype(F32)
        low = small_ref[0, rows, :].astype(BF16)
        la = _log_sigmoid(_dot(low, wa_ref[...]) + ba_ref[...]) * (1.0 / GLA_TAU)
        la_hi = la.astype(BF16)
        la_lo = (la - la_hi.astype(F32)).astype(BF16)
        bcum = _dot(tri_ref[...], la_hi) + _dot(tri_ref[...], la_lo)
        b_last = jnp.concatenate(
            [jnp.broadcast_to(bcum[c * C + C - 1:c * C + C, :], (C, KW)) for c in range(R // C)], axis=0)
        q_dec = (q * jnp.exp(bcum)).astype(BF16)
        k_dec = (k * jnp.exp(-bcum)).astype(BF16)
        k_end = (k * jnp.exp(b_last - bcum)).astype(BF16)

        o_acc = jnp.zeros((R, KW), F32)
        for h in range(GLA_HEADS):
            hm = hmask_ref[h:h + 1, :]
            att = lax.dot_general(q_dec * hm.astype(BF16), k_dec, (((1,), (1,)), ((), ())),
                                  preferred_element_type=F32)
            att = (att * tri_ref[...].astype(F32)).astype(BF16)
            o_acc = o_acc + _dot(att, v * hm.astype(BF16))

        o_parts = []
        for c in range(R // C):
            cs = slice(c * C, (c + 1) * C)
            st = st_ref[...]
            o_parts.append(lax.dot_general(q_dec[cs], st.astype(BF16), (((1,), (1,)), ((), ())),
                                           preferred_element_type=F32))
            kv_t = lax.dot_general(v[cs], k_end[cs], (((0,), (0,)), ((), ())),
                                   preferred_element_type=F32)
            decay = jnp.exp(b_last[c * C:c * C + 1, :])
            st_ref[...] = st * decay + kv_t * bdmask_ref[...]
        o = o_acc + jnp.concatenate(o_parts, axis=0)

        ms = _dot((o * o).astype(BF16), ones_ref[...]) * (1.0 / GLA_DK)
        o = o * lax.rsqrt(ms + EPS) * gain_ref[...]
        o_ref[0, rows, :] = (o * (g * _sigmoid(g))).astype(o_ref.dtype)


def _gla_mixer(gla, small, wa, ba, gain, tri, hmask, bdmask, ones_bd, ts):
    B, S, _ = gla.shape
    KW = GLA_KW
    return pl.pallas_call(
        _gla_kernel,
        grid=(B, S // ts),
        in_specs=[
            pl.BlockSpec((1, ts, GLA_COLS), lambda b, i: (b, i, 0)),
            pl.BlockSpec((1, ts, SMALL_COLS), lambda b, i: (b, i, 0)),
            _const_spec((LANES, KW)), _const_spec((1, KW)), _const_spec((1, KW)),
            _const_spec((GLA_ROWS, GLA_ROWS)), _const_spec((GLA_HEADS, KW)),
            _const_spec((KW, KW)), _const_spec((KW, KW)),
        ],
        out_specs=pl.BlockSpec((1, ts, KW), lambda b, i: (b, i, 0)),
        out_shape=jax.ShapeDtypeStruct((B, S, KW), BF16),
        scratch_shapes=[pltpu.VMEM((KW, KW), F32)],
        compiler_params=_cparams(("arbitrary", "arbitrary")),
        name="gla_mixer",
    )(gla, small, wa, ba, gain, tri, hmask, bdmask, ones_bd)


def _ffn_chunks(d_ff):
    chunks, c0 = [], 0
    while c0 < d_ff:
        n = min(512, d_ff - c0)
        chunks.append((c0, n))
        c0 += n
    return chunks


def _out_ffn_kernel(x_ref, a_ref, b_ref, c_ref, mod_ref, g_ref, wo_ref, wgu_ref, wd_ref, o_ref,
                    acc_ref):
    d_ff = wd_ref.shape[0]
    wa, wb = a_ref.shape[2], b_ref.shape[2]
    mix = (_dot(a_ref[0], wo_ref[0:wa, :]) + _dot(b_ref[0], wo_ref[wa:wa + wb, :])
           + _dot(c_ref[0], wo_ref[wa + wb:, :]))
    x1 = x_ref[0] + mod_ref[0, 2:3, :] * mix
    h = _rms_mod(x1, g_ref[...], mod_ref[0, 4:5, :], mod_ref[0, 3:4, :]).astype(BF16)
    for idx, (c0, n) in enumerate(_ffn_chunks(d_ff)):
        gt = _dot(h, wgu_ref[:, c0:c0 + n])
        up = _dot(h, wgu_ref[:, d_ff + c0:d_ff + c0 + n])
        act = (gt * _sigmoid(gt) * up).astype(BF16)
        y = _dot(act, wd_ref[c0:c0 + n, :])
        if idx == 0:
            acc_ref[...] = y
        else:
            acc_ref[...] += y
    o_ref[0] = x1 + mod_ref[0, 5:6, :] * acc_ref[...]


def _out_ffn(x, a, b, c, mod, gain, wo, wgu, wd, tm):
    B, S, D = x.shape
    tok = lambda n: pl.BlockSpec((1, tm, n), lambda bb, i: (bb, i, 0))
    return pl.pallas_call(
        _out_ffn_kernel,
        grid=(B, S // tm),
        in_specs=[
            tok(D), tok(a.shape[2]), tok(b.shape[2]), tok(c.shape[2]),
            pl.BlockSpec((1, N_MOD, D), lambda bb, i: (bb, 0, 0)),
            _const_spec((1, D)),
            _resident_spec(wo.shape), _resident_spec(wgu.shape), _resident_spec(wd.shape),
        ],
        out_specs=tok(D),
        out_shape=jax.ShapeDtypeStruct((B, S, D), F32),
        scratch_shapes=[pltpu.VMEM((tm, D), F32)],
        compiler_params=_cparams(("arbitrary", "arbitrary")),
        name="out_proj_ffn",
    )(x, a, b, c, mod, gain, wo, wgu, wd)


def _block_diag(w):
    L, n, d, e = w.shape
    eye = jnp.eye(n, dtype=w.dtype)
    return jnp.einsum("lnde,nm->lndme", w, eye).reshape(L, n * d, n * e)


def _regroup_w_in(w_in):
    L, D, _ = w_in.shape
    sizes = (LRU_WIDTH, LRU_WIDTH, FOX_WIDTH, FOX_WIDTH, FOX_WIDTH, FOX_HEADS,
             GLA_KW, GLA_KW, GLA_VW, GLA_RANK, GLA_VW)
    offs = np.concatenate([[0], np.cumsum(sizes)])
    seg = lambda i: w_in[:, :, offs[i]:offs[i + 1]]
    z = lambda n: jnp.zeros((L, D, n), w_in.dtype)
    cols = [seg(0), seg(1), seg(2), seg(3), seg(4), seg(6), seg(7), seg(8), seg(10),
            seg(5), z(GLOW_LANE0 - FOX_HEADS), seg(9), z(SMALL_COLS - GLOW_LANE0 - GLA_RANK)]
    return jnp.concatenate(cols, axis=-1).astype(BF16)


def _head_ones(n_heads, dim):
    return jnp.asarray(np.kron(np.eye(n_heads), np.ones((dim, dim))), BF16)


def _fox_constants():
    H, W = FOX_HEADS, FOX_WIDTH
    place = np.zeros((W, H * AUG), np.float32)
    rq = np.zeros((LANES, H * AUG), np.float32)
    rk = np.zeros((LANES, H * AUG), np.float32)
    for h in range(H):
        for d in range(HEAD_DIM):
            place[h * HEAD_DIM + d, h * AUG + d] = 1.0
        for s in range(N_SPLIT):
            rq[s * H + h, h * AUG + HEAD_DIM + s] = 1.0
            rq[ONE_LANE, h * AUG + HEAD_DIM + N_SPLIT + s] = 1.0
            rk[ONE_LANE, h * AUG + HEAD_DIM + s] = 1.0
            rk[s * H + h, h * AUG + HEAD_DIM + N_SPLIT + s] = -1.0
    return jnp.asarray(place, BF16), jnp.asarray(rq, BF16), jnp.asarray(rk, BF16)


def _gla_constants():
    R, C, KW = GLA_ROWS, GLA_CHUNK, GLA_KW
    i = np.arange(R)
    tri = ((i[:, None] >= i[None, :]) & (i[:, None] // C == i[None, :] // C)).astype(np.float32)
    lane = np.arange(KW)
    hmask = (lane[None, :] // GLA_DK == np.arange(GLA_HEADS)[:, None]).astype(np.float32)
    bdmask = (lane[:, None] // GLA_DK == lane[None, :] // GLA_DK).astype(np.float32)
    return jnp.asarray(tri, BF16), jnp.asarray(hmask, F32), jnp.asarray(bdmask, F32)


def _pick_tile(n, want):
    t = min(n, want)
    while n % t:
        t //= 2
    return t


def kernel(x, c, norm1_gain, norm2_gain, w_mod, b_mod, w_in, conv_w, conv_b, lru_w_r, lru_b_r,
           lru_w_i, lru_b_i, lru_lambda, fox_b_f, fox_q_gain, fox_k_gain, gla_w_alpha, gla_b_alpha,
           gla_out_gain, w_out, ffn_w_gate_up, ffn_w_down):
    B, S, D = x.shape
    L = w_in.shape[0]
    tm = _pick_tile(S, 512)
    tq = _pick_tile(S, 256)

    mod = _modulation(c, w_mod, b_mod)
    w_in_p = _regroup_w_in(w_in)
    wr_bd = _block_diag(lru_w_r).astype(BF16)
    wi_bd = _block_diag(lru_w_i).astype(BF16)
    w_out_b = w_out.astype(BF16)
    wgu_b = ffn_w_gate_up.astype(BF16)
    wd_b = ffn_w_down.astype(BF16)
    row = lambda a: a.reshape(L, 1, a.shape[-1])
    bf_pad = jnp.pad(fox_b_f, ((0, 0), (0, LANES - FOX_HEADS))).reshape(L, 1, LANES)
    qg = jnp.tile(fox_q_gain * (HEAD_DIM ** -0.5 * LOG2E), (1, FOX_HEADS)).reshape(L, 1, FOX_WIDTH)
    kg = jnp.tile(fox_k_gain, (1, FOX_HEADS)).reshape(L, 1, FOX_WIDTH)
    wa_pad = jnp.pad(gla_w_alpha, ((0, 0), (GLOW_LANE0, LANES - GLOW_LANE0 - GLA_RANK), (0, 0))).astype(BF16)
    og = jnp.tile(gla_out_gain, (1, GLA_HEADS)).reshape(L, 1, GLA_VW)
    fox_ones = _head_ones(FOX_HEADS, HEAD_DIM)
    gla_ones = _head_ones(GLA_HEADS, GLA_DK)
    place, rq, rk = _fox_constants()
    tri, hmask, bdmask = _gla_constants()

    for l in range(L):
        lru, fox, gla, small = _in_proj(x, mod[l], row(norm1_gain)[l], w_in_p[l], tm)
        out_a = _lru_mixer(lru, conv_w[l], row(conv_b)[l], wr_bd[l], row(lru_b_r)[l], wi_bd[l],
                           row(lru_b_i)[l], row(lru_lambda)[l], tm)
        kaug, qaugt, vt = _fox_prep(fox, small, bf_pad[l], qg[l], kg[l], fox_ones, place, rq, rk, tm, tq)
        out_b = _fox_attention(kaug, qaugt, vt, tq)
        out_c = _gla_mixer(gla, small, wa_pad[l], row(gla_b_alpha)[l], og[l], tri, hmask, bdmask,
                           gla_ones, tm)
        x = _out_ffn(x, out_a, out_b, out_c, mod[l], row(norm2_gain)[l], w_out_b[l], wgu_b[l],
                     wd_b[l], tm)
    return x
```

```python
import functools

import numpy as np
import jax
import jax.numpy as jnp
from jax import lax
from jax.experimental import pallas as pl
from jax.experimental.pallas import tpu as pltpu

F32 = jnp.float32
BF16 = jnp.bfloat16

EPS = 1e-6
HEAD_DIM = 64
LRU_WIDTH = 384
LRU_BLOCKS = 6
CONV_WIDTH = 4
LRU_C = 8.0
FOX_HEADS = 6
FOX_WIDTH = FOX_HEADS * HEAD_DIM
GLA_HEADS = 4
GLA_DK = 64
GLA_KW = GLA_HEADS * GLA_DK
GLA_VW = GLA_KW
GLA_RANK = 16
GLA_TAU = 16.0
GLA_CHUNK = 64
N_MOD = 6

LANES = 128
SUBLANES = 8
VMEM_LIMIT_BYTES = 56 * 1024 * 1024

LRU_COLS = 2 * LRU_WIDTH
FOX_COLS = 3 * FOX_WIDTH
GLA_COLS = 3 * GLA_KW + GLA_VW
SMALL_COLS = LANES
GLOW_LANE0 = 8
NP_COLS = LRU_COLS + FOX_COLS + GLA_COLS + SMALL_COLS
IN_PROJ_GROUP = 768

AUG = LANES
N_SPLIT = 3
ONE_LANE = N_SPLIT * FOX_HEADS
NEG_BIG = -1e30
V_ROWS = HEAD_DIM + 16
LOG2E = float(np.log2(np.e))


def _sigmoid(x):
    return 0.5 * jnp.tanh(0.5 * x) + 0.5


def _log_sigmoid(x):
    return jnp.minimum(x, 0.0) - jnp.log(1.0 + jnp.exp(-jnp.abs(x)))


def _softplus(x):
    return jnp.maximum(x, 0.0) + jnp.log(1.0 + jnp.exp(-jnp.abs(x)))


def _gelu_tanh(x):
    c = np.float32(np.sqrt(2.0 / np.pi))
    return 0.5 * x * (1.0 + jnp.tanh(c * (x + 0.044715 * (x * x * x))))


def _dot(a, b):
    return jnp.dot(a, b, preferred_element_type=F32)


def _cparams(sem):
    return pltpu.CompilerParams(dimension_semantics=sem, vmem_limit_bytes=VMEM_LIMIT_BYTES)


def _const_spec(shape):
    nd = len(shape)
    return pl.BlockSpec(shape, lambda *_: (0,) * nd)


def _layer_weight_spec(stacked_shape, layer):
    tail = tuple(stacked_shape[1:])
    return pl.BlockSpec((None,) + tail, lambda *_: (layer,) + (0,) * len(tail),
                        pipeline_mode=pl.Buffered(1))


def _mod_kernel(c_ref, w_ref, b_ref, o_ref):
    c = c_ref[...]
    ca = (c * _sigmoid(c)).astype(BF16)
    o_ref[0] = _dot(ca, w_ref[0].astype(BF16)) + b_ref[0]


def _modulation(c, w_mod, b_mod):
    L, D, N = w_mod.shape
    B = c.shape[0]
    bp = -(-B // SUBLANES) * SUBLANES
    cp = jnp.pad(c, ((0, bp - B), (0, 0)))
    tn = 1024
    out = pl.pallas_call(
        _mod_kernel,
        grid=(L, N // tn),
        in_specs=[
            pl.BlockSpec((bp, D), lambda l, n: (0, 0)),
            pl.BlockSpec((1, D, tn), lambda l, n: (l, 0, n)),
            pl.BlockSpec((1, 1, tn), lambda l, n: (l, 0, n)),
        ],
        out_specs=pl.BlockSpec((1, bp, tn), lambda l, n: (l, 0, n)),
        out_shape=jax.ShapeDtypeStruct((L, bp, N), F32),
        compiler_params=_cparams(("arbitrary", "arbitrary")),
        name="adaln_mod",
    )(cp, w_mod, b_mod.reshape(L, 1, N))
    return out[:, :B].reshape(L, B, N_MOD, D)


def _rms_mod(x, gain, scale, shift):
    ms = jnp.mean(x * x, axis=-1, keepdims=True)
    y = x * lax.rsqrt(ms + EPS) * gain
    return y * (1.0 + scale) + shift


def _in_proj_kernel(x_ref, mod_ref, g_ref, w_ref, lru_ref, fox_ref, gla_ref, small_ref):
    h = _rms_mod(x_ref[0], g_ref[...], mod_ref[0, 1:2, :], mod_ref[0, 0:1, :]).astype(BF16)
    segs, c0 = [], 0
    for ref, n in ((lru_ref, LRU_COLS), (fox_ref, FOX_COLS), (gla_ref, GLA_COLS), (small_ref, SMALL_COLS)):
        segs.append((ref, c0, c0 + n))
        c0 += n
    for g0 in range(0, NP_COLS, IN_PROJ_GROUP):
        g1 = min(g0 + IN_PROJ_GROUP, NP_COLS)
        y = _dot(h, w_ref[:, g0:g1])
        for ref, s0, s1 in segs:
            lo, hi = max(g0, s0), min(g1, s1)
            if lo < hi:
                ref[0, :, lo - s0:hi - s0] = y[:, lo - g0:hi - g0].astype(ref.dtype)


def _in_proj(x, mod, gain, w, layer, tm):
    B, S, D = x.shape
    tok = lambda n: pl.BlockSpec((1, tm, n), lambda b, i: (b, i, 0))
    return pl.pallas_call(
        _in_proj_kernel,
        grid=(B, S // tm),
        in_specs=[
            tok(D),
            pl.BlockSpec((1, N_MOD, D), lambda b, i: (b, 0, 0)),
            _const_spec((1, D)),
            _layer_weight_spec(w.shape, layer),
        ],
        out_specs=[tok(LRU_COLS), tok(FOX_COLS), tok(GLA_COLS), tok(SMALL_COLS)],
        out_shape=[
            jax.ShapeDtypeStruct((B, S, LRU_COLS), BF16),
            jax.ShapeDtypeStruct((B, S, FOX_COLS), BF16),
            jax.ShapeDtypeStruct((B, S, GLA_COLS), BF16),
            jax.ShapeDtypeStruct((B, S, SMALL_COLS), F32),
        ],
        compiler_params=_cparams(("arbitrary", "arbitrary")),
        name="in_proj",
    )(x, mod, gain, w)


def _scan_add(v):
    n = v.shape[0]
    row = lax.broadcasted_iota(jnp.int32, v.shape, 0)
    d = 1
    while d < n:
        v = v + jnp.where(row >= d, pltpu.roll(v, d, axis=0), 0.0)
        d *= 2
    return v


def _scan_linear(a, u, h0):
    n, w = a.shape
    g = n // SUBLANES
    a = a.reshape(g, SUBLANES, w)
    u = u.reshape(g, SUBLANES, w)
    sub = lax.broadcasted_iota(jnp.int32, a.shape, 1)
    d = 1
    while d < SUBLANES:
        m = sub >= d
        u = u + jnp.where(m, a * pltpu.roll(u, d, axis=1), 0.0)
        a = jnp.where(m, a * pltpu.roll(a, d, axis=1), a)
        d *= 2
    hs, h = [], h0
    for v in range(g):
        hv = a[v] * h + u[v]
        h = hv[SUBLANES - 1:SUBLANES, :]
        hs.append(hv)
    return jnp.concatenate(hs, axis=0), h


def _lru_kernel(lru_ref, cw_ref, cb_ref, wr_ref, br_ref, wi_ref, bi_ref, lam_ref, o_ref,
                xs_ref, h_ref):
    ts = o_ref.shape[1]
    W = LRU_WIDTH

    @pl.when(pl.program_id(1) == 0)
    def _():
        xs_ref[0:SUBLANES, :] = jnp.zeros((SUBLANES, W), F32)
        h_ref[...] = jnp.zeros_like(h_ref)

    xa = lru_ref[0, :, 0:W].astype(F32)
    ya = lru_ref[0, :, W:2 * W].astype(F32)
    xs_ref[SUBLANES:SUBLANES + ts, :] = xa
    xc = xa * cw_ref[CONV_WIDTH - 1:CONV_WIDTH, :] + cb_ref[...]
    for k in range(1, CONV_WIDTH):
        j = CONV_WIDTH - 1 - k
        xc = xc + xs_ref[pl.ds(SUBLANES - k, ts), :] * cw_ref[j:j + 1, :]
    xs_ref[0:SUBLANES, :] = xs_ref[ts:ts + SUBLANES, :]

    xcb = xc.astype(BF16)
    r = _sigmoid(_dot(xcb, wr_ref[...]) + br_ref[...])
    i = _sigmoid(_dot(xcb, wi_ref[...]) + bi_ref[...])
    log_a = (-LRU_C) * r * _softplus(-lam_ref[...])
    a = jnp.exp(log_a)
    mult = jnp.sqrt(1.0 - a * a)
    u = mult * (i * xc)
    h, h_last = _scan_linear(a, u, h_ref[0:1, :])
    h_ref[...] = jnp.broadcast_to(h_last, h_ref.shape)
    o_ref[0] = (h * _gelu_tanh(ya)).astype(o_ref.dtype)


def _lru_mixer(lru, cw, cb, wr, br, wi, bi, lam, ts):
    B, S, _ = lru.shape
    W = LRU_WIDTH
    return pl.pallas_call(
        _lru_kernel,
        grid=(B, S // ts),
        in_specs=[
            pl.BlockSpec((1, ts, LRU_COLS), lambda b, i: (b, i, 0)),
            _const_spec((CONV_WIDTH, W)), _const_spec((1, W)),
            _const_spec((W, W)), _const_spec((1, W)),
            _const_spec((W, W)), _const_spec((1, W)),
            _const_spec((1, W)),
        ],
        out_specs=pl.BlockSpec((1, ts, W), lambda b, i: (b, i, 0)),
        out_shape=jax.ShapeDtypeStruct((B, S, W), BF16),
        scratch_shapes=[pltpu.VMEM((ts + SUBLANES, W), F32), pltpu.VMEM((SUBLANES, W), F32)],
        compiler_params=_cparams(("arbitrary", "arbitrary")),
        name="lru_mixer",
    )(lru, cw, cb, wr, br, wi, bi, lam)


def _fox_prep_kernel(fox_ref, small_ref, bf_ref, qg_ref, kg_ref, ones_ref,
                     rq_ref, rk_ref, kaug_ref, qaugt_ref, vt_ref, cum_ref):
    tm = fox_ref.shape[1]
    W = FOX_WIDTH

    @pl.when(pl.program_id(1) == 0)
    def _():
        cum_ref[...] = jnp.zeros_like(cum_ref)

    q = fox_ref[0, :, 0:W].astype(F32)
    k = fox_ref[0, :, W:2 * W].astype(F32)
    v = fox_ref[0, :, 2 * W:3 * W].astype(F32)

    def head_norm(t, gain):
        ms = _dot((t * t).astype(BF16), ones_ref[...]) * (1.0 / HEAD_DIM)
        return t * lax.rsqrt(ms + EPS) * gain

    qn = head_norm(q, qg_ref[...])
    kn = head_norm(k, kg_ref[...])

    lf = _log_sigmoid(small_ref[0] + bf_ref[...])
    cum = _scan_add(lf) + cum_ref[0:1, :]
    cum_ref[...] = jnp.broadcast_to(cum[tm - 1:tm, :], cum_ref.shape)

    cs = cum * LOG2E
    c1 = cs.astype(BF16).astype(F32)
    r1 = cs - c1
    c2 = r1.astype(BF16).astype(F32)
    c3 = (r1 - c2).astype(BF16).astype(F32)
    lane = lax.broadcasted_iota(jnp.int32, cum.shape, 1)
    H = FOX_HEADS
    packed = jnp.where(lane < H, c1, 0.0)
    packed = jnp.where((lane >= H) & (lane < 2 * H), pltpu.roll(c2, H, axis=1), packed)
    packed = jnp.where((lane >= 2 * H) & (lane < 3 * H), pltpu.roll(c3, 2 * H, axis=1), packed)
    packed = jnp.where(lane == ONE_LANE, 1.0, packed).astype(BF16)

    kbias = _dot(packed, rk_ref[...])
    qbias = _dot(packed, rq_ref[...])
    feat = lane < HEAD_DIM

    def head_aug(t, bias, h):
        src = t[:, (h // 2) * LANES:(h // 2 + 1) * LANES]
        if h % 2:
            src = pltpu.roll(src, HEAD_DIM, axis=1)
        return jnp.where(feat, src, bias[:, h * AUG:(h + 1) * AUG])

    vt = v.T
    tk = vt_ref.shape[4]
    for h in range(H):
        kaug_ref[0, h] = head_aug(kn, kbias, h).astype(BF16)
        qaugt_ref[0, h] = head_aug(qn, qbias, h).T.astype(BF16)
        for cb in range(tm // tk):
            vt_ref[0, h, cb, 0:HEAD_DIM, :] = (
                vt[h * HEAD_DIM:(h + 1) * HEAD_DIM, cb * tk:(cb + 1) * tk].astype(BF16))
            vt_ref[0, h, cb, HEAD_DIM:V_ROWS, :] = jnp.ones((V_ROWS - HEAD_DIM, tk), BF16)


def _fox_prep(fox, small, bf, qg, kg, ones_bd, rq, rk, tm, tk):
    B, S, _ = fox.shape
    H, W = FOX_HEADS, FOX_WIDTH
    return pl.pallas_call(
        _fox_prep_kernel,
        grid=(B, S // tm),
        in_specs=[
            pl.BlockSpec((1, tm, FOX_COLS), lambda b, i: (b, i, 0)),
            pl.BlockSpec((1, tm, SMALL_COLS), lambda b, i: (b, i, 0)),
            _const_spec((1, LANES)), _const_spec((1, W)), _const_spec((1, W)),
            _const_spec((W, W)),
            _const_spec((LANES, H * AUG)), _const_spec((LANES, H * AUG)),
        ],
        out_specs=[
            pl.BlockSpec((1, H, tm, AUG), lambda b, i: (b, 0, i, 0)),
            pl.BlockSpec((1, H, AUG, tm), lambda b, i: (b, 0, 0, i)),
            pl.BlockSpec((1, H, tm // tk, V_ROWS, tk), lambda b, i: (b, 0, i, 0, 0)),
        ],
        out_shape=[
            jax.ShapeDtypeStruct((B, H, S, AUG), BF16),
            jax.ShapeDtypeStruct((B, H, AUG, S), BF16),
            jax.ShapeDtypeStruct((B, H, S // tk, V_ROWS, tk), BF16),
        ],
        scratch_shapes=[pltpu.VMEM((SUBLANES, LANES), F32)],
        compiler_params=_cparams(("arbitrary", "arbitrary")),
        name="fox_prep",
    )(fox, small, bf, qg, kg, ones_bd, rq, rk)


def _fox_attn_kernel(k_ref, qt_ref, vt_ref, o_ref, acc_ref, sa_ref, sb_ref, *, tq):
    qi = pl.program_id(1)
    H = FOX_HEADS
    last_block = vt_ref.shape[2] - 1
    acc_ref[...] = jnp.zeros_like(acc_ref)

    def scores(j, s_ref):
        start = pl.multiple_of(jnp.minimum(j, last_block) * tq, tq)
        for h in range(H):
            s_ref[h] = _dot(k_ref[0, h, pl.ds(start, tq), :], qt_ref[0, h])

    def softmax(s_ref, ms, mask):
        new, ps = [], []
        for h in range(H):
            s = s_ref[h]
            if mask is not None:
                s = jnp.where(mask, s, NEG_BIG)
            m_new = jnp.maximum(ms[h], jnp.max(s, axis=0, keepdims=True))
            new.append(m_new)
            ps.append((jnp.exp2(ms[h] - m_new), jnp.exp2(s - m_new).astype(BF16)))
        return tuple(new), ps

    def values(j, ps):
        jc = jnp.minimum(j, last_block)
        for h in range(H):
            alpha, p = ps[h]
            acc_ref[h] = alpha * acc_ref[h] + _dot(vt_ref[0, h, jc], p)

    def pair(t, ml, masks, prefetch):
        j0 = 2 * t
        ml, ps = softmax(sa_ref, ml, masks[0])
        scores(j0 + 1, sb_ref)
        values(j0, ps)
        ml, ps = softmax(sb_ref, ml, masks[1])
        if prefetch:
            scores(j0 + 2, sa_ref)
        values(j0 + 1, ps)
        return ml

    scores(0, sa_ref)
    ml = tuple(jnp.full((1, tq), NEG_BIG, F32) for _ in range(H))
    n_pairs = lax.div(qi, 2)
    ml = lax.fori_loop(0, n_pairs, lambda t, c: pair(t, c, (None, None), True), ml)
    rel = (lax.broadcasted_iota(jnp.int32, (tq, tq), 0)
           - lax.broadcasted_iota(jnp.int32, (tq, tq), 1))
    j0 = 2 * n_pairs
    masks = (rel <= (qi - j0) * tq, rel <= (qi - j0 - 1) * tq)
    ml = pair(n_pairs, ml, masks, False)
    out = jnp.concatenate(
        [acc_ref[h, 0:HEAD_DIM, :] * (1.0 / acc_ref[h, HEAD_DIM:HEAD_DIM + 1, :]) for h in range(H)], axis=0)
    o_ref[0] = out.T.astype(o_ref.dtype)


def _fox_attention(kaug, qaugt, vt, tq):
    B, H, S, _ = kaug.shape
    return pl.pallas_call(
        functools.partial(_fox_attn_kernel, tq=tq),
        grid=(B, S // tq),
        in_specs=[
            pl.BlockSpec((1, H, S, AUG), lambda b, i: (b, 0, 0, 0), pipeline_mode=pl.Buffered(1)),
            pl.BlockSpec((1, H, AUG, tq), lambda b, i: (b, 0, 0, i)),
            pl.BlockSpec((1, H, S // tq, V_ROWS, tq), lambda b, i: (b, 0, 0, 0, 0),
                         pipeline_mode=pl.Buffered(1)),
        ],
        out_specs=pl.BlockSpec((1, tq, H * HEAD_DIM), lambda b, i: (b, i, 0)),
        out_shape=jax.ShapeDtypeStruct((B, S, H * HEAD_DIM), BF16),
        scratch_shapes=[pltpu.VMEM((H, V_ROWS, tq), F32), pltpu.VMEM((H, tq, tq), F32),
                        pltpu.VMEM((H, tq, tq), F32)],
        compiler_params=_cparams(("arbitrary", "arbitrary")),
        name="fox_attention",
    )(kaug, qaugt, vt)


GLA_ROWS = 256


def _gla_kernel(gla_ref, small_ref, wa_ref, ba_ref, gain_ref, tri_ref, hmask_ref, bdmask_ref,
                ones_ref, o_ref, st_ref):
    ts = o_ref.shape[1]
    KW, C, R = GLA_KW, GLA_CHUNK, GLA_ROWS

    @pl.when(pl.program_id(1) == 0)
    def _():
        st_ref[...] = jnp.zeros_like(st_ref)

    q = gla_ref[0, :, 0:KW].astype(F32) * (GLA_DK ** -0.5)
    k = gla_ref[0, :, KW:2 * KW].astype(F32)
    v = gla_ref[0, :, 2 * KW:3 * KW]
    low = small_ref[0].astype(BF16)
    la = _log_sigmoid(_dot(low, wa_ref[...]) + ba_ref[...]) * (1.0 / GLA_TAU)
    la_hi = la.astype(BF16)
    la_lo = (la - la_hi.astype(F32)).astype(BF16)
    tri = tri_ref[...]
    bcum = jnp.concatenate(
        [_dot(tri, la_hi[g0:g0 + R]) + _dot(tri, la_lo[g0:g0 + R]) for g0 in range(0, ts, R)], axis=0)
    ends = [bcum[c0 + C - 1:c0 + C, :] for c0 in range(0, ts, C)]
    b_last = jnp.concatenate([jnp.broadcast_to(e, (C, KW)) for e in ends], axis=0)
    q_dec = (q * jnp.exp(bcum)).astype(BF16)
    k_dec = (k * jnp.exp(-bcum)).astype(BF16)
    k_end = (k * jnp.exp(b_last - bcum)).astype(BF16)

    tri_f = tri.astype(F32)
    o_groups = []
    for g0 in range(0, ts, R):
        rows = slice(g0, g0 + R)
        o_acc = jnp.zeros((R, KW), F32)
        for h in range(GLA_HEADS):
            hm = hmask_ref[h:h + 1, :].astype(BF16)
            att = lax.dot_general(q_dec[rows] * hm, k_dec[rows], (((1,), (1,)), ((), ())),
                                  preferred_element_type=F32)
            o_acc = o_acc + _dot((att * tri_f).astype(BF16), v[rows] * hm)
        o_groups.append(o_acc)

    o_parts = []
    for ci, c0 in enumerate(range(0, ts, C)):
        cs = slice(c0, c0 + C)
        st = st_ref[...]
        o_parts.append(lax.dot_general(q_dec[cs], st.astype(BF16), (((1,), (1,)), ((), ())),
                                       preferred_element_type=F32))
        kv_t = lax.dot_general(v[cs], k_end[cs], (((0,), (0,)), ((), ())),
                               preferred_element_type=F32)
        st_ref[...] = st * jnp.exp(ends[ci]) + kv_t * bdmask_ref[...]
    o = jnp.concatenate(o_groups, axis=0) + jnp.concatenate(o_parts, axis=0)

    ms = _dot((o * o).astype(BF16), ones_ref[...]) * (1.0 / GLA_DK)
    o = o * lax.rsqrt(ms + EPS) * gain_ref[...]
    g = gla_ref[0, :, 3 * KW:4 * KW].astype(F32)
    o_ref[0] = (o * (g * _sigmoid(g))).astype(o_ref.dtype)


def _gla_mixer(gla, small, wa, ba, gain, tri, hmask, bdmask, ones_bd, ts):
    B, S, _ = gla.shape
    KW = GLA_KW
    return pl.pallas_call(
        _gla_kernel,
        grid=(B, S // ts),
        in_specs=[
            pl.BlockSpec((1, ts, GLA_COLS), lambda b, i: (b, i, 0)),
            pl.BlockSpec((1, ts, SMALL_COLS), lambda b, i: (b, i, 0)),
            _const_spec((LANES, KW)), _const_spec((1, KW)), _const_spec((1, KW)),
            _const_spec((GLA_ROWS, GLA_ROWS)), _const_spec((GLA_HEADS, KW)),
            _const_spec((KW, KW)), _const_spec((KW, KW)),
        ],
        out_specs=pl.BlockSpec((1, ts, KW), lambda b, i: (b, i, 0)),
        out_shape=jax.ShapeDtypeStruct((B, S, KW), BF16),
        scratch_shapes=[pltpu.VMEM((KW, KW), F32)],
        compiler_params=_cparams(("arbitrary", "arbitrary")),
        name="gla_mixer",
    )(gla, small, wa, ba, gain, tri, hmask, bdmask, ones_bd)


def _ffn_chunks(d_ff):
    chunks, c0 = [], 0
    while c0 < d_ff:
        n = min(512, d_ff - c0)
        chunks.append((c0, n))
        c0 += n
    return chunks


def _out_ffn_kernel(x_ref, a_ref, b_ref, c_ref, mod_ref, g_ref, wo_ref, wgu_ref, wd_ref, o_ref,
                    acc_ref):
    d_ff = wd_ref.shape[0]
    mix = _dot(jnp.concatenate([a_ref[0], b_ref[0], c_ref[0]], axis=1), wo_ref[...])
    x1 = x_ref[0] + mod_ref[0, 2:3, :] * mix
    h = _rms_mod(x1, g_ref[...], mod_ref[0, 4:5, :], mod_ref[0, 3:4, :]).astype(BF16)
    for idx, (c0, n) in enumerate(_ffn_chunks(d_ff)):
        gt = _dot(h, wgu_ref[:, c0:c0 + n])
        up = _dot(h, wgu_ref[:, d_ff + c0:d_ff + c0 + n])
        act = (gt * _sigmoid(gt) * up).astype(BF16)
        y = _dot(act, wd_ref[c0:c0 + n, :])
        if idx == 0:
            acc_ref[...] = y
        else:
            acc_ref[...] += y
    o_ref[0] = x1 + mod_ref[0, 5:6, :] * acc_ref[...]


def _out_ffn(x, a, b, c, mod, gain, wo, wgu, wd, layer, tm):
    B, S, D = x.shape
    tok = lambda n: pl.BlockSpec((1, tm, n), lambda bb, i: (bb, i, 0))
    return pl.pallas_call(
        _out_ffn_kernel,
        grid=(B, S // tm),
        in_specs=[
            tok(D), tok(a.shape[2]), tok(b.shape[2]), tok(c.shape[2]),
            pl.BlockSpec((1, N_MOD, D), lambda bb, i: (bb, 0, 0)),
            _const_spec((1, D)),
            _layer_weight_spec(wo.shape, layer), _layer_weight_spec(wgu.shape, layer),
            _layer_weight_spec(wd.shape, layer),
        ],
        out_specs=tok(D),
        out_shape=jax.ShapeDtypeStruct((B, S, D), F32),
        scratch_shapes=[pltpu.VMEM((tm, D), F32)],
        compiler_params=_cparams(("arbitrary", "arbitrary")),
        name="out_proj_ffn",
    )(x, a, b, c, mod, gain, wo, wgu, wd)


def _block_diag(w):
    L, n, d, e = w.shape
    eye = jnp.eye(n, dtype=w.dtype)
    return jnp.einsum("lnde,nm->lndme", w, eye).reshape(L, n * d, n * e)


def _regroup_w_in(w_in):
    L, D, _ = w_in.shape
    sizes = (LRU_WIDTH, LRU_WIDTH, FOX_WIDTH, FOX_WIDTH, FOX_WIDTH, FOX_HEADS,
             GLA_KW, GLA_KW, GLA_VW, GLA_RANK, GLA_VW)
    offs = np.concatenate([[0], np.cumsum(sizes)])
    seg = lambda i: w_in[:, :, offs[i]:offs[i + 1]]
    z = lambda n: jnp.zeros((L, D, n), w_in.dtype)
    cols = [seg(0), seg(1), seg(2), seg(3), seg(4), seg(6), seg(7), seg(8), seg(10),
            seg(5), z(GLOW_LANE0 - FOX_HEADS), seg(9), z(SMALL_COLS - GLOW_LANE0 - GLA_RANK)]
    return jnp.concatenate(cols, axis=-1).astype(BF16)


def _head_ones(n_heads, dim):
    return jnp.asarray(np.kron(np.eye(n_heads), np.ones((dim, dim))), BF16)


def _fox_constants():
    H, W = FOX_HEADS, FOX_WIDTH
    rq = np.zeros((LANES, H * AUG), np.float32)
    rk = np.zeros((LANES, H * AUG), np.float32)
    for h in range(H):
        for s in range(N_SPLIT):
            rq[s * H + h, h * AUG + HEAD_DIM + s] = 1.0
            rq[ONE_LANE, h * AUG + HEAD_DIM + N_SPLIT + s] = 1.0
            rk[ONE_LANE, h * AUG + HEAD_DIM + s] = 1.0
            rk[s * H + h, h * AUG + HEAD_DIM + N_SPLIT + s] = -1.0
    return jnp.asarray(rq, BF16), jnp.asarray(rk, BF16)


def _gla_constants():
    R, C, KW = GLA_ROWS, GLA_CHUNK, GLA_KW
    i = np.arange(R)
    tri = ((i[:, None] >= i[None, :]) & (i[:, None] // C == i[None, :] // C)).astype(np.float32)
    lane = np.arange(KW)
    hmask = (lane[None, :] // GLA_DK == np.arange(GLA_HEADS)[:, None]).astype(np.float32)
    bdmask = (lane[:, None] // GLA_DK == lane[None, :] // GLA_DK).astype(np.float32)
    return jnp.asarray(tri, BF16), jnp.asarray(hmask, F32), jnp.asarray(bdmask, F32)


def _pick_tile(n, want):
    t = min(n, want)
    while n % t:
        t //= 2
    return t


def kernel(x, c, norm1_gain, norm2_gain, w_mod, b_mod, w_in, conv_w, conv_b, lru_w_r, lru_b_r,
           lru_w_i, lru_b_i, lru_lambda, fox_b_f, fox_q_gain, fox_k_gain, gla_w_alpha, gla_b_alpha,
           gla_out_gain, w_out, ffn_w_gate_up, ffn_w_down):
    B, S, D = x.shape
    L = w_in.shape[0]
    tm = _pick_tile(S, 512)
    tq = _pick_tile(S, 256)

    mod = _modulation(c, w_mod, b_mod)
    w_in_p = _regroup_w_in(w_in)
    wr_bd = _block_diag(lru_w_r).astype(BF16)
    wi_bd = _block_diag(lru_w_i).astype(BF16)
    w_out_b = w_out.astype(BF16)
    wgu_b = ffn_w_gate_up.astype(BF16)
    wd_b = ffn_w_down.astype(BF16)
    row = lambda a: a.reshape(L, 1, a.shape[-1])
    bf_pad = jnp.pad(fox_b_f, ((0, 0), (0, LANES - FOX_HEADS))).reshape(L, 1, LANES)
    qg = jnp.tile(fox_q_gain * (HEAD_DIM ** -0.5 * LOG2E), (1, FOX_HEADS)).reshape(L, 1, FOX_WIDTH)
    kg = jnp.tile(fox_k_gain, (1, FOX_HEADS)).reshape(L, 1, FOX_WIDTH)
    wa_pad = jnp.pad(gla_w_alpha, ((0, 0), (GLOW_LANE0, LANES - GLOW_LANE0 - GLA_RANK), (0, 0))).astype(BF16)
    og = jnp.tile(gla_out_gain, (1, GLA_HEADS)).reshape(L, 1, GLA_VW)
    fox_ones = _head_ones(FOX_HEADS, HEAD_DIM)
    gla_ones = _head_ones(GLA_HEADS, GLA_DK)
    rq, rk = _fox_constants()
    tri, hmask, bdmask = _gla_constants()

    for l in range(L):
        lru, fox, gla, small = _in_proj(x, mod[l], row(norm1_gain)[l], w_in_p, l, tm)
        out_a = _lru_mixer(lru, conv_w[l], row(conv_b)[l], wr_bd[l], row(lru_b_r)[l], wi_bd[l],
                           row(lru_b_i)[l], row(lru_lambda)[l], tm)
        kaug, qaugt, vt = _fox_prep(fox, small, bf_pad[l], qg[l], kg[l], fox_ones, rq, rk, tm, tq)
        out_b = _fox_attention(kaug, qaugt, vt, tq)
        out_c = _gla_mixer(gla, small, wa_pad[l], row(gla_b_alpha)[l], og[l], tri, hmask, bdmask,
                           gla_ones, tm)
        x = _out_ffn(x, out_a, out_b, out_c, mod[l], row(norm2_gain)[l], w_out_b, wgu_b, wd_b, l, tm)
    return x
```

```python
import functools

import numpy as np
import jax
import jax.numpy as jnp
from jax import lax
from jax.experimental import pallas as pl
from jax.experimental.pallas import tpu as pltpu

F32 = jnp.float32
BF16 = jnp.bfloat16

EPS = 1e-6
HEAD_DIM = 64
LRU_WIDTH = 384
LRU_BLOCKS = 6
CONV_WIDTH = 4
LRU_C = 8.0
FOX_HEADS = 6
FOX_WIDTH = FOX_HEADS * HEAD_DIM
GLA_HEADS = 4
GLA_DK = 64
GLA_KW = GLA_HEADS * GLA_DK
GLA_VW = GLA_KW
GLA_RANK = 16
GLA_TAU = 16.0
GLA_CHUNK = 64
N_MOD = 6

LANES = 128
SUBLANES = 8
VMEM_LIMIT_BYTES = 56 * 1024 * 1024

LRU_COLS = 2 * LRU_WIDTH
FOX_COLS = 3 * FOX_WIDTH
GLA_COLS = 3 * GLA_KW + GLA_VW
SMALL_COLS = LANES
GLOW_LANE0 = 8
NP_COLS = LRU_COLS + FOX_COLS + GLA_COLS + SMALL_COLS
IN_PROJ_GROUP = 768

AUG = LANES
N_SPLIT = 3
ONE_LANE = N_SPLIT * FOX_HEADS
NEG_BIG = -1e30
V_ROWS = HEAD_DIM + 16
LOG2E = float(np.log2(np.e))


def _sigmoid(x):
    return 0.5 * jnp.tanh(0.5 * x) + 0.5


def _log_sigmoid(x):
    return jnp.minimum(x, 0.0) - jnp.log(1.0 + jnp.exp(-jnp.abs(x)))


def _softplus(x):
    return jnp.maximum(x, 0.0) + jnp.log(1.0 + jnp.exp(-jnp.abs(x)))


def _gelu_tanh(x):
    c = np.float32(np.sqrt(2.0 / np.pi))
    return 0.5 * x * (1.0 + jnp.tanh(c * (x + 0.044715 * (x * x * x))))


def _dot(a, b):
    return jnp.dot(a, b, preferred_element_type=F32)


def _cparams(sem):
    return pltpu.CompilerParams(dimension_semantics=sem, vmem_limit_bytes=VMEM_LIMIT_BYTES)


def _const_spec(shape):
    nd = len(shape)
    return pl.BlockSpec(shape, lambda *_: (0,) * nd)


def _layer_weight_spec(stacked_shape, layer):
    tail = tuple(stacked_shape[1:])
    return pl.BlockSpec((None,) + tail, lambda *_: (layer,) + (0,) * len(tail),
                        pipeline_mode=pl.Buffered(1))


def _mod_kernel(c_ref, w_ref, b_ref, o_ref):
    c = c_ref[...]
    ca = (c * _sigmoid(c)).astype(BF16)
    o_ref[0] = _dot(ca, w_ref[0].astype(BF16)) + b_ref[0]


def _modulation(c, w_mod, b_mod):
    L, D, N = w_mod.shape
    B = c.shape[0]
    bp = -(-B // SUBLANES) * SUBLANES
    cp = jnp.pad(c, ((0, bp - B), (0, 0)))
    tn = 1024
    out = pl.pallas_call(
        _mod_kernel,
        grid=(L, N // tn),
        in_specs=[
            pl.BlockSpec((bp, D), lambda l, n: (0, 0)),
            pl.BlockSpec((1, D, tn), lambda l, n: (l, 0, n)),
            pl.BlockSpec((1, 1, tn), lambda l, n: (l, 0, n)),
        ],
        out_specs=pl.BlockSpec((1, bp, tn), lambda l, n: (l, 0, n)),
        out_shape=jax.ShapeDtypeStruct((L, bp, N), F32),
        compiler_params=_cparams(("arbitrary", "arbitrary")),
        name="adaln_mod",
    )(cp, w_mod, b_mod.reshape(L, 1, N))
    return out[:, :B].reshape(L, B, N_MOD, D)


def _rms_mod(x, gain, scale, shift):
    ms = jnp.mean(x * x, axis=-1, keepdims=True)
    y = x * lax.rsqrt(ms + EPS) * gain
    return y * (1.0 + scale) + shift


def _in_proj_kernel(x_ref, mod_ref, g_ref, w_ref, lru_ref, fox_ref, gla_ref, small_ref):
    h = _rms_mod(x_ref[0], g_ref[...], mod_ref[0, 1:2, :], mod_ref[0, 0:1, :]).astype(BF16)
    segs, c0 = [], 0
    for ref, n in ((lru_ref, LRU_COLS), (fox_ref, FOX_COLS), (gla_ref, GLA_COLS), (small_ref, SMALL_COLS)):
        segs.append((ref, c0, c0 + n))
        c0 += n
    for g0 in range(0, NP_COLS, IN_PROJ_GROUP):
        g1 = min(g0 + IN_PROJ_GROUP, NP_COLS)
        y = _dot(h, w_ref[:, g0:g1])
        for ref, s0, s1 in segs:
            lo, hi = max(g0, s0), min(g1, s1)
            if lo < hi:
                ref[0, :, lo - s0:hi - s0] = y[:, lo - g0:hi - g0].astype(ref.dtype)


def _in_proj(x, mod, gain, w, layer, tm):
    B, S, D = x.shape
    tok = lambda n: pl.BlockSpec((1, tm, n), lambda b, i: (b, i, 0))
    return pl.pallas_call(
        _in_proj_kernel,
        grid=(B, S // tm),
        in_specs=[
            tok(D),
            pl.BlockSpec((1, N_MOD, D), lambda b, i: (b, 0, 0)),
            _const_spec((1, D)),
            _layer_weight_spec(w.shape, layer),
        ],
        out_specs=[tok(LRU_COLS), tok(FOX_COLS), tok(GLA_COLS), tok(SMALL_COLS)],
        out_shape=[
            jax.ShapeDtypeStruct((B, S, LRU_COLS), BF16),
            jax.ShapeDtypeStruct((B, S, FOX_COLS), BF16),
            jax.ShapeDtypeStruct((B, S, GLA_COLS), BF16),
            jax.ShapeDtypeStruct((B, S, SMALL_COLS), F32),
        ],
        compiler_params=_cparams(("arbitrary", "arbitrary")),
        name="in_proj",
    )(x, mod, gain, w)


def _scan_add(v):
    n = v.shape[0]
    row = lax.broadcasted_iota(jnp.int32, v.shape, 0)
    d = 1
    while d < n:
        v = v + jnp.where(row >= d, pltpu.roll(v, d, axis=0), 0.0)
        d *= 2
    return v


def _scan_linear(a, u, h0):
    n, w = a.shape
    g = n // SUBLANES
    a = a.reshape(g, SUBLANES, w)
    u = u.reshape(g, SUBLANES, w)
    sub = lax.broadcasted_iota(jnp.int32, a.shape, 1)
    d = 1
    while d < SUBLANES:
        m = sub >= d
        u = u + jnp.where(m, a * pltpu.roll(u, d, axis=1), 0.0)
        a = jnp.where(m, a * pltpu.roll(a, d, axis=1), a)
        d *= 2
    hs, h = [], h0
    for v in range(g):
        hv = a[v] * h + u[v]
        h = hv[SUBLANES - 1:SUBLANES, :]
        hs.append(hv)
    return jnp.concatenate(hs, axis=0), h


def _lru_kernel(lru_ref, cw_ref, cb_ref, wr_ref, br_ref, wi_ref, bi_ref, lam_ref, o_ref,
                xs_ref, h_ref):
    ts = o_ref.shape[1]
    W = LRU_WIDTH

    @pl.when(pl.program_id(1) == 0)
    def _():
        xs_ref[0:SUBLANES, :] = jnp.zeros((SUBLANES, W), F32)
        h_ref[...] = jnp.zeros_like(h_ref)

    xa = lru_ref[0, :, 0:W].astype(F32)
    ya = lru_ref[0, :, W:2 * W].astype(F32)
    xs_ref[SUBLANES:SUBLANES + ts, :] = xa
    xc = xa * cw_ref[CONV_WIDTH - 1:CONV_WIDTH, :] + cb_ref[...]
    for k in range(1, CONV_WIDTH):
        j = CONV_WIDTH - 1 - k
        xc = xc + xs_ref[pl.ds(SUBLANES - k, ts), :] * cw_ref[j:j + 1, :]
    xs_ref[0:SUBLANES, :] = xs_ref[ts:ts + SUBLANES, :]

    xcb = xc.astype(BF16)
    r = _sigmoid(_dot(xcb, wr_ref[...]) + br_ref[...])
    i = _sigmoid(_dot(xcb, wi_ref[...]) + bi_ref[...])
    log_a = (-LRU_C) * r * _softplus(-lam_ref[...])
    a = jnp.exp(log_a)
    mult = jnp.sqrt(1.0 - a * a)
    u = mult * (i * xc)
    h, h_last = _scan_linear(a, u, h_ref[0:1, :])
    h_ref[...] = jnp.broadcast_to(h_last, h_ref.shape)
    o_ref[0] = (h * _gelu_tanh(ya)).astype(o_ref.dtype)


def _lru_mixer(lru, cw, cb, wr, br, wi, bi, lam, ts):
    B, S, _ = lru.shape
    W = LRU_WIDTH
    return pl.pallas_call(
        _lru_kernel,
        grid=(B, S // ts),
        in_specs=[
            pl.BlockSpec((1, ts, LRU_COLS), lambda b, i: (b, i, 0)),
            _const_spec((CONV_WIDTH, W)), _const_spec((1, W)),
            _const_spec((W, W)), _const_spec((1, W)),
            _const_spec((W, W)), _const_spec((1, W)),
            _const_spec((1, W)),
        ],
        out_specs=pl.BlockSpec((1, ts, W), lambda b, i: (b, i, 0)),
        out_shape=jax.ShapeDtypeStruct((B, S, W), BF16),
        scratch_shapes=[pltpu.VMEM((ts + SUBLANES, W), F32), pltpu.VMEM((SUBLANES, W), F32)],
        compiler_params=_cparams(("arbitrary", "arbitrary")),
        name="lru_mixer",
    )(lru, cw, cb, wr, br, wi, bi, lam)


def _fox_prep_kernel(fox_ref, small_ref, bf_ref, qg_ref, kg_ref, ones_ref,
                     rq_ref, rk_ref, kaug_ref, qaugt_ref, vt_ref, cum_ref):
    tm = fox_ref.shape[1]
    W = FOX_WIDTH

    @pl.when(pl.program_id(1) == 0)
    def _():
        cum_ref[...] = jnp.zeros_like(cum_ref)

    q = fox_ref[0, :, 0:W].astype(F32)
    k = fox_ref[0, :, W:2 * W].astype(F32)
    v = fox_ref[0, :, 2 * W:3 * W].astype(F32)

    def head_norm(t, gain):
        ms = _dot((t * t).astype(BF16), ones_ref[...]) * (1.0 / HEAD_DIM)
        return t * lax.rsqrt(ms + EPS) * gain

    qn = head_norm(q, qg_ref[...])
    kn = head_norm(k, kg_ref[...])

    lf = _log_sigmoid(small_ref[0] + bf_ref[...])
    cum = _scan_add(lf) + cum_ref[0:1, :]
    cum_ref[...] = jnp.broadcast_to(cum[tm - 1:tm, :], cum_ref.shape)

    cs = cum * LOG2E
    c1 = cs.astype(BF16).astype(F32)
    r1 = cs - c1
    c2 = r1.astype(BF16).astype(F32)
    c3 = (r1 - c2).astype(BF16).astype(F32)
    lane = lax.broadcasted_iota(jnp.int32, cum.shape, 1)
    H = FOX_HEADS
    packed = jnp.where(lane < H, c1, 0.0)
    packed = jnp.where((lane >= H) & (lane < 2 * H), pltpu.roll(c2, H, axis=1), packed)
    packed = jnp.where((lane >= 2 * H) & (lane < 3 * H), pltpu.roll(c3, 2 * H, axis=1), packed)
    packed = jnp.where(lane == ONE_LANE, 1.0, packed).astype(BF16)

    kbias = _dot(packed, rk_ref[...])
    qbias = _dot(packed, rq_ref[...])
    feat = lane < HEAD_DIM

    def head_aug(t, bias, h):
        src = t[:, (h // 2) * LANES:(h // 2 + 1) * LANES]
        if h % 2:
            src = pltpu.roll(src, HEAD_DIM, axis=1)
        return jnp.where(feat, src, bias[:, h * AUG:(h + 1) * AUG])

    vt = v.T
    tk = vt_ref.shape[4]
    for h in range(H):
        kaug_ref[0, h] = head_aug(kn, kbias, h).astype(BF16)
        qaugt_ref[0, h] = head_aug(qn, qbias, h).T.astype(BF16)
        for cb in range(tm // tk):
            vt_ref[0, h, cb, 0:HEAD_DIM, :] = (
                vt[h * HEAD_DIM:(h + 1) * HEAD_DIM, cb * tk:(cb + 1) * tk].astype(BF16))
            vt_ref[0, h, cb, HEAD_DIM:V_ROWS, :] = jnp.ones((V_ROWS - HEAD_DIM, tk), BF16)


def _fox_prep(fox, small, bf, qg, kg, ones_bd, rq, rk, tm, tk):
    B, S, _ = fox.shape
    H, W = FOX_HEADS, FOX_WIDTH
    return pl.pallas_call(
        _fox_prep_kernel,
        grid=(B, S // tm),
        in_specs=[
            pl.BlockSpec((1, tm, FOX_COLS), lambda b, i: (b, i, 0)),
            pl.BlockSpec((1, tm, SMALL_COLS), lambda b, i: (b, i, 0)),
            _const_spec((1, LANES)), _const_spec((1, W)), _const_spec((1, W)),
            _const_spec((W, W)),
            _const_spec((LANES, H * AUG)), _const_spec((LANES, H * AUG)),
        ],
        out_specs=[
            pl.BlockSpec((1, H, tm, AUG), lambda b, i: (b, 0, i, 0)),
            pl.BlockSpec((1, H, AUG, tm), lambda b, i: (b, 0, 0, i)),
            pl.BlockSpec((1, H, tm // tk, V_ROWS, tk), lambda b, i: (b, 0, i, 0, 0)),
        ],
        out_shape=[
            jax.ShapeDtypeStruct((B, H, S, AUG), BF16),
            jax.ShapeDtypeStruct((B, H, AUG, S), BF16),
            jax.ShapeDtypeStruct((B, H, S // tk, V_ROWS, tk), BF16),
        ],
        scratch_shapes=[pltpu.VMEM((SUBLANES, LANES), F32)],
        compiler_params=_cparams(("arbitrary", "arbitrary")),
        name="fox_prep",
    )(fox, small, bf, qg, kg, ones_bd, rq, rk)


def _fox_attn_kernel(k_ref, qt_ref, vt_ref, o_ref, acc_ref, sa_ref, sb_ref, *, tq):
    qi = pl.program_id(1)
    H = FOX_HEADS
    acc_ref[...] = jnp.zeros_like(acc_ref)

    def scores(j, s_ref):
        start = pl.multiple_of(j * tq, tq)
        cmax = []
        for h in range(H):
            s = _dot(k_ref[0, h, pl.ds(start, tq), :], qt_ref[0, h])
            s_ref[h] = s
            cmax.append(jnp.max(s, axis=0, keepdims=True))
        return tuple(cmax)

    def softmax(s_ref, cmax, ms, mask):
        new, ps = [], []
        for h in range(H):
            s = s_ref[h]
            if mask is None:
                bmax = cmax[h]
            else:
                s = jnp.where(mask, s, NEG_BIG)
                bmax = jnp.max(s, axis=0, keepdims=True)
            m_new = jnp.maximum(ms[h], bmax)
            new.append(m_new)
            ps.append((jnp.exp2(ms[h] - m_new), jnp.exp2(s - m_new).astype(BF16)))
        return tuple(new), ps

    def values(j, ps):
        for h in range(H):
            alpha, p = ps[h]
            acc_ref[h] = alpha * acc_ref[h] + _dot(vt_ref[0, h, j], p)

    def pair(t, carry):
        ms, ca = carry
        j0 = 2 * t
        ms, ps = softmax(sa_ref, ca, ms, None)
        cb = scores(j0 + 1, sb_ref)
        values(j0, ps)
        ms, ps = softmax(sb_ref, cb, ms, None)
        ca = scores(j0 + 2, sa_ref)
        values(j0 + 1, ps)
        return ms, ca

    carry = (tuple(jnp.full((1, tq), NEG_BIG, F32) for _ in range(H)), scores(0, sa_ref))
    n_pairs = lax.div(qi, 2)
    n_quads = lax.div(n_pairs, 2)
    carry = lax.fori_loop(0, n_quads, lambda u, c: pair(2 * u + 1, pair(2 * u, c)), carry)
    carry = lax.fori_loop(2 * n_quads, n_pairs, pair, carry)
    rel = (lax.broadcasted_iota(jnp.int32, (tq, tq), 0)
           - lax.broadcasted_iota(jnp.int32, (tq, tq), 1))
    j0 = 2 * n_pairs
    ms, ca = carry
    ms, ps = softmax(sa_ref, ca, ms, rel <= (qi - j0) * tq)
    values(j0, ps)

    @pl.when(qi > j0)
    def _():
        cb = scores(qi, sb_ref)
        values(qi, softmax(sb_ref, cb, ms, rel <= 0)[1])

    out = jnp.concatenate(
        [acc_ref[h, 0:HEAD_DIM, :] * (1.0 / acc_ref[h, HEAD_DIM:HEAD_DIM + 1, :]) for h in range(H)], axis=0)
    o_ref[0] = out.T.astype(o_ref.dtype)


def _fox_attention(kaug, qaugt, vt, tq):
    B, H, S, _ = kaug.shape
    return pl.pallas_call(
        functools.partial(_fox_attn_kernel, tq=tq),
        grid=(B, S // tq),
        in_specs=[
            pl.BlockSpec((1, H, S, AUG), lambda b, i: (b, 0, 0, 0), pipeline_mode=pl.Buffered(1)),
            pl.BlockSpec((1, H, AUG, tq), lambda b, i: (b, 0, 0, i)),
            pl.BlockSpec((1, H, S // tq, V_ROWS, tq), lambda b, i: (b, 0, 0, 0, 0),
                         pipeline_mode=pl.Buffered(1)),
        ],
        out_specs=pl.BlockSpec((1, tq, H * HEAD_DIM), lambda b, i: (b, i, 0)),
        out_shape=jax.ShapeDtypeStruct((B, S, H * HEAD_DIM), BF16),
        scratch_shapes=[pltpu.VMEM((H, V_ROWS, tq), F32), pltpu.VMEM((H, tq, tq), F32),
                        pltpu.VMEM((H, tq, tq), F32)],
        compiler_params=_cparams(("arbitrary", "arbitrary")),
        name="fox_attention",
    )(kaug, qaugt, vt)


GLA_ROWS = 256


def _gla_kernel(gla_ref, small_ref, wa_ref, ba_ref, gain_ref, tri_ref, hmask_ref, bdmask_ref,
                ones_ref, o_ref, st_ref):
    ts = o_ref.shape[1]
    KW, C, R = GLA_KW, GLA_CHUNK, GLA_ROWS

    @pl.when(pl.program_id(1) == 0)
    def _():
        st_ref[...] = jnp.zeros_like(st_ref)

    q = gla_ref[0, :, 0:KW].astype(F32) * (GLA_DK ** -0.5)
    k = gla_ref[0, :, KW:2 * KW].astype(F32)
    v = gla_ref[0, :, 2 * KW:3 * KW]
    low = small_ref[0].astype(BF16)
    la = _log_sigmoid(_dot(low, wa_ref[...]) + ba_ref[...]) * (1.0 / GLA_TAU)
    la_hi = la.astype(BF16)
    la_lo = (la - la_hi.astype(F32)).astype(BF16)
    tri = tri_ref[...]
    bcum = jnp.concatenate(
        [_dot(tri, la_hi[g0:g0 + R]) + _dot(tri, la_lo[g0:g0 + R]) for g0 in range(0, ts, R)], axis=0)
    ends = [bcum[c0 + C - 1:c0 + C, :] for c0 in range(0, ts, C)]
    b_last = jnp.concatenate([jnp.broadcast_to(e, (C, KW)) for e in ends], axis=0)
    q_dec = (q * jnp.exp(bcum)).astype(BF16)
    k_dec = (k * jnp.exp(-bcum)).astype(BF16)
    k_end = (k * jnp.exp(b_last - bcum)).astype(BF16)

    tri_f = tri.astype(F32)
    o_groups = []
    for g0 in range(0, ts, R):
        rows = slice(g0, g0 + R)
        o_acc = jnp.zeros((R, KW), F32)
        for h in range(GLA_HEADS):
            hm = hmask_ref[h:h + 1, :].astype(BF16)
            att = lax.dot_general(q_dec[rows] * hm, k_dec[rows], (((1,), (1,)), ((), ())),
                                  preferred_element_type=F32)
            o_acc = o_acc + _dot((att * tri_f).astype(BF16), v[rows] * hm)
        o_groups.append(o_acc)

    o_parts = []
    for ci, c0 in enumerate(range(0, ts, C)):
        cs = slice(c0, c0 + C)
        st = st_ref[...]
        o_parts.append(lax.dot_general(q_dec[cs], st.astype(BF16), (((1,), (1,)), ((), ())),
                                       preferred_element_type=F32))
        kv_t = lax.dot_general(v[cs], k_end[cs], (((0,), (0,)), ((), ())),
                               preferred_element_type=F32)
        st_ref[...] = st * jnp.exp(ends[ci]) + kv_t * bdmask_ref[...]
    o = jnp.concatenate(o_groups, axis=0) + jnp.concatenate(o_parts, axis=0)

    ms = _dot((o * o).astype(BF16), ones_ref[...]) * (1.0 / GLA_DK)
    o = o * lax.rsqrt(ms + EPS) * gain_ref[...]
    g = gla_ref[0, :, 3 * KW:4 * KW].astype(F32)
    o_ref[0] = (o * (g * _sigmoid(g))).astype(o_ref.dtype)


def _gla_mixer(gla, small, wa, ba, gain, tri, hmask, bdmask, ones_bd, ts):
    B, S, _ = gla.shape
    KW = GLA_KW
    return pl.pallas_call(
        _gla_kernel,
        grid=(B, S // ts),
        in_specs=[
            pl.BlockSpec((1, ts, GLA_COLS), lambda b, i: (b, i, 0)),
            pl.BlockSpec((1, ts, SMALL_COLS), lambda b, i: (b, i, 0)),
            _const_spec((LANES, KW)), _const_spec((1, KW)), _const_spec((1, KW)),
            _const_spec((GLA_ROWS, GLA_ROWS)), _const_spec((GLA_HEADS, KW)),
            _const_spec((KW, KW)), _const_spec((KW, KW)),
        ],
        out_specs=pl.BlockSpec((1, ts, KW), lambda b, i: (b, i, 0)),
        out_shape=jax.ShapeDtypeStruct((B, S, KW), BF16),
        scratch_shapes=[pltpu.VMEM((KW, KW), F32)],
        compiler_params=_cparams(("arbitrary", "arbitrary")),
        name="gla_mixer",
    )(gla, small, wa, ba, gain, tri, hmask, bdmask, ones_bd)


def _ffn_chunks(d_ff):
    chunks, c0 = [], 0
    while c0 < d_ff:
        n = min(512, d_ff - c0)
        chunks.append((c0, n))
        c0 += n
    return chunks


def _out_ffn_kernel(x_ref, a_ref, b_ref, c_ref, mod_ref, g_ref, wo_ref, wgu_ref, wd_ref, o_ref,
                    acc_ref):
    d_ff = wd_ref.shape[0]
    mix = _dot(jnp.concatenate([a_ref[0], b_ref[0], c_ref[0]], axis=1), wo_ref[...])
    x1 = x_ref[0] + mod_ref[0, 2:3, :] * mix
    h = _rms_mod(x1, g_ref[...], mod_ref[0, 4:5, :], mod_ref[0, 3:4, :]).astype(BF16)
    for idx, (c0, n) in enumerate(_ffn_chunks(d_ff)):
        gt = _dot(h, wgu_ref[:, c0:c0 + n])
        up = _dot(h, wgu_ref[:, d_ff + c0:d_ff + c0 + n])
        act = (gt * _sigmoid(gt) * up).astype(BF16)
        y = _dot(act, wd_ref[c0:c0 + n, :])
        if idx == 0:
            acc_ref[...] = y
        else:
            acc_ref[...] += y
    o_ref[0] = x1 + mod_ref[0, 5:6, :] * acc_ref[...]


def _out_ffn(x, a, b, c, mod, gain, wo, wgu, wd, layer, tm):
    B, S, D = x.shape
    tok = lambda n: pl.BlockSpec((1, tm, n), lambda bb, i: (bb, i, 0))
    return pl.pallas_call(
        _out_ffn_kernel,
        grid=(B, S // tm),
        in_specs=[
            tok(D), tok(a.shape[2]), tok(b.shape[2]), tok(c.shape[2]),
            pl.BlockSpec((1, N_MOD, D), lambda bb, i: (bb, 0, 0)),
            _const_spec((1, D)),
            _layer_weight_spec(wo.shape, layer), _layer_weight_spec(wgu.shape, layer),
            _layer_weight_spec(wd.shape, layer),
        ],
        out_specs=tok(D),
        out_shape=jax.ShapeDtypeStruct((B, S, D), F32),
        scratch_shapes=[pltpu.VMEM((tm, D), F32)],
        compiler_params=_cparams(("arbitrary", "arbitrary")),
        name="out_proj_ffn",
    )(x, a, b, c, mod, gain, wo, wgu, wd)


def _block_diag(w):
    L, n, d, e = w.shape
    eye = jnp.eye(n, dtype=w.dtype)
    return jnp.einsum("lnde,nm->lndme", w, eye).reshape(L, n * d, n * e)


def _regroup_w_in(w_in):
    L, D, _ = w_in.shape
    sizes = (LRU_WIDTH, LRU_WIDTH, FOX_WIDTH, FOX_WIDTH, FOX_WIDTH, FOX_HEADS,
             GLA_KW, GLA_KW, GLA_VW, GLA_RANK, GLA_VW)
    offs = np.concatenate([[0], np.cumsum(sizes)])
    seg = lambda i: w_in[:, :, offs[i]:offs[i + 1]]
    z = lambda n: jnp.zeros((L, D, n), w_in.dtype)
    cols = [seg(0), seg(1), seg(2), seg(3), seg(4), seg(6), seg(7), seg(8), seg(10),
            seg(5), z(GLOW_LANE0 - FOX_HEADS), seg(9), z(SMALL_COLS - GLOW_LANE0 - GLA_RANK)]
    return jnp.concatenate(cols, axis=-1).astype(BF16)


def _head_ones(n_heads, dim):
    return jnp.asarray(np.kron(np.eye(n_heads), np.ones((dim, dim))), BF16)


def _fox_constants():
    H, W = FOX_HEADS, FOX_WIDTH
    rq = np.zeros((LANES, H * AUG), np.float32)
    rk = np.zeros((LANES, H * AUG), np.float32)
    for h in range(H):
        for s in range(N_SPLIT):
            rq[s * H + h, h * AUG + HEAD_DIM + s] = 1.0
            rq[ONE_LANE, h * AUG + HEAD_DIM + N_SPLIT + s] = 1.0
            rk[ONE_LANE, h * AUG + HEAD_DIM + s] = 1.0
            rk[s * H + h, h * AUG + HEAD_DIM + N_SPLIT + s] = -1.0
    return jnp.asarray(rq, BF16), jnp.asarray(rk, BF16)


def _gla_constants():
    R, C, KW = GLA_ROWS, GLA_CHUNK, GLA_KW
    i = np.arange(R)
    tri = ((i[:, None] >= i[None, :]) & (i[:, None] // C == i[None, :] // C)).astype(np.float32)
    lane = np.arange(KW)
    hmask = (lane[None, :] // GLA_DK == np.arange(GLA_HEADS)[:, None]).astype(np.float32)
    bdmask = (lane[:, None] // GLA_DK == lane[None, :] // GLA_DK).astype(np.float32)
    return jnp.asarray(tri, BF16), jnp.asarray(hmask, F32), jnp.asarray(bdmask, F32)


def _pick_tile(n, want):
    t = min(n, want)
    while n % t:
        t //= 2
    return t


def kernel(x, c, norm1_gain, norm2_gain, w_mod, b_mod, w_in, conv_w, conv_b, lru_w_r, lru_b_r,
           lru_w_i, lru_b_i, lru_lambda, fox_b_f, fox_q_gain, fox_k_gain, gla_w_alpha, gla_b_alpha,
           gla_out_gain, w_out, ffn_w_gate_up, ffn_w_down):
    B, S, D = x.shape
    L = w_in.shape[0]
    tm = _pick_tile(S, 512)
    tq = _pick_tile(S, 256)

    mod = _modulation(c, w_mod, b_mod)
    w_in_p = _regroup_w_in(w_in)
    wr_bd = _block_diag(lru_w_r).astype(BF16)
    wi_bd = _block_diag(lru_w_i).astype(BF16)
    w_out_b = w_out.astype(BF16)
    wgu_b = ffn_w_gate_up.astype(BF16)
    wd_b = ffn_w_down.astype(BF16)
    row = lambda a: a.reshape(L, 1, a.shape[-1])
    bf_pad = jnp.pad(fox_b_f, ((0, 0), (0, LANES - FOX_HEADS))).reshape(L, 1, LANES)
    qg = jnp.tile(fox_q_gain * (HEAD_DIM ** -0.5 * LOG2E), (1, FOX_HEADS)).reshape(L, 1, FOX_WIDTH)
    kg = jnp.tile(fox_k_gain, (1, FOX_HEADS)).reshape(L, 1, FOX_WIDTH)
    wa_pad = jnp.pad(gla_w_alpha, ((0, 0), (GLOW_LANE0, LANES - GLOW_LANE0 - GLA_RANK), (0, 0))).astype(BF16)
    og = jnp.tile(gla_out_gain, (1, GLA_HEADS)).reshape(L, 1, GLA_VW)
    fox_ones = _head_ones(FOX_HEADS, HEAD_DIM)
    gla_ones = _head_ones(GLA_HEADS, GLA_DK)
    rq, rk = _fox_constants()
    tri, hmask, bdmask = _gla_constants()

    for l in range(L):
        lru, fox, gla, small = _in_proj(x, mod[l], row(norm1_gain)[l], w_in_p, l, tm)
        out_a = _lru_mixer(lru, conv_w[l], row(conv_b)[l], wr_bd[l], row(lru_b_r)[l], wi_bd[l],
                           row(lru_b_i)[l], row(lru_lambda)[l], tm)
        kaug, qaugt, vt = _fox_prep(fox, small, bf_pad[l], qg[l], kg[l], fox_ones, rq, rk, tm, tq)
        out_b = _fox_attention(kaug, qaugt, vt, tq)
        out_c = _gla_mixer(gla, small, wa_pad[l], row(gla_b_alpha)[l], og[l], tri, hmask, bdmask,
                           gla_ones, tm)
        x = _out_ffn(x, out_a, out_b, out_c, mod[l], row(norm2_gain)[l], w_out_b, wgu_b, wd_b, l, tm)
    return x
```

```python
import functools

import numpy as np
import jax
import jax.numpy as jnp
from jax import lax
from jax.experimental import pallas as pl
from jax.experimental.pallas import tpu as pltpu

F32 = jnp.float32
BF16 = jnp.bfloat16

EPS = 1e-6
HEAD_DIM = 64
LRU_WIDTH = 384
LRU_BLOCKS = 6
CONV_WIDTH = 4
LRU_C = 8.0
FOX_HEADS = 6
FOX_WIDTH = FOX_HEADS * HEAD_DIM
GLA_HEADS = 4
GLA_DK = 64
GLA_KW = GLA_HEADS * GLA_DK
GLA_VW = GLA_KW
GLA_RANK = 16
GLA_TAU = 16.0
GLA_CHUNK = 64
N_MOD = 6

LANES = 128
SUBLANES = 8
VMEM_LIMIT_BYTES = 56 * 1024 * 1024

LRU_COLS = 2 * LRU_WIDTH
FOX_COLS = 3 * FOX_WIDTH
GLA_COLS = 3 * GLA_KW + GLA_VW
SMALL_COLS = LANES
GLOW_LANE0 = 8
NP_COLS = LRU_COLS + FOX_COLS + GLA_COLS + SMALL_COLS
IN_PROJ_GROUP = 768

AUG = LANES
N_SPLIT = 3
ONE_LANE = N_SPLIT * FOX_HEADS
NEG_BIG = -1e30
V_ROWS = HEAD_DIM + 16
LOG2E = float(np.log2(np.e))


def _sigmoid(x):
    return 0.5 * jnp.tanh(0.5 * x) + 0.5


def _log_sigmoid(x):
    return jnp.minimum(x, 0.0) - jnp.log(1.0 + jnp.exp(-jnp.abs(x)))


def _softplus(x):
    return jnp.maximum(x, 0.0) + jnp.log(1.0 + jnp.exp(-jnp.abs(x)))


def _gelu_tanh(x):
    c = np.float32(np.sqrt(2.0 / np.pi))
    return 0.5 * x * (1.0 + jnp.tanh(c * (x + 0.044715 * (x * x * x))))


def _dot(a, b):
    return jnp.dot(a, b, preferred_element_type=F32)


def _cparams(sem):
    return pltpu.CompilerParams(dimension_semantics=sem, vmem_limit_bytes=VMEM_LIMIT_BYTES)


def _const_spec(shape):
    nd = len(shape)
    return pl.BlockSpec(shape, lambda *_: (0,) * nd)


def _layer_weight_spec(stacked_shape, layer):
    tail = tuple(stacked_shape[1:])
    return pl.BlockSpec((None,) + tail, lambda *_: (layer,) + (0,) * len(tail),
                        pipeline_mode=pl.Buffered(1))


def _mod_kernel(c_ref, w_ref, b_ref, o_ref):
    c = c_ref[...]
    ca = (c * _sigmoid(c)).astype(BF16)
    o_ref[0] = _dot(ca, w_ref[0].astype(BF16)) + b_ref[0]


def _modulation(c, w_mod, b_mod):
    L, D, N = w_mod.shape
    B = c.shape[0]
    bp = -(-B // SUBLANES) * SUBLANES
    cp = jnp.pad(c, ((0, bp - B), (0, 0)))
    tn = 1024
    out = pl.pallas_call(
        _mod_kernel,
        grid=(L, N // tn),
        in_specs=[
            pl.BlockSpec((bp, D), lambda l, n: (0, 0)),
            pl.BlockSpec((1, D, tn), lambda l, n: (l, 0, n)),
            pl.BlockSpec((1, 1, tn), lambda l, n: (l, 0, n)),
        ],
        out_specs=pl.BlockSpec((1, bp, tn), lambda l, n: (l, 0, n)),
        out_shape=jax.ShapeDtypeStruct((L, bp, N), F32),
        compiler_params=_cparams(("arbitrary", "arbitrary")),
        name="adaln_mod",
    )(cp, w_mod, b_mod.reshape(L, 1, N))
    return out[:, :B].reshape(L, B, N_MOD, D)


def _rms_mod(x, gain, scale, shift):
    ms = jnp.mean(x * x, axis=-1, keepdims=True)
    y = x * lax.rsqrt(ms + EPS) * gain
    return y * (1.0 + scale) + shift


def _in_proj_kernel(x_ref, mod_ref, g_ref, w_ref, cw_ref, cb_ref, wr_ref, br_ref, wi_ref, bi_ref,
                    lam_ref, oa_ref, fox_ref, gla_ref, small_ref, xs_ref, h_ref):
    @pl.when(pl.program_id(1) == 0)
    def _():
        xs_ref[0:SUBLANES, :] = jnp.zeros((SUBLANES, LRU_WIDTH), F32)
        h_ref[...] = jnp.zeros_like(h_ref)

    h = _rms_mod(x_ref[0], g_ref[...], mod_ref[0, 1:2, :], mod_ref[0, 0:1, :]).astype(BF16)
    segs, c0 = [], LRU_COLS
    for ref, n in ((fox_ref, FOX_COLS), (gla_ref, GLA_COLS), (small_ref, SMALL_COLS)):
        segs.append((ref, c0, c0 + n))
        c0 += n

    def project(g0):
        g1 = g0 + IN_PROJ_GROUP
        y = _dot(h, w_ref[:, g0:g1])
        for ref, s0, s1 in segs:
            lo, hi = max(g0, s0), min(g1, s1)
            if lo < hi:
                ref[0, :, lo - s0:hi - s0] = y[:, lo - g0:hi - g0].astype(ref.dtype)

    assert LRU_COLS == IN_PROJ_GROUP and NP_COLS == 4 * IN_PROJ_GROUP
    y0 = _dot(h, w_ref[:, 0:LRU_COLS])
    for s in range(LRU_WIDTH // LANES):
        sl = slice(s * LANES, (s + 1) * LANES)
        xc, r_pre, i_pre = _lru_slab_gates(y0[:, sl], sl, cw_ref, cb_ref, wr_ref, wi_ref, xs_ref)
        project((s + 1) * IN_PROJ_GROUP)
        ya = y0[:, LRU_WIDTH + s * LANES:LRU_WIDTH + (s + 1) * LANES]
        oa_ref[0, :, sl] = _lru_slab_finish(xc, r_pre, i_pre, ya, sl, br_ref, bi_ref, lam_ref,
                                            h_ref).astype(oa_ref.dtype)


def _in_proj(x, mod, gain, w, lru_p, layer, tm):
    B, S, D = x.shape
    W = LRU_WIDTH
    tok = lambda n: pl.BlockSpec((1, tm, n), lambda b, i: (b, i, 0))
    return pl.pallas_call(
        _in_proj_kernel,
        grid=(B, S // tm),
        in_specs=[
            tok(D),
            pl.BlockSpec((1, N_MOD, D), lambda b, i: (b, 0, 0)),
            _const_spec((1, D)),
            _layer_weight_spec(w.shape, layer),
            _const_spec((CONV_WIDTH, W)), _const_spec((1, W)),
            _const_spec((W, W)), _const_spec((1, W)),
            _const_spec((W, W)), _const_spec((1, W)),
            _const_spec((1, W)),
        ],
        out_specs=[tok(W), tok(FOX_COLS), tok(GLA_COLS), tok(SMALL_COLS)],
        out_shape=[
            jax.ShapeDtypeStruct((B, S, W), BF16),
            jax.ShapeDtypeStruct((B, S, FOX_COLS), BF16),
            jax.ShapeDtypeStruct((B, S, GLA_COLS), BF16),
            jax.ShapeDtypeStruct((B, S, SMALL_COLS), F32),
        ],
        scratch_shapes=[pltpu.VMEM((tm + SUBLANES, W), F32), pltpu.VMEM((SUBLANES, W), F32)],
        compiler_params=_cparams(("arbitrary", "arbitrary")),
        name="in_proj_lru",
    )(x, mod, gain, w, *lru_p)


def _scan_add(v):
    n = v.shape[0]
    row = lax.broadcasted_iota(jnp.int32, v.shape, 0)
    d = 1
    while d < n:
        v = v + jnp.where(row >= d, pltpu.roll(v, d, axis=0), 0.0)
        d *= 2
    return v


def _scan_linear(a, u, h0):
    n, w = a.shape
    g = n // SUBLANES
    a = a.reshape(g, SUBLANES, w)
    u = u.reshape(g, SUBLANES, w)
    sub = lax.broadcasted_iota(jnp.int32, a.shape, 1)
    d = 1
    while d < SUBLANES:
        m = sub >= d
        u = u + jnp.where(m, a * pltpu.roll(u, d, axis=1), 0.0)
        a = jnp.where(m, a * pltpu.roll(a, d, axis=1), a)
        d *= 2
    hs, h = [], h0
    for v in range(g):
        hv = a[v] * h + u[v]
        h = hv[SUBLANES - 1:SUBLANES, :]
        hs.append(hv)
    return jnp.concatenate(hs, axis=0), h


def _lru_slab_gates(xa, sl, cw_ref, cb_ref, wr_ref, wi_ref, xs_ref):
    ts = xa.shape[0]
    xs_ref[SUBLANES:SUBLANES + ts, sl] = xa
    xc = xa * cw_ref[CONV_WIDTH - 1:CONV_WIDTH, sl] + cb_ref[:, sl]
    for k in range(1, CONV_WIDTH):
        j = CONV_WIDTH - 1 - k
        xc = xc + xs_ref[pl.ds(SUBLANES - k, ts), sl] * cw_ref[j:j + 1, sl]
    xs_ref[0:SUBLANES, sl] = xs_ref[ts:ts + SUBLANES, sl]
    xcb = xc.astype(BF16)
    return xc, _dot(xcb, wr_ref[sl, sl]), _dot(xcb, wi_ref[sl, sl])


def _lru_slab_finish(xc, r_pre, i_pre, ya, sl, br_ref, bi_ref, lam_ref, h_ref):
    r = _sigmoid(r_pre + br_ref[:, sl])
    i = _sigmoid(i_pre + bi_ref[:, sl])
    log_a = (-LRU_C) * r * _softplus(-lam_ref[:, sl])
    a = jnp.exp(log_a)
    mult = jnp.sqrt(1.0 - a * a)
    u = mult * (i * xc)
    h, h_last = _scan_linear(a, u, h_ref[0:1, sl])
    h_ref[:, sl] = jnp.broadcast_to(h_last, (h_ref.shape[0], h_last.shape[1]))
    return h * _gelu_tanh(ya)


def _fox_prep_kernel(fox_ref, small_ref, bf_ref, qg_ref, kg_ref, ones_ref,
                     rq_ref, rk_ref, kaug_ref, qaugt_ref, vt_ref, cum_ref):
    tm = fox_ref.shape[1]
    W = FOX_WIDTH

    @pl.when(pl.program_id(1) == 0)
    def _():
        cum_ref[...] = jnp.zeros_like(cum_ref)

    q = fox_ref[0, :, 0:W].astype(F32)
    k = fox_ref[0, :, W:2 * W].astype(F32)
    v = fox_ref[0, :, 2 * W:3 * W].astype(F32)

    def head_norm(t, gain):
        ms = _dot((t * t).astype(BF16), ones_ref[...]) * (1.0 / HEAD_DIM)
        return t * lax.rsqrt(ms + EPS) * gain

    qn = head_norm(q, qg_ref[...])
    kn = head_norm(k, kg_ref[...])

    lf = _log_sigmoid(small_ref[0] + bf_ref[...])
    cum = _scan_add(lf) + cum_ref[0:1, :]
    cum_ref[...] = jnp.broadcast_to(cum[tm - 1:tm, :], cum_ref.shape)

    cs = cum * LOG2E
    c1 = cs.astype(BF16).astype(F32)
    r1 = cs - c1
    c2 = r1.astype(BF16).astype(F32)
    c3 = (r1 - c2).astype(BF16).astype(F32)
    lane = lax.broadcasted_iota(jnp.int32, cum.shape, 1)
    H = FOX_HEADS
    packed = jnp.where(lane < H, c1, 0.0)
    packed = jnp.where((lane >= H) & (lane < 2 * H), pltpu.roll(c2, H, axis=1), packed)
    packed = jnp.where((lane >= 2 * H) & (lane < 3 * H), pltpu.roll(c3, 2 * H, axis=1), packed)
    packed = jnp.where(lane == ONE_LANE, 1.0, packed).astype(BF16)

    kbias = _dot(packed, rk_ref[...])
    qbias = _dot(packed, rq_ref[...])
    feat = lane < HEAD_DIM

    def head_aug(t, bias, h):
        src = t[:, (h // 2) * LANES:(h // 2 + 1) * LANES]
        if h % 2:
            src = pltpu.roll(src, HEAD_DIM, axis=1)
        return jnp.where(feat, src, bias[:, h * AUG:(h + 1) * AUG])

    vt = v.T
    tk = vt_ref.shape[4]
    for h in range(H):
        kaug_ref[0, h] = head_aug(kn, kbias, h).astype(BF16)
        qaugt_ref[0, h] = head_aug(qn, qbias, h).T.astype(BF16)
        for cb in range(tm // tk):
            vt_ref[0, h, cb, 0:HEAD_DIM, :] = (
                vt[h * HEAD_DIM:(h + 1) * HEAD_DIM, cb * tk:(cb + 1) * tk].astype(BF16))
            vt_ref[0, h, cb, HEAD_DIM:V_ROWS, :] = jnp.ones((V_ROWS - HEAD_DIM, tk), BF16)


def _fox_prep(fox, small, bf, qg, kg, ones_bd, rq, rk, tm, tk):
    B, S, _ = fox.shape
    H, W = FOX_HEADS, FOX_WIDTH
    return pl.pallas_call(
        _fox_prep_kernel,
        grid=(B, S // tm),
        in_specs=[
            pl.BlockSpec((1, tm, FOX_COLS), lambda b, i: (b, i, 0)),
            pl.BlockSpec((1, tm, SMALL_COLS), lambda b, i: (b, i, 0)),
            _const_spec((1, LANES)), _const_spec((1, W)), _const_spec((1, W)),
            _const_spec((W, W)),
            _const_spec((LANES, H * AUG)), _const_spec((LANES, H * AUG)),
        ],
        out_specs=[
            pl.BlockSpec((1, H, tm, AUG), lambda b, i: (b, 0, i, 0)),
            pl.BlockSpec((1, H, AUG, tm), lambda b, i: (b, 0, 0, i)),
            pl.BlockSpec((1, H, tm // tk, V_ROWS, tk), lambda b, i: (b, 0, i, 0, 0)),
        ],
        out_shape=[
            jax.ShapeDtypeStruct((B, H, S, AUG), BF16),
            jax.ShapeDtypeStruct((B, H, AUG, S), BF16),
            jax.ShapeDtypeStruct((B, H, S // tk, V_ROWS, tk), BF16),
        ],
        scratch_shapes=[pltpu.VMEM((SUBLANES, LANES), F32)],
        compiler_params=_cparams(("arbitrary", "arbitrary")),
        name="fox_prep",
    )(fox, small, bf, qg, kg, ones_bd, rq, rk)


def _fox_attn_kernel(k_ref, qt_ref, vt_ref, o_ref, acc_ref, sa_ref, sb_ref, *, tq):
    qi = pl.program_id(1)
    H = FOX_HEADS
    acc_ref[...] = jnp.zeros_like(acc_ref)

    def scores(j, s_ref):
        start = pl.multiple_of(j * tq, tq)
        cmax = []
        for h in range(H):
            s = _dot(k_ref[0, h, pl.ds(start, tq), :], qt_ref[0, h])
            s_ref[h] = s
            cmax.append(jnp.max(s, axis=0, keepdims=True))
        return tuple(cmax)

    def update(h, j, s, m_old, m_new):
        p = jnp.exp2(s - m_new).astype(BF16)
        acc_ref[h] = jnp.exp2(m_old - m_new) * acc_ref[h] + _dot(vt_ref[0, h, j], p)

    def masked_step(j, cur_ref, mask, ms, nxt_ref=None, j_nxt=None):
        new_ms = []
        for h in range(H):
            if nxt_ref is not None:
                start = pl.multiple_of(j_nxt * tq, tq)
                nxt_ref[h] = _dot(k_ref[0, h, pl.ds(start, tq), :], qt_ref[0, h])
            s = jnp.where(mask, cur_ref[h], NEG_BIG)
            m_new = jnp.maximum(ms[h], jnp.max(s, axis=0, keepdims=True))
            new_ms.append(m_new)
            update(h, j, s, ms[h], m_new)
        return tuple(new_ms)

    def step(j, cur_ref, c_cur, nxt_ref, ms):
        start = pl.multiple_of((j + 1) * tq, tq)
        new_ms, c_nxt = [], []
        for h in range(H):
            s_n = _dot(k_ref[0, h, pl.ds(start, tq), :], qt_ref[0, h])
            nxt_ref[h] = s_n
            c_nxt.append(jnp.max(s_n, axis=0, keepdims=True))
            m_new = jnp.maximum(ms[h], c_cur[h])
            new_ms.append(m_new)
            update(h, j, cur_ref[h], ms[h], m_new)
        return tuple(new_ms), tuple(c_nxt)

    def pair(t, carry):
        ms, ca = carry
        j0 = 2 * t
        ms, cb = step(j0, sa_ref, ca, sb_ref, ms)
        ms, ca = step(j0 + 1, sb_ref, cb, sa_ref, ms)
        return ms, ca

    carry = (tuple(jnp.full((1, tq), NEG_BIG, F32) for _ in range(H)), scores(0, sa_ref))
    n_pairs = lax.div(qi, 2)
    n_quads = lax.div(n_pairs, 2)
    carry = lax.fori_loop(0, n_quads, lambda u, c: pair(2 * u + 1, pair(2 * u, c)), carry)
    carry = lax.fori_loop(2 * n_quads, n_pairs, pair, carry)
    rel = (lax.broadcasted_iota(jnp.int32, (tq, tq), 0)
           - lax.broadcasted_iota(jnp.int32, (tq, tq), 1))
    j0 = 2 * n_pairs
    ms = masked_step(j0, sa_ref, rel <= (qi - j0) * tq, carry[0], sb_ref, qi)

    @pl.when(qi > j0)
    def _():
        masked_step(qi, sb_ref, rel <= 0, ms)

    out = jnp.concatenate(
        [acc_ref[h, 0:HEAD_DIM, :] * (1.0 / acc_ref[h, HEAD_DIM:HEAD_DIM + 1, :]) for h in range(H)], axis=0)
    o_ref[0] = out.T.astype(o_ref.dtype)


def _fox_attention(kaug, qaugt, vt, tq):
    B, H, S, _ = kaug.shape
    return pl.pallas_call(
        functools.partial(_fox_attn_kernel, tq=tq),
        grid=(B, S // tq),
        in_specs=[
            pl.BlockSpec((1, H, S, AUG), lambda b, i: (b, 0, 0, 0), pipeline_mode=pl.Buffered(1)),
            pl.BlockSpec((1, H, AUG, tq), lambda b, i: (b, 0, 0, i)),
            pl.BlockSpec((1, H, S // tq, V_ROWS, tq), lambda b, i: (b, 0, 0, 0, 0),
                         pipeline_mode=pl.Buffered(1)),
        ],
        out_specs=pl.BlockSpec((1, tq, H * HEAD_DIM), lambda b, i: (b, i, 0)),
        out_shape=jax.ShapeDtypeStruct((B, S, H * HEAD_DIM), BF16),
        scratch_shapes=[pltpu.VMEM((H, V_ROWS, tq), F32), pltpu.VMEM((H, tq, tq), F32),
                        pltpu.VMEM((H, tq, tq), F32)],
        compiler_params=_cparams(("arbitrary", "arbitrary")),
        name="fox_attention",
    )(kaug, qaugt, vt)


GLA_ROWS = 256


def _gla_kernel(gla_ref, small_ref, wa_ref, ba_ref, gain_ref, tri_ref, hmask_ref, bdmask_ref,
                ones_ref, o_ref, st_ref):
    ts = o_ref.shape[1]
    KW, C, R = GLA_KW, GLA_CHUNK, GLA_ROWS

    @pl.when(pl.program_id(1) == 0)
    def _():
        st_ref[...] = jnp.zeros_like(st_ref)

    q = gla_ref[0, :, 0:KW].astype(F32) * (GLA_DK ** -0.5)
    k = gla_ref[0, :, KW:2 * KW].astype(F32)
    v = gla_ref[0, :, 2 * KW:3 * KW]
    low = small_ref[0].astype(BF16)
    la = _log_sigmoid(_dot(low, wa_ref[...]) + ba_ref[...]) * (1.0 / GLA_TAU)
    la_hi = la.astype(BF16)
    la_lo = (la - la_hi.astype(F32)).astype(BF16)
    tri = tri_ref[...]
    bcum = jnp.concatenate(
        [_dot(tri, la_hi[g0:g0 + R]) + _dot(tri, la_lo[g0:g0 + R]) for g0 in range(0, ts, R)], axis=0)
    ends = [bcum[c0 + C - 1:c0 + C, :] for c0 in range(0, ts, C)]
    b_last = jnp.concatenate([jnp.broadcast_to(e, (C, KW)) for e in ends], axis=0)
    q_dec = (q * jnp.exp(bcum)).astype(BF16)
    k_dec = (k * jnp.exp(-bcum)).astype(BF16)
    k_end = (k * jnp.exp(b_last - bcum)).astype(BF16)

    tri_f = tri.astype(F32)
    o_groups = []
    for g0 in range(0, ts, R):
        rows = slice(g0, g0 + R)
        o_acc = jnp.zeros((R, KW), F32)
        for h in range(GLA_HEADS):
            hm = hmask_ref[h:h + 1, :].astype(BF16)
            att = lax.dot_general(q_dec[rows] * hm, k_dec[rows], (((1,), (1,)), ((), ())),
                                  preferred_element_type=F32)
            o_acc = o_acc + _dot((att * tri_f).astype(BF16), v[rows] * hm)
        o_groups.append(o_acc)

    o_parts = []
    for ci, c0 in enumerate(range(0, ts, C)):
        cs = slice(c0, c0 + C)
        st = st_ref[...]
        o_parts.append(lax.dot_general(q_dec[cs], st.astype(BF16), (((1,), (1,)), ((), ())),
                                       preferred_element_type=F32))
        kv_t = lax.dot_general(v[cs], k_end[cs], (((0,), (0,)), ((), ())),
                               preferred_element_type=F32)
        st_ref[...] = st * jnp.exp(ends[ci]) + kv_t * bdmask_ref[...]
    o = jnp.concatenate(o_groups, axis=0) + jnp.concatenate(o_parts, axis=0)

    ms = _dot((o * o).astype(BF16), ones_ref[...]) * (1.0 / GLA_DK)
    o = o * lax.rsqrt(ms + EPS) * gain_ref[...]
    g = gla_ref[0, :, 3 * KW:4 * KW].astype(F32)
    o_ref[0] = (o * (g * _sigmoid(g))).astype(o_ref.dtype)


def _gla_mixer(gla, small, wa, ba, gain, tri, hmask, bdmask, ones_bd, ts):
    B, S, _ = gla.shape
    KW = GLA_KW
    return pl.pallas_call(
        _gla_kernel,
        grid=(B, S // ts),
        in_specs=[
            pl.BlockSpec((1, ts, GLA_COLS), lambda b, i: (b, i, 0)),
            pl.BlockSpec((1, ts, SMALL_COLS), lambda b, i: (b, i, 0)),
            _const_spec((LANES, KW)), _const_spec((1, KW)), _const_spec((1, KW)),
            _const_spec((GLA_ROWS, GLA_ROWS)), _const_spec((GLA_HEADS, KW)),
            _const_spec((KW, KW)), _const_spec((KW, KW)),
        ],
        out_specs=pl.BlockSpec((1, ts, KW), lambda b, i: (b, i, 0)),
        out_shape=jax.ShapeDtypeStruct((B, S, KW), BF16),
        scratch_shapes=[pltpu.VMEM((KW, KW), F32)],
        compiler_params=_cparams(("arbitrary", "arbitrary")),
        name="gla_mixer",
    )(gla, small, wa, ba, gain, tri, hmask, bdmask, ones_bd)


def _ffn_chunks(d_ff):
    chunks, c0 = [], 0
    while c0 < d_ff:
        n = min(512, d_ff - c0)
        chunks.append((c0, n))
        c0 += n
    return chunks


def _out_ffn_kernel(x_ref, a_ref, b_ref, c_ref, mod_ref, g_ref, wo_ref, wgu_ref, wd_ref, o_ref,
                    acc_ref):
    d_ff = wd_ref.shape[0]
    mix = _dot(jnp.concatenate([a_ref[0], b_ref[0], c_ref[0]], axis=1), wo_ref[...])
    x1 = x_ref[0] + mod_ref[0, 2:3, :] * mix
    h = _rms_mod(x1, g_ref[...], mod_ref[0, 4:5, :], mod_ref[0, 3:4, :]).astype(BF16)
    for idx, (c0, n) in enumerate(_ffn_chunks(d_ff)):
        gt = _dot(h, wgu_ref[:, c0:c0 + n])
        up = _dot(h, wgu_ref[:, d_ff + c0:d_ff + c0 + n])
        act = (gt * _sigmoid(gt) * up).astype(BF16)
        y = _dot(act, wd_ref[c0:c0 + n, :])
        if idx == 0:
            acc_ref[...] = y
        else:
            acc_ref[...] += y
    o_ref[0] = x1 + mod_ref[0, 5:6, :] * acc_ref[...]


def _out_ffn(x, a, b, c, mod, gain, wo, wgu, wd, layer, tm):
    B, S, D = x.shape
    tok = lambda n: pl.BlockSpec((1, tm, n), lambda bb, i: (bb, i, 0))
    return pl.pallas_call(
        _out_ffn_kernel,
        grid=(B, S // tm),
        in_specs=[
            tok(D), tok(a.shape[2]), tok(b.shape[2]), tok(c.shape[2]),
            pl.BlockSpec((1, N_MOD, D), lambda bb, i: (bb, 0, 0)),
            _const_spec((1, D)),
            _layer_weight_spec(wo.shape, layer), _layer_weight_spec(wgu.shape, layer),
            _layer_weight_spec(wd.shape, layer),
        ],
        out_specs=tok(D),
        out_shape=jax.ShapeDtypeStruct((B, S, D), F32),
        scratch_shapes=[pltpu.VMEM((tm, D), F32)],
        compiler_params=_cparams(("arbitrary", "arbitrary")),
        name="out_proj_ffn",
    )(x, a, b, c, mod, gain, wo, wgu, wd)


def _block_diag(w):
    L, n, d, e = w.shape
    eye = jnp.eye(n, dtype=w.dtype)
    return jnp.einsum("lnde,nm->lndme", w, eye).reshape(L, n * d, n * e)


def _regroup_w_in(w_in):
    L, D, _ = w_in.shape
    sizes = (LRU_WIDTH, LRU_WIDTH, FOX_WIDTH, FOX_WIDTH, FOX_WIDTH, FOX_HEADS,
             GLA_KW, GLA_KW, GLA_VW, GLA_RANK, GLA_VW)
    offs = np.concatenate([[0], np.cumsum(sizes)])
    seg = lambda i: w_in[:, :, offs[i]:offs[i + 1]]
    z = lambda n: jnp.zeros((L, D, n), w_in.dtype)
    cols = [seg(0), seg(1), seg(2), seg(3), seg(4), seg(6), seg(7), seg(8), seg(10),
            seg(5), z(GLOW_LANE0 - FOX_HEADS), seg(9), z(SMALL_COLS - GLOW_LANE0 - GLA_RANK)]
    return jnp.concatenate(cols, axis=-1).astype(BF16)


def _head_ones(n_heads, dim):
    return jnp.asarray(np.kron(np.eye(n_heads), np.ones((dim, dim))), BF16)


def _fox_constants():
    H, W = FOX_HEADS, FOX_WIDTH
    rq = np.zeros((LANES, H * AUG), np.float32)
    rk = np.zeros((LANES, H * AUG), np.float32)
    for h in range(H):
        for s in range(N_SPLIT):
            rq[s * H + h, h * AUG + HEAD_DIM + s] = 1.0
            rq[ONE_LANE, h * AUG + HEAD_DIM + N_SPLIT + s] = 1.0
            rk[ONE_LANE, h * AUG + HEAD_DIM + s] = 1.0
            rk[s * H + h, h * AUG + HEAD_DIM + N_SPLIT + s] = -1.0
    return jnp.asarray(rq, BF16), jnp.asarray(rk, BF16)


def _gla_constants():
    R, C, KW = GLA_ROWS, GLA_CHUNK, GLA_KW
    i = np.arange(R)
    tri = ((i[:, None] >= i[None, :]) & (i[:, None] // C == i[None, :] // C)).astype(np.float32)
    lane = np.arange(KW)
    hmask = (lane[None, :] // GLA_DK == np.arange(GLA_HEADS)[:, None]).astype(np.float32)
    bdmask = (lane[:, None] // GLA_DK == lane[None, :] // GLA_DK).astype(np.float32)
    return jnp.asarray(tri, BF16), jnp.asarray(hmask, F32), jnp.asarray(bdmask, F32)


def _pick_tile(n, want):
    t = min(n, want)
    while n % t:
        t //= 2
    return t


def kernel(x, c, norm1_gain, norm2_gain, w_mod, b_mod, w_in, conv_w, conv_b, lru_w_r, lru_b_r,
           lru_w_i, lru_b_i, lru_lambda, fox_b_f, fox_q_gain, fox_k_gain, gla_w_alpha, gla_b_alpha,
           gla_out_gain, w_out, ffn_w_gate_up, ffn_w_down):
    B, S, D = x.shape
    L = w_in.shape[0]
    tm = _pick_tile(S, 512)
    tq = _pick_tile(S, 256)

    mod = _modulation(c, w_mod, b_mod)
    w_in_p = _regroup_w_in(w_in)
    wr_bd = _block_diag(lru_w_r).astype(BF16)
    wi_bd = _block_diag(lru_w_i).astype(BF16)
    w_out_b = w_out.astype(BF16)
    wgu_b = ffn_w_gate_up.astype(BF16)
    wd_b = ffn_w_down.astype(BF16)
    row = lambda a: a.reshape(L, 1, a.shape[-1])
    bf_pad = jnp.pad(fox_b_f, ((0, 0), (0, LANES - FOX_HEADS))).reshape(L, 1, LANES)
    qg = jnp.tile(fox_q_gain * (HEAD_DIM ** -0.5 * LOG2E), (1, FOX_HEADS)).reshape(L, 1, FOX_WIDTH)
    kg = jnp.tile(fox_k_gain, (1, FOX_HEADS)).reshape(L, 1, FOX_WIDTH)
    wa_pad = jnp.pad(gla_w_alpha, ((0, 0), (GLOW_LANE0, LANES - GLOW_LANE0 - GLA_RANK), (0, 0))).astype(BF16)
    og = jnp.tile(gla_out_gain, (1, GLA_HEADS)).reshape(L, 1, GLA_VW)
    fox_ones = _head_ones(FOX_HEADS, HEAD_DIM)
    gla_ones = _head_ones(GLA_HEADS, GLA_DK)
    rq, rk = _fox_constants()
    tri, hmask, bdmask = _gla_constants()

    for l in range(L):
        lru_p = (conv_w[l], row(conv_b)[l], wr_bd[l], row(lru_b_r)[l], wi_bd[l], row(lru_b_i)[l],
                 row(lru_lambda)[l])
        out_a, fox, gla, small = _in_proj(x, mod[l], row(norm1_gain)[l], w_in_p, lru_p, l, tm)
        kaug, qaugt, vt = _fox_prep(fox, small, bf_pad[l], qg[l], kg[l], fox_ones, rq, rk, tm, tq)
        out_b = _fox_attention(kaug, qaugt, vt, tq)
        out_c = _gla_mixer(gla, small, wa_pad[l], row(gla_b_alpha)[l], og[l], tri, hmask, bdmask,
                           gla_ones, tm)
        x = _out_ffn(x, out_a, out_b, out_c, mod[l], row(norm2_gain)[l], w_out_b, wgu_b, wd_b, l, tm)
    return x
```

```python
import functools

import numpy as np
import jax
import jax.numpy as jnp
from jax import lax
from jax.experimental import pallas as pl
from jax.experimental.pallas import tpu as pltpu

F32 = jnp.float32
BF16 = jnp.bfloat16

EPS = 1e-6
HEAD_DIM = 64
LRU_WIDTH = 384
LRU_BLOCKS = 6
CONV_WIDTH = 4
LRU_C = 8.0
FOX_HEADS = 6
FOX_WIDTH = FOX_HEADS * HEAD_DIM
GLA_HEADS = 4
GLA_DK = 64
GLA_KW = GLA_HEADS * GLA_DK
GLA_VW = GLA_KW
GLA_RANK = 16
GLA_TAU = 16.0
GLA_CHUNK = 64
N_MOD = 6

LANES = 128
SUBLANES = 8
VMEM_LIMIT_BYTES = 56 * 1024 * 1024

LRU_COLS = 2 * LRU_WIDTH
FOX_COLS = 3 * FOX_WIDTH
GLA_COLS = 3 * GLA_KW + GLA_VW
SMALL_COLS = LANES
GLOW_LANE0 = 8
NP_COLS = LRU_COLS + FOX_COLS + GLA_COLS + SMALL_COLS
MXU_COLS = 256
LRU_ROWS = 128

AUG = LANES
N_SPLIT = 3
ONE_LANE = N_SPLIT * FOX_HEADS
NEG_BIG = -1e30
V_ROWS = HEAD_DIM + 16
LOG2E = float(np.log2(np.e))


def _sigmoid(x):
    return 0.5 * jnp.tanh(0.5 * x) + 0.5


def _log_sigmoid(x):
    return jnp.minimum(x, 0.0) - jnp.log(1.0 + jnp.exp(-jnp.abs(x)))


def _softplus(x):
    return jnp.maximum(x, 0.0) + jnp.log(1.0 + jnp.exp(-jnp.abs(x)))


def _gelu_tanh(x):
    c = np.float32(np.sqrt(2.0 / np.pi))
    return 0.5 * x * (1.0 + jnp.tanh(c * (x + 0.044715 * (x * x * x))))


def _dot(a, b):
    return jnp.dot(a, b, preferred_element_type=F32)


def _cparams(sem):
    return pltpu.CompilerParams(dimension_semantics=sem, vmem_limit_bytes=VMEM_LIMIT_BYTES)


def _const_spec(shape):
    nd = len(shape)
    return pl.BlockSpec(shape, lambda *_: (0,) * nd)


def _layer_weight_spec(stacked_shape, layer):
    tail = tuple(stacked_shape[1:])
    return pl.BlockSpec((None,) + tail, lambda *_: (layer,) + (0,) * len(tail),
                        pipeline_mode=pl.Buffered(1))


def _mod_kernel(c_ref, w_ref, b_ref, o_ref):
    c = c_ref[...]
    ca = (c * _sigmoid(c)).astype(BF16)
    o_ref[0] = _dot(ca, w_ref[0].astype(BF16)) + b_ref[0]


def _modulation(c, w_mod, b_mod):
    L, D, N = w_mod.shape
    B = c.shape[0]
    bp = -(-B // SUBLANES) * SUBLANES
    cp = jnp.pad(c, ((0, bp - B), (0, 0)))
    tn = 1024
    out = pl.pallas_call(
        _mod_kernel,
        grid=(L, N // tn),
        in_specs=[
            pl.BlockSpec((bp, D), lambda l, n: (0, 0)),
            pl.BlockSpec((1, D, tn), lambda l, n: (l, 0, n)),
            pl.BlockSpec((1, 1, tn), lambda l, n: (l, 0, n)),
        ],
        out_specs=pl.BlockSpec((1, bp, tn), lambda l, n: (l, 0, n)),
        out_shape=jax.ShapeDtypeStruct((L, bp, N), F32),
        compiler_params=_cparams(("arbitrary", "arbitrary")),
        name="adaln_mod",
    )(cp, w_mod, b_mod.reshape(L, 1, N))
    return out[:, :B].reshape(L, B, N_MOD, D)


def _rms_mod(x, gain, scale, shift):
    ms = jnp.mean(x * x, axis=-1, keepdims=True)
    y = x * lax.rsqrt(ms + EPS) * gain
    return y * (1.0 + scale) + shift


def _in_proj_kernel(x_ref, mod_ref, g_ref, w_ref, cw_ref, cb_ref, wr_ref, br_ref, wi_ref, bi_ref,
                    lam_ref, oa_ref, fox_ref, gla_ref, small_ref, xs_ref, h_ref):
    tm, W = x_ref.shape[1], LRU_WIDTH

    @pl.when(pl.program_id(1) == 0)
    def _():
        xs_ref[0:SUBLANES, :] = jnp.zeros((SUBLANES, W), F32)
        h_ref[...] = jnp.zeros_like(h_ref)

    h = _rms_mod(x_ref[0], g_ref[...], mod_ref[0, 1:2, :], mod_ref[0, 0:1, :]).astype(BF16)
    segs, c0 = [], LRU_COLS
    for ref, n in ((fox_ref, FOX_COLS), (gla_ref, GLA_COLS), (small_ref, SMALL_COLS)):
        segs.append((ref, c0, c0 + n))
        c0 += n

    def project(g0):
        g1 = g0 + MXU_COLS
        y = _dot(h, w_ref[:, g0:g1])
        for ref, s0, s1 in segs:
            lo, hi = max(g0, s0), min(g1, s1)
            if lo < hi:
                ref[0, :, lo - s0:hi - s0] = y[:, lo - g0:hi - g0].astype(ref.dtype)

    y0 = _dot(h, w_ref[:, 0:LRU_COLS])
    xs_ref[SUBLANES:SUBLANES + tm, :] = y0[:, 0:W]
    pieces = [(s, r0) for s in range(W // LANES) for r0 in range(0, tm, LRU_ROWS)]
    gates = [_lru_gates(y0[r0:r0 + LRU_ROWS, s * LANES:(s + 1) * LANES], r0, slice(s * LANES, (s + 1) * LANES),
                        cw_ref, cb_ref, wr_ref, wi_ref, xs_ref) for s, r0 in pieces]
    xs_ref[0:SUBLANES, :] = xs_ref[tm:tm + SUBLANES, :]
    chunks = list(range(LRU_COLS, NP_COLS, MXU_COLS))
    carry = None
    for (s, r0), g in zip(pieces, gates):
        sl = slice(s * LANES, (s + 1) * LANES)
        rows = slice(r0, r0 + LRU_ROWS)
        if chunks:
            project(chunks.pop(0))
        if r0 == 0:
            carry = h_ref[0:1, sl]
        out, carry = _lru_finish(*g, y0[rows, W + s * LANES:W + (s + 1) * LANES], sl, carry,
                                 br_ref, bi_ref, lam_ref)
        oa_ref[0, rows, sl] = out.astype(oa_ref.dtype)
        if r0 + LRU_ROWS == tm:
            h_ref[:, sl] = jnp.broadcast_to(carry, (h_ref.shape[0], LANES))
    for g0 in chunks:
        project(g0)


def _in_proj(x, mod, gain, w, lru_p, layer, tm):
    B, S, D = x.shape
    W = LRU_WIDTH
    tok = lambda n: pl.BlockSpec((1, tm, n), lambda b, i: (b, i, 0))
    return pl.pallas_call(
        _in_proj_kernel,
        grid=(B, S // tm),
        in_specs=[
            tok(D),
            pl.BlockSpec((1, N_MOD, D), lambda b, i: (b, 0, 0)),
            _const_spec((1, D)),
            _layer_weight_spec(w.shape, layer),
            _const_spec((CONV_WIDTH, W)), _const_spec((1, W)),
            _const_spec((W, W)), _const_spec((1, W)),
            _const_spec((W, W)), _const_spec((1, W)),
            _const_spec((1, W)),
        ],
        out_specs=[tok(W), tok(FOX_COLS), tok(GLA_COLS), tok(SMALL_COLS)],
        out_shape=[
            jax.ShapeDtypeStruct((B, S, W), BF16),
            jax.ShapeDtypeStruct((B, S, FOX_COLS), BF16),
            jax.ShapeDtypeStruct((B, S, GLA_COLS), BF16),
            jax.ShapeDtypeStruct((B, S, SMALL_COLS), F32),
        ],
        scratch_shapes=[pltpu.VMEM((tm + SUBLANES, W), F32), pltpu.VMEM((SUBLANES, W), F32)],
        compiler_params=_cparams(("arbitrary", "arbitrary")),
        name="in_proj_lru",
    )(x, mod, gain, w, *lru_p)


def _scan_add(v):
    n = v.shape[0]
    row = lax.broadcasted_iota(jnp.int32, v.shape, 0)
    d = 1
    while d < n:
        v = v + jnp.where(row >= d, pltpu.roll(v, d, axis=0), 0.0)
        d *= 2
    return v


def _scan_linear(a, u, h0):
    n, w = a.shape
    g = n // SUBLANES
    a = a.reshape(g, SUBLANES, w)
    u = u.reshape(g, SUBLANES, w)
    sub = lax.broadcasted_iota(jnp.int32, a.shape, 1)
    d = 1
    while d < SUBLANES:
        m = sub >= d
        u = u + jnp.where(m, a * pltpu.roll(u, d, axis=1), 0.0)
        a = jnp.where(m, a * pltpu.roll(a, d, axis=1), a)
        d *= 2
    hs, h = [], h0
    for v in range(g):
        hv = a[v] * h + u[v]
        h = hv[SUBLANES - 1:SUBLANES, :]
        hs.append(hv)
    return jnp.concatenate(hs, axis=0), h


def _lru_gates(xa, r0, sl, cw_ref, cb_ref, wr_ref, wi_ref, xs_ref):
    n = xa.shape[0]
    xc = xa * cw_ref[CONV_WIDTH - 1:CONV_WIDTH, sl] + cb_ref[:, sl]
    for k in range(1, CONV_WIDTH):
        j = CONV_WIDTH - 1 - k
        xc = xc + xs_ref[pl.ds(SUBLANES - k + r0, n), sl] * cw_ref[j:j + 1, sl]
    xcb = xc.astype(BF16)
    return xc, _dot(xcb, wr_ref[sl, sl]), _dot(xcb, wi_ref[sl, sl])


def _lru_finish(xc, r_pre, i_pre, ya, sl, h0, br_ref, bi_ref, lam_ref):
    r = _sigmoid(r_pre + br_ref[:, sl])
    i = _sigmoid(i_pre + bi_ref[:, sl])
    a = jnp.exp2(r * ((-LRU_C * LOG2E) * _softplus(-lam_ref[:, sl])))
    mult = jnp.sqrt(1.0 - a * a)
    u = mult * (i * xc)
    h, h_last = _scan_linear(a, u, h0)
    return h * _gelu_tanh(ya), h_last


def _fox_prep_kernel(fox_ref, small_ref, bf_ref, qg_ref, kg_ref, ones_ref,
                     rq_ref, rk_ref, kaug_ref, qaugt_ref, vt_ref, cum_ref):
    tm = fox_ref.shape[1]
    W = FOX_WIDTH

    @pl.when(pl.program_id(1) == 0)
    def _():
        cum_ref[...] = jnp.zeros_like(cum_ref)

    q = fox_ref[0, :, 0:W].astype(F32)
    k = fox_ref[0, :, W:2 * W].astype(F32)
    v = fox_ref[0, :, 2 * W:3 * W].astype(F32)

    def head_norm(t, gain):
        ms = _dot((t * t).astype(BF16), ones_ref[...])
        return t * lax.rsqrt(ms + EPS) * gain

    qn = head_norm(q, qg_ref[...])
    kn = head_norm(k, kg_ref[...])

    lf = _log_sigmoid(small_ref[0] + bf_ref[...])
    cum = _scan_add(lf) + cum_ref[0:1, :]
    cum_ref[...] = jnp.broadcast_to(cum[tm - 1:tm, :], cum_ref.shape)

    cs = cum * LOG2E
    c1 = cs.astype(BF16).astype(F32)
    r1 = cs - c1
    c2 = r1.astype(BF16).astype(F32)
    c3 = (r1 - c2).astype(BF16).astype(F32)
    lane = lax.broadcasted_iota(jnp.int32, cum.shape, 1)
    H = FOX_HEADS
    packed = jnp.where(lane < H, c1, 0.0)
    packed = jnp.where((lane >= H) & (lane < 2 * H), pltpu.roll(c2, H, axis=1), packed)
    packed = jnp.where((lane >= 2 * H) & (lane < 3 * H), pltpu.roll(c3, 2 * H, axis=1), packed)
    packed = jnp.where(lane == ONE_LANE, 1.0, packed).astype(BF16)

    kbias = _dot(packed, rk_ref[...])
    qbias = _dot(packed, rq_ref[...])
    feat = lane < HEAD_DIM

    def head_aug(t, bias, h):
        src = t[:, (h // 2) * LANES:(h // 2 + 1) * LANES]
        if h % 2:
            src = pltpu.roll(src, HEAD_DIM, axis=1)
        return jnp.where(feat, src, bias[:, h * AUG:(h + 1) * AUG])

    vt = v.T
    tk = vt_ref.shape[4]
    for h in range(H):
        kaug_ref[0, h] = head_aug(kn, kbias, h).astype(BF16)
        qaugt_ref[0, h] = head_aug(qn, qbias, h).T.astype(BF16)
        for cb in range(tm // tk):
            vt_ref[0, h, cb, 0:HEAD_DIM, :] = (
                vt[h * HEAD_DIM:(h + 1) * HEAD_DIM, cb * tk:(cb + 1) * tk].astype(BF16))
            vt_ref[0, h, cb, HEAD_DIM:V_ROWS, :] = jnp.ones((V_ROWS - HEAD_DIM, tk), BF16)


def _fox_prep(fox, small, bf, qg, kg, ones_bd, rq, rk, tm, tk):
    B, S, _ = fox.shape
    H, W = FOX_HEADS, FOX_WIDTH
    return pl.pallas_call(
        _fox_prep_kernel,
        grid=(B, S // tm),
        in_specs=[
            pl.BlockSpec((1, tm, FOX_COLS), lambda b, i: (b, i, 0)),
            pl.BlockSpec((1, tm, SMALL_COLS), lambda b, i: (b, i, 0)),
            _const_spec((1, LANES)), _const_spec((1, W)), _const_spec((1, W)),
            _const_spec((W, W)),
            _const_spec((LANES, H * AUG)), _const_spec((LANES, H * AUG)),
        ],
        out_specs=[
            pl.BlockSpec((1, H, tm, AUG), lambda b, i: (b, 0, i, 0)),
            pl.BlockSpec((1, H, AUG, tm), lambda b, i: (b, 0, 0, i)),
            pl.BlockSpec((1, H, tm // tk, V_ROWS, tk), lambda b, i: (b, 0, i, 0, 0)),
        ],
        out_shape=[
            jax.ShapeDtypeStruct((B, H, S, AUG), BF16),
            jax.ShapeDtypeStruct((B, H, AUG, S), BF16),
            jax.ShapeDtypeStruct((B, H, S // tk, V_ROWS, tk), BF16),
        ],
        scratch_shapes=[pltpu.VMEM((SUBLANES, LANES), F32)],
        compiler_params=_cparams(("arbitrary", "arbitrary")),
        name="fox_prep",
    )(fox, small, bf, qg, kg, ones_bd, rq, rk)


def _fox_attn_kernel(k_ref, qt_ref, qt_next_ref, vt_ref, o_ref, acc_ref, sa_ref, sb_ref, cm_ref, *, tq):
    qi = pl.program_id(1)
    H = FOX_HEADS
    acc_ref[...] = jnp.zeros_like(acc_ref)

    def first_scores(q_ref, h):
        s = _dot(k_ref[0, h, 0:tq, :], q_ref[0, h])
        sa_ref[h] = s
        cm_ref[h] = jnp.max(s, axis=0, keepdims=True)

    def update(h, j, s, m_old, m_new):
        p = jnp.exp2(s - m_new).astype(BF16)
        acc_ref[h] = jnp.exp2(m_old - m_new) * acc_ref[h] + _dot(vt_ref[0, h, j], p)

    def masked_step(j, cur_ref, mask, ms, nxt_ref=None, j_nxt=None, renew=False):
        new_ms = []
        for h in range(H):
            if nxt_ref is not None:
                start = pl.multiple_of(j_nxt * tq, tq)
                nxt_ref[h] = _dot(k_ref[0, h, pl.ds(start, tq), :], qt_ref[0, h])
            s = jnp.where(mask, cur_ref[h], NEG_BIG)
            if renew:
                first_scores(qt_next_ref, h)
            m_new = jnp.maximum(ms[h], jnp.max(s, axis=0, keepdims=True))
            new_ms.append(m_new)
            update(h, j, s, ms[h], m_new)
        return tuple(new_ms)

    def step(j, cur_ref, c_cur, nxt_ref, ms):
        start = pl.multiple_of((j + 1) * tq, tq)
        new_ms, c_nxt = [], []
        for h in range(H):
            s_n = _dot(k_ref[0, h, pl.ds(start, tq), :], qt_ref[0, h])
            nxt_ref[h] = s_n
            c_nxt.append(jnp.max(s_n, axis=0, keepdims=True))
            m_new = jnp.maximum(ms[h], c_cur[h])
            new_ms.append(m_new)
            update(h, j, cur_ref[h], ms[h], m_new)
        return tuple(new_ms), tuple(c_nxt)

    def pair(t, carry):
        ms, ca = carry
        j0 = 2 * t
        ms, cb = step(j0, sa_ref, ca, sb_ref, ms)
        ms, ca = step(j0 + 1, sb_ref, cb, sa_ref, ms)
        return ms, ca

    @pl.when(qi == 0)
    def _():
        for h in range(H):
            first_scores(qt_ref, h)

    carry = (tuple(jnp.full((1, tq), NEG_BIG, F32) for _ in range(H)),
             tuple(cm_ref[h] for h in range(H)))
    n_pairs = lax.div(qi, 2)
    n_quads = lax.div(n_pairs, 2)
    carry = lax.fori_loop(0, n_quads, lambda u, c: pair(2 * u + 1, pair(2 * u, c)), carry)
    carry = lax.fori_loop(2 * n_quads, n_pairs, pair, carry)
    rel = (lax.broadcasted_iota(jnp.int32, (tq, tq), 0)
           - lax.broadcasted_iota(jnp.int32, (tq, tq), 1))
    j0 = 2 * n_pairs
    ms = masked_step(j0, sa_ref, rel <= (qi - j0) * tq, carry[0], sb_ref, qi, renew=True)

    @pl.when(qi > j0)
    def _():
        masked_step(qi, sb_ref, rel <= 0, ms)

    out = jnp.concatenate(
        [acc_ref[h, 0:HEAD_DIM, :] * (1.0 / acc_ref[h, HEAD_DIM:HEAD_DIM + 1, :]) for h in range(H)], axis=0)
    o_ref[0] = out.T.astype(o_ref.dtype)


def _fox_attention(kaug, qaugt, vt, tq):
    B, H, S, _ = kaug.shape
    last_q = S // tq - 1
    return pl.pallas_call(
        functools.partial(_fox_attn_kernel, tq=tq),
        grid=(B, S // tq),
        in_specs=[
            pl.BlockSpec((1, H, S, AUG), lambda b, i: (b, 0, 0, 0), pipeline_mode=pl.Buffered(1)),
            pl.BlockSpec((1, H, AUG, tq), lambda b, i: (b, 0, 0, i)),
            pl.BlockSpec((1, H, AUG, tq), lambda b, i: (b, 0, 0, jnp.minimum(i + 1, last_q))),
            pl.BlockSpec((1, H, S // tq, V_ROWS, tq), lambda b, i: (b, 0, 0, 0, 0),
                         pipeline_mode=pl.Buffered(1)),
        ],
        out_specs=pl.BlockSpec((1, tq, H * HEAD_DIM), lambda b, i: (b, i, 0)),
        out_shape=jax.ShapeDtypeStruct((B, S, H * HEAD_DIM), BF16),
        scratch_shapes=[pltpu.VMEM((H, V_ROWS, tq), F32), pltpu.VMEM((H, tq, tq), F32),
                        pltpu.VMEM((H, tq, tq), F32), pltpu.VMEM((H, 1, tq), F32)],
        compiler_params=_cparams(("arbitrary", "arbitrary")),
        name="fox_attention",
    )(kaug, qaugt, qaugt, vt)


GLA_ROWS = 256


def _gla_kernel(gla_ref, small_ref, wa_ref, ba_ref, gain_ref, tri_ref, hmask_ref, bdmask_ref,
                ones_ref, o_ref, st_ref):
    ts = o_ref.shape[1]
    KW, C, R = GLA_KW, GLA_CHUNK, GLA_ROWS

    @pl.when(pl.program_id(1) == 0)
    def _():
        st_ref[...] = jnp.zeros_like(st_ref)

    q = gla_ref[0, :, 0:KW].astype(F32) * (GLA_DK ** -0.5)
    k = gla_ref[0, :, KW:2 * KW].astype(F32)
    v = gla_ref[0, :, 2 * KW:3 * KW]
    low = small_ref[0].astype(BF16)
    la = _log_sigmoid(_dot(low, wa_ref[...]) + ba_ref[...]) * (LOG2E / GLA_TAU)
    la_hi = la.astype(BF16)
    la_lo = (la - la_hi.astype(F32)).astype(BF16)
    tri = tri_ref[...]
    bcum = jnp.concatenate(
        [_dot(tri, la_hi[g0:g0 + R]) + _dot(tri, la_lo[g0:g0 + R]) for g0 in range(0, ts, R)], axis=0)
    ends = [bcum[c0 + C - 1:c0 + C, :] for c0 in range(0, ts, C)]
    b_last = jnp.concatenate([jnp.broadcast_to(e, (C, KW)) for e in ends], axis=0)
    q_dec = (q * jnp.exp2(bcum)).astype(BF16)
    k_dec = (k * jnp.exp2(-bcum)).astype(BF16)
    k_end = (k * jnp.exp2(b_last - bcum)).astype(BF16)

    tri_f = tri.astype(F32)
    o_groups = []
    for g0 in range(0, ts, R):
        rows = slice(g0, g0 + R)
        o_acc = jnp.zeros((R, KW), F32)
        for h in range(GLA_HEADS):
            hm = hmask_ref[h:h + 1, :].astype(BF16)
            att = lax.dot_general(q_dec[rows] * hm, k_dec[rows], (((1,), (1,)), ((), ())),
                                  preferred_element_type=F32)
            o_acc = o_acc + _dot((att * tri_f).astype(BF16), v[rows] * hm)
        o_groups.append(o_acc)

    o_parts = []
    for ci, c0 in enumerate(range(0, ts, C)):
        cs = slice(c0, c0 + C)
        st = st_ref[...]
        o_parts.append(lax.dot_general(q_dec[cs], st.astype(BF16), (((1,), (1,)), ((), ())),
                                       preferred_element_type=F32))
        kv_t = lax.dot_general(v[cs], k_end[cs], (((0,), (0,)), ((), ())),
                               preferred_element_type=F32)
        st_ref[...] = st * jnp.exp2(ends[ci]) + kv_t * bdmask_ref[...]
    o = jnp.concatenate(o_groups, axis=0) + jnp.concatenate(o_parts, axis=0)

    ms = _dot((o * o).astype(BF16), ones_ref[...])
    o = o * lax.rsqrt(ms + EPS) * gain_ref[...]
    g = gla_ref[0, :, 3 * KW:4 * KW].astype(F32)
    o_ref[0] = (o * (g * _sigmoid(g))).astype(o_ref.dtype)


def _gla_mixer(gla, small, wa, ba, gain, tri, hmask, bdmask, ones_bd, ts):
    B, S, _ = gla.shape
    KW = GLA_KW
    return pl.pallas_call(
        _gla_kernel,
        grid=(B, S // ts),
        in_specs=[
            pl.BlockSpec((1, ts, GLA_COLS), lambda b, i: (b, i, 0)),
            pl.BlockSpec((1, ts, SMALL_COLS), lambda b, i: (b, i, 0)),
            _const_spec((LANES, KW)), _const_spec((1, KW)), _const_spec((1, KW)),
            _const_spec((GLA_ROWS, GLA_ROWS)), _const_spec((GLA_HEADS, KW)),
            _const_spec((KW, KW)), _const_spec((KW, KW)),
        ],
        out_specs=pl.BlockSpec((1, ts, KW), lambda b, i: (b, i, 0)),
        out_shape=jax.ShapeDtypeStruct((B, S, KW), BF16),
        scratch_shapes=[pltpu.VMEM((KW, KW), F32)],
        compiler_params=_cparams(("arbitrary", "arbitrary")),
        name="gla_mixer",
    )(gla, small, wa, ba, gain, tri, hmask, bdmask, ones_bd)


def _ffn_chunks(d_ff):
    chunks, c0 = [], 0
    while c0 < d_ff:
        n = min(512, d_ff - c0)
        chunks.append((c0, n))
        c0 += n
    return chunks


def _out_ffn_kernel(x_ref, a_ref, b_ref, c_ref, mod_ref, g_ref, wo_ref, wgu_ref, wd_ref, o_ref,
                    acc_ref):
    d_ff = wd_ref.shape[0]
    mix = _dot(jnp.concatenate([a_ref[0], b_ref[0], c_ref[0]], axis=1), wo_ref[...])
    x1 = x_ref[0] + mod_ref[0, 2:3, :] * mix
    h = _rms_mod(x1, g_ref[...], mod_ref[0, 4:5, :], mod_ref[0, 3:4, :]).astype(BF16)
    for idx, (c0, n) in enumerate(_ffn_chunks(d_ff)):
        gt = _dot(h, wgu_ref[:, c0:c0 + n])
        up = _dot(h, wgu_ref[:, d_ff + c0:d_ff + c0 + n])
        act = (gt * _sigmoid(gt) * up).astype(BF16)
        y = _dot(act, wd_ref[c0:c0 + n, :])
        if idx == 0:
            acc_ref[...] = y
        else:
            acc_ref[...] += y
    o_ref[0] = x1 + mod_ref[0, 5:6, :] * acc_ref[...]


def _out_ffn(x, a, b, c, mod, gain, wo, wgu, wd, layer, tm):
    B, S, D = x.shape
    tok = lambda n: pl.BlockSpec((1, tm, n), lambda bb, i: (bb, i, 0))
    return pl.pallas_call(
        _out_ffn_kernel,
        grid=(B, S // tm),
        in_specs=[
            tok(D), tok(a.shape[2]), tok(b.shape[2]), tok(c.shape[2]),
            pl.BlockSpec((1, N_MOD, D), lambda bb, i: (bb, 0, 0)),
            _const_spec((1, D)),
            _layer_weight_spec(wo.shape, layer), _layer_weight_spec(wgu.shape, layer),
            _layer_weight_spec(wd.shape, layer),
        ],
        out_specs=tok(D),
        out_shape=jax.ShapeDtypeStruct((B, S, D), F32),
        scratch_shapes=[pltpu.VMEM((tm, D), F32)],
        compiler_params=_cparams(("arbitrary", "arbitrary")),
        name="out_proj_ffn",
    )(x, a, b, c, mod, gain, wo, wgu, wd)


def _block_diag(w):
    L, n, d, e = w.shape
    eye = jnp.eye(n, dtype=w.dtype)
    return jnp.einsum("lnde,nm->lndme", w, eye).reshape(L, n * d, n * e)


def _regroup_w_in(w_in):
    L, D, _ = w_in.shape
    sizes = (LRU_WIDTH, LRU_WIDTH, FOX_WIDTH, FOX_WIDTH, FOX_WIDTH, FOX_HEADS,
             GLA_KW, GLA_KW, GLA_VW, GLA_RANK, GLA_VW)
    offs = np.concatenate([[0], np.cumsum(sizes)])
    seg = lambda i: w_in[:, :, offs[i]:offs[i + 1]]
    z = lambda n: jnp.zeros((L, D, n), w_in.dtype)
    cols = [seg(0), seg(1), seg(2), seg(3), seg(4), seg(6), seg(7), seg(8), seg(10),
            seg(5), z(GLOW_LANE0 - FOX_HEADS), seg(9), z(SMALL_COLS - GLOW_LANE0 - GLA_RANK)]
    return jnp.concatenate(cols, axis=-1).astype(BF16)


def _head_ones(n_heads, dim):
    assert dim & (dim - 1) == 0
    return jnp.asarray(np.kron(np.eye(n_heads), np.full((dim, dim), 1.0 / dim)), BF16)


def _fox_constants():
    H, W = FOX_HEADS, FOX_WIDTH
    rq = np.zeros((LANES, H * AUG), np.float32)
    rk = np.zeros((LANES, H * AUG), np.float32)
    for h in range(H):
        for s in range(N_SPLIT):
            rq[s * H + h, h * AUG + HEAD_DIM + s] = 1.0
            rq[ONE_LANE, h * AUG + HEAD_DIM + N_SPLIT + s] = 1.0
            rk[ONE_LANE, h * AUG + HEAD_DIM + s] = 1.0
            rk[s * H + h, h * AUG + HEAD_DIM + N_SPLIT + s] = -1.0
    return jnp.asarray(rq, BF16), jnp.asarray(rk, BF16)


def _gla_constants():
    R, C, KW = GLA_ROWS, GLA_CHUNK, GLA_KW
    i = np.arange(R)
    tri = ((i[:, None] >= i[None, :]) & (i[:, None] // C == i[None, :] // C)).astype(np.float32)
    lane = np.arange(KW)
    hmask = (lane[None, :] // GLA_DK == np.arange(GLA_HEADS)[:, None]).astype(np.float32)
    bdmask = (lane[:, None] // GLA_DK == lane[None, :] // GLA_DK).astype(np.float32)
    return jnp.asarray(tri, BF16), jnp.asarray(hmask, F32), jnp.asarray(bdmask, F32)


def _pick_tile(n, want):
    t = min(n, want)
    while n % t:
        t //= 2
    return t


def kernel(x, c, norm1_gain, norm2_gain, w_mod, b_mod, w_in, conv_w, conv_b, lru_w_r, lru_b_r,
           lru_w_i, lru_b_i, lru_lambda, fox_b_f, fox_q_gain, fox_k_gain, gla_w_alpha, gla_b_alpha,
           gla_out_gain, w_out, ffn_w_gate_up, ffn_w_down):
    B, S, D = x.shape
    L = w_in.shape[0]
    tm = _pick_tile(S, 512)
    tq = _pick_tile(S, 256)

    mod = _modulation(c, w_mod, b_mod)
    w_in_p = _regroup_w_in(w_in)
    wr_bd = _block_diag(lru_w_r).astype(BF16)
    wi_bd = _block_diag(lru_w_i).astype(BF16)
    w_out_b = w_out.astype(BF16)
    wgu_b = ffn_w_gate_up.astype(BF16)
    wd_b = ffn_w_down.astype(BF16)
    row = lambda a: a.reshape(L, 1, a.shape[-1])
    bf_pad = jnp.pad(fox_b_f, ((0, 0), (0, LANES - FOX_HEADS))).reshape(L, 1, LANES)
    qg = jnp.tile(fox_q_gain * (HEAD_DIM ** -0.5 * LOG2E), (1, FOX_HEADS)).reshape(L, 1, FOX_WIDTH)
    kg = jnp.tile(fox_k_gain, (1, FOX_HEADS)).reshape(L, 1, FOX_WIDTH)
    wa_pad = jnp.pad(gla_w_alpha, ((0, 0), (GLOW_LANE0, LANES - GLOW_LANE0 - GLA_RANK), (0, 0))).astype(BF16)
    og = jnp.tile(gla_out_gain, (1, GLA_HEADS)).reshape(L, 1, GLA_VW)
    fox_ones = _head_ones(FOX_HEADS, HEAD_DIM)
    gla_ones = _head_ones(GLA_HEADS, GLA_DK)
    rq, rk = _fox_constants()
    tri, hmask, bdmask = _gla_constants()

    for l in range(L):
        lru_p = (conv_w[l], row(conv_b)[l], wr_bd[l], row(lru_b_r)[l], wi_bd[l], row(lru_b_i)[l],
                 row(lru_lambda)[l])
        out_a, fox, gla, small = _in_proj(x, mod[l], row(norm1_gain)[l], w_in_p, lru_p, l, tm)
        kaug, qaugt, vt = _fox_prep(fox, small, bf_pad[l], qg[l], kg[l], fox_ones, rq, rk, tm, tq)
        out_b = _fox_attention(kaug, qaugt, vt, tq)
        out_c = _gla_mixer(gla, small, wa_pad[l], row(gla_b_alpha)[l], og[l], tri, hmask, bdmask,
                           gla_ones, tm)
        x = _out_ffn(x, out_a, out_b, out_c, mod[l], row(norm2_gain)[l], w_out_b, wgu_b, wd_b, l, tm)
    return x
```

```python
import functools

import numpy as np
import jax
import jax.numpy as jnp
from jax import lax
from jax.experimental import pallas as pl
from jax.experimental.pallas import tpu as pltpu

F32 = jnp.float32
BF16 = jnp.bfloat16

EPS = 1e-6
HEAD_DIM = 64
LRU_WIDTH = 384
LRU_BLOCKS = 6
CONV_WIDTH = 4
LRU_C = 8.0
FOX_HEADS = 6
FOX_WIDTH = FOX_HEADS * HEAD_DIM
GLA_HEADS = 4
GLA_DK = 64
GLA_KW = GLA_HEADS * GLA_DK
GLA_VW = GLA_KW
GLA_RANK = 16
GLA_TAU = 16.0
GLA_CHUNK = 64
N_MOD = 6

LANES = 128
SUBLANES = 8
VMEM_LIMIT_BYTES = 56 * 1024 * 1024

LRU_COLS = 2 * LRU_WIDTH
FOX_COLS = 3 * FOX_WIDTH
GLA_COLS = 3 * GLA_KW + GLA_VW
SMALL_COLS = LANES
GLOW_LANE0 = 8
NP_COLS = LRU_COLS + FOX_COLS + GLA_COLS + SMALL_COLS
MXU_COLS = 256
LRU_ROWS = 128

AUG = LANES
N_SPLIT = 3
ONE_LANE = N_SPLIT * FOX_HEADS
NEG_BIG = -1e30
V_ROWS = HEAD_DIM + 16
LOG2E = float(np.log2(np.e))


def _sigmoid(x):
    return 0.5 * jnp.tanh(0.5 * x) + 0.5


def _log_sigmoid(x):
    return jnp.minimum(x, 0.0) - jnp.log(1.0 + jnp.exp(-jnp.abs(x)))


def _softplus(x):
    return jnp.maximum(x, 0.0) + jnp.log(1.0 + jnp.exp(-jnp.abs(x)))


def _gelu_tanh(x):
    c = np.float32(np.sqrt(2.0 / np.pi))
    return 0.5 * x * (1.0 + jnp.tanh(c * (x + 0.044715 * (x * x * x))))


def _dot(a, b):
    return jnp.dot(a, b, preferred_element_type=F32)


def _cparams(sem):
    return pltpu.CompilerParams(dimension_semantics=sem, vmem_limit_bytes=VMEM_LIMIT_BYTES)


def _const_spec(shape):
    nd = len(shape)
    return pl.BlockSpec(shape, lambda *_: (0,) * nd)


def _layer_weight_spec(stacked_shape, layer):
    tail = tuple(stacked_shape[1:])
    return pl.BlockSpec((None,) + tail, lambda *_: (layer,) + (0,) * len(tail),
                        pipeline_mode=pl.Buffered(1))


def _mod_kernel(c_ref, w_ref, b_ref, o_ref):
    c = c_ref[...]
    ca = (c * _sigmoid(c)).astype(BF16)
    o_ref[0] = _dot(ca, w_ref[0].astype(BF16)) + b_ref[0]


def _modulation(c, w_mod, b_mod):
    L, D, N = w_mod.shape
    B = c.shape[0]
    bp = -(-B // SUBLANES) * SUBLANES
    cp = jnp.pad(c, ((0, bp - B), (0, 0)))
    tn = 1024
    out = pl.pallas_call(
        _mod_kernel,
        grid=(L, N // tn),
        in_specs=[
            pl.BlockSpec((bp, D), lambda l, n: (0, 0)),
            pl.BlockSpec((1, D, tn), lambda l, n: (l, 0, n)),
            pl.BlockSpec((1, 1, tn), lambda l, n: (l, 0, n)),
        ],
        out_specs=pl.BlockSpec((1, bp, tn), lambda l, n: (l, 0, n)),
        out_shape=jax.ShapeDtypeStruct((L, bp, N), F32),
        compiler_params=_cparams(("arbitrary", "arbitrary")),
        name="adaln_mod",
    )(cp, w_mod, b_mod.reshape(L, 1, N))
    return out[:, :B].reshape(L, B, N_MOD, D)


def _rms_mod(x, gain, scale, shift):
    ms = jnp.mean(x * x, axis=-1, keepdims=True)
    y = x * lax.rsqrt(ms + EPS) * gain
    return y * (1.0 + scale) + shift


def _in_proj_kernel(x_ref, mod_ref, g_ref, w_ref, cw_ref, cb_ref, wr_ref, br_ref, wi_ref, bi_ref,
                    lam_ref, oa_ref, fox_ref, gla_ref, small_ref, xs_ref, h_ref):
    tm, W = x_ref.shape[1], LRU_WIDTH

    @pl.when(pl.program_id(1) == 0)
    def _():
        xs_ref[0:SUBLANES, :] = jnp.zeros((SUBLANES, W), F32)
        h_ref[...] = jnp.zeros_like(h_ref)

    h = _rms_mod(x_ref[0], g_ref[...], mod_ref[0, 1:2, :], mod_ref[0, 0:1, :]).astype(BF16)
    segs, c0 = [], LRU_COLS
    for ref, n in ((fox_ref, FOX_COLS), (gla_ref, GLA_COLS), (small_ref, SMALL_COLS)):
        segs.append((ref, c0, c0 + n))
        c0 += n

    def project(g0):
        g1 = g0 + MXU_COLS
        y = _dot(h, w_ref[:, g0:g1])
        for ref, s0, s1 in segs:
            lo, hi = max(g0, s0), min(g1, s1)
            if lo < hi:
                ref[0, :, lo - s0:hi - s0] = y[:, lo - g0:hi - g0].astype(ref.dtype)

    y0 = _dot(h, w_ref[:, 0:LRU_COLS])
    xs_ref[SUBLANES:SUBLANES + tm, :] = y0[:, 0:W]
    pieces = [(s, r0) for s in range(W // LANES) for r0 in range(0, tm, LRU_ROWS)]
    gates = [_lru_gates(y0[r0:r0 + LRU_ROWS, s * LANES:(s + 1) * LANES], r0, slice(s * LANES, (s + 1) * LANES),
                        cw_ref, cb_ref, wr_ref, wi_ref, xs_ref) for s, r0 in pieces]
    xs_ref[0:SUBLANES, :] = xs_ref[tm:tm + SUBLANES, :]
    chunks = list(range(LRU_COLS, NP_COLS, MXU_COLS))
    carry = None
    for (s, r0), g in zip(pieces, gates):
        sl = slice(s * LANES, (s + 1) * LANES)
        rows = slice(r0, r0 + LRU_ROWS)
        if chunks:
            project(chunks.pop(0))
        if r0 == 0:
            carry = h_ref[0:1, sl]
        out, carry = _lru_finish(*g, y0[rows, W + s * LANES:W + (s + 1) * LANES], sl, carry,
                                 br_ref, bi_ref, lam_ref)
        oa_ref[0, rows, sl] = out.astype(oa_ref.dtype)
        if r0 + LRU_ROWS == tm:
            h_ref[:, sl] = jnp.broadcast_to(carry, (h_ref.shape[0], LANES))
    for g0 in chunks:
        project(g0)


def _in_proj(x, mod, gain, w, lru_p, layer, tm):
    B, S, D = x.shape
    W = LRU_WIDTH
    tok = lambda n: pl.BlockSpec((1, tm, n), lambda b, i: (b, i, 0))
    return pl.pallas_call(
        _in_proj_kernel,
        grid=(B, S // tm),
        in_specs=[
            tok(D),
            pl.BlockSpec((1, N_MOD, D), lambda b, i: (b, 0, 0)),
            _const_spec((1, D)),
            _layer_weight_spec(w.shape, layer),
            _const_spec((CONV_WIDTH, W)), _const_spec((1, W)),
            _const_spec((W, W)), _const_spec((1, W)),
            _const_spec((W, W)), _const_spec((1, W)),
            _const_spec((1, W)),
        ],
        out_specs=[tok(W), tok(FOX_COLS), tok(GLA_COLS), tok(SMALL_COLS)],
        out_shape=[
            jax.ShapeDtypeStruct((B, S, W), BF16),
            jax.ShapeDtypeStruct((B, S, FOX_COLS), BF16),
            jax.ShapeDtypeStruct((B, S, GLA_COLS), BF16),
            jax.ShapeDtypeStruct((B, S, SMALL_COLS), F32),
        ],
        scratch_shapes=[pltpu.VMEM((tm + SUBLANES, W), F32), pltpu.VMEM((SUBLANES, W), F32)],
        compiler_params=_cparams(("arbitrary", "arbitrary")),
        name="in_proj_lru",
    )(x, mod, gain, w, *lru_p)


def _scan_add(v):
    n = v.shape[0]
    row = lax.broadcasted_iota(jnp.int32, v.shape, 0)
    d = 1
    while d < n:
        v = v + jnp.where(row >= d, pltpu.roll(v, d, axis=0), 0.0)
        d *= 2
    return v


def _scan_linear(a, u, h0):
    n, w = a.shape
    g = n // SUBLANES
    a = a.reshape(g, SUBLANES, w)
    u = u.reshape(g, SUBLANES, w)
    sub = lax.broadcasted_iota(jnp.int32, a.shape, 1)
    d = 1
    while d < SUBLANES:
        m = sub >= d
        u = u + jnp.where(m, a * pltpu.roll(u, d, axis=1), 0.0)
        a = jnp.where(m, a * pltpu.roll(a, d, axis=1), a)
        d *= 2
    hs, h = [], h0
    for v in range(g):
        hv = a[v] * h + u[v]
        h = hv[SUBLANES - 1:SUBLANES, :]
        hs.append(hv)
    return jnp.concatenate(hs, axis=0), h


def _lru_gates(xa, r0, sl, cw_ref, cb_ref, wr_ref, wi_ref, xs_ref):
    n = xa.shape[0]
    xc = xa * cw_ref[CONV_WIDTH - 1:CONV_WIDTH, sl] + cb_ref[:, sl]
    for k in range(1, CONV_WIDTH):
        j = CONV_WIDTH - 1 - k
        xc = xc + xs_ref[pl.ds(SUBLANES - k + r0, n), sl] * cw_ref[j:j + 1, sl]
    xcb = xc.astype(BF16)
    return xc, _dot(xcb, wr_ref[sl, sl]), _dot(xcb, wi_ref[sl, sl])


def _lru_finish(xc, r_pre, i_pre, ya, sl, h0, br_ref, bi_ref, lam_ref):
    r = _sigmoid(r_pre + br_ref[:, sl])
    i = _sigmoid(i_pre + bi_ref[:, sl])
    a = jnp.exp2(r * ((-LRU_C * LOG2E) * _softplus(-lam_ref[:, sl])))
    mult = jnp.sqrt(1.0 - a * a)
    u = mult * (i * xc)
    h, h_last = _scan_linear(a, u, h0)
    return h * _gelu_tanh(ya), h_last


def _fox_prep_body(fox_ref, small_ref, bf_ref, qg_ref, kg_ref, ones_ref,
                   rq_ref, rk_ref, kaug_ref, qaugt_ref, vt_ref, cum_ref):
    tm = fox_ref.shape[1]
    W = FOX_WIDTH
    q = fox_ref[0, :, 0:W].astype(F32)
    k = fox_ref[0, :, W:2 * W].astype(F32)
    v = fox_ref[0, :, 2 * W:3 * W].astype(F32)

    def head_norm(t, gain):
        ms = _dot((t * t).astype(BF16), ones_ref[...])
        return t * lax.rsqrt(ms + EPS) * gain

    qn = head_norm(q, qg_ref[...])
    kn = head_norm(k, kg_ref[...])

    lf = _log_sigmoid(small_ref[0] + bf_ref[...])
    cum = _scan_add(lf) + cum_ref[0:1, :]
    cum_ref[...] = jnp.broadcast_to(cum[tm - 1:tm, :], cum_ref.shape)

    cs = cum * LOG2E
    c1 = cs.astype(BF16).astype(F32)
    r1 = cs - c1
    c2 = r1.astype(BF16).astype(F32)
    c3 = (r1 - c2).astype(BF16).astype(F32)
    lane = lax.broadcasted_iota(jnp.int32, cum.shape, 1)
    H = FOX_HEADS
    packed = jnp.where(lane < H, c1, 0.0)
    packed = jnp.where((lane >= H) & (lane < 2 * H), pltpu.roll(c2, H, axis=1), packed)
    packed = jnp.where((lane >= 2 * H) & (lane < 3 * H), pltpu.roll(c3, 2 * H, axis=1), packed)
    packed = jnp.where(lane == ONE_LANE, 1.0, packed).astype(BF16)

    kbias = _dot(packed, rk_ref[...])
    qbias = _dot(packed, rq_ref[...])
    feat = lane < HEAD_DIM

    def head_aug(t, bias, h):
        src = t[:, (h // 2) * LANES:(h // 2 + 1) * LANES]
        if h % 2:
            src = pltpu.roll(src, HEAD_DIM, axis=1)
        return jnp.where(feat, src, bias[:, h * AUG:(h + 1) * AUG])

    vt = v.T
    tk = vt_ref.shape[4]
    for h in range(H):
        kaug_ref[0, h] = head_aug(kn, kbias, h).astype(BF16)
        qaugt_ref[0, h] = head_aug(qn, qbias, h).T.astype(BF16)
        for cb in range(tm // tk):
            vt_ref[0, h, cb, 0:HEAD_DIM, :] = (
                vt[h * HEAD_DIM:(h + 1) * HEAD_DIM, cb * tk:(cb + 1) * tk].astype(BF16))
            vt_ref[0, h, cb, HEAD_DIM:V_ROWS, :] = jnp.ones((V_ROWS - HEAD_DIM, tk), BF16)


def _fox_attn_kernel(k_ref, qt_ref, vt_ref, o_ref, acc_ref, sa_ref, sb_ref, *, tq):
    qi = pl.program_id(1)
    H = FOX_HEADS
    acc_ref[...] = jnp.zeros_like(acc_ref)

    def scores(j, s_ref):
        start = pl.multiple_of(j * tq, tq)
        cmax = []
        for h in range(H):
            s = _dot(k_ref[0, h, pl.ds(start, tq), :], qt_ref[0, h])
            s_ref[h] = s
            cmax.append(jnp.max(s, axis=0, keepdims=True))
        return tuple(cmax)

    def update(h, j, s, m_old, m_new):
        p = jnp.exp2(s - m_new).astype(BF16)
        acc_ref[h] = jnp.exp2(m_old - m_new) * acc_ref[h] + _dot(vt_ref[0, h, j], p)

    def masked_step(j, cur_ref, mask, ms, nxt_ref=None, j_nxt=None):
        new_ms = []
        for h in range(H):
            if nxt_ref is not None:
                start = pl.multiple_of(j_nxt * tq, tq)
                nxt_ref[h] = _dot(k_ref[0, h, pl.ds(start, tq), :], qt_ref[0, h])
            s = jnp.where(mask, cur_ref[h], NEG_BIG)
            m_new = jnp.maximum(ms[h], jnp.max(s, axis=0, keepdims=True))
            new_ms.append(m_new)
            update(h, j, s, ms[h], m_new)
        return tuple(new_ms)

    def step(j, cur_ref, c_cur, nxt_ref, ms):
        start = pl.multiple_of((j + 1) * tq, tq)
        new_ms, c_nxt = [], []
        for h in range(H):
            s_n = _dot(k_ref[0, h, pl.ds(start, tq), :], qt_ref[0, h])
            nxt_ref[h] = s_n
            c_nxt.append(jnp.max(s_n, axis=0, keepdims=True))
            m_new = jnp.maximum(ms[h], c_cur[h])
            new_ms.append(m_new)
            update(h, j, cur_ref[h], ms[h], m_new)
        return tuple(new_ms), tuple(c_nxt)

    def pair(t, carry):
        ms, ca = carry
        j0 = 2 * t
        ms, cb = step(j0, sa_ref, ca, sb_ref, ms)
        ms, ca = step(j0 + 1, sb_ref, cb, sa_ref, ms)
        return ms, ca

    carry = (tuple(jnp.full((1, tq), NEG_BIG, F32) for _ in range(H)), scores(0, sa_ref))
    n_pairs = lax.div(qi, 2)
    n_quads = lax.div(n_pairs, 2)
    carry = lax.fori_loop(0, n_quads, lambda u, c: pair(2 * u + 1, pair(2 * u, c)), carry)
    carry = lax.fori_loop(2 * n_quads, n_pairs, pair, carry)
    rel = (lax.broadcasted_iota(jnp.int32, (tq, tq), 0)
           - lax.broadcasted_iota(jnp.int32, (tq, tq), 1))
    j0 = 2 * n_pairs
    ms = masked_step(j0, sa_ref, rel <= (qi - j0) * tq, carry[0], sb_ref, qi)

    @pl.when(qi > j0)
    def _():
        masked_step(qi, sb_ref, rel <= 0, ms)

    out = jnp.concatenate(
        [acc_ref[h, 0:HEAD_DIM, :] * (1.0 / acc_ref[h, HEAD_DIM:HEAD_DIM + 1, :]) for h in range(H)], axis=0)
    o_ref[0] = out.T.astype(o_ref.dtype)


def _fox_attention(kaug, qaugt, vt, tq):
    B, H, S, _ = kaug.shape
    return pl.pallas_call(
        functools.partial(_fox_attn_kernel, tq=tq),
        grid=(B, S // tq),
        in_specs=[
            pl.BlockSpec((1, H, S, AUG), lambda b, i: (b, 0, 0, 0), pipeline_mode=pl.Buffered(1)),
            pl.BlockSpec((1, H, AUG, tq), lambda b, i: (b, 0, 0, i)),
            pl.BlockSpec((1, H, S // tq, V_ROWS, tq), lambda b, i: (b, 0, 0, 0, 0),
                         pipeline_mode=pl.Buffered(1)),
        ],
        out_specs=pl.BlockSpec((1, tq, H * HEAD_DIM), lambda b, i: (b, i, 0)),
        out_shape=jax.ShapeDtypeStruct((B, S, H * HEAD_DIM), BF16),
        scratch_shapes=[pltpu.VMEM((H, V_ROWS, tq), F32), pltpu.VMEM((H, tq, tq), F32),
                        pltpu.VMEM((H, tq, tq), F32)],
        compiler_params=_cparams(("arbitrary", "arbitrary")),
        name="fox_attention",
    )(kaug, qaugt, vt)


GLA_ROWS = 256


def _gla_body(gla_ref, small_ref, wa_ref, ba_ref, gain_ref, tri_ref, hmask_ref, bdmask_ref,
              ones_ref, o_ref, st_ref):
    ts = o_ref.shape[1]
    KW, C, R = GLA_KW, GLA_CHUNK, GLA_ROWS
    q = gla_ref[0, :, 0:KW].astype(F32) * (GLA_DK ** -0.5)
    k = gla_ref[0, :, KW:2 * KW].astype(F32)
    v = gla_ref[0, :, 2 * KW:3 * KW]
    low = small_ref[0].astype(BF16)
    la = _log_sigmoid(_dot(low, wa_ref[...]) + ba_ref[...]) * (LOG2E / GLA_TAU)
    la_hi = la.astype(BF16)
    la_lo = (la - la_hi.astype(F32)).astype(BF16)
    tri = tri_ref[...]
    bcum = jnp.concatenate(
        [_dot(tri, la_hi[g0:g0 + R]) + _dot(tri, la_lo[g0:g0 + R]) for g0 in range(0, ts, R)], axis=0)
    ends = [bcum[c0 + C - 1:c0 + C, :] for c0 in range(0, ts, C)]
    b_last = jnp.concatenate([jnp.broadcast_to(e, (C, KW)) for e in ends], axis=0)
    q_dec = (q * jnp.exp2(bcum)).astype(BF16)
    k_dec = (k * jnp.exp2(-bcum)).astype(BF16)
    k_end = (k * jnp.exp2(b_last - bcum)).astype(BF16)

    tri_f = tri.astype(F32)
    o_groups = []
    for g0 in range(0, ts, R):
        rows = slice(g0, g0 + R)
        o_acc = jnp.zeros((R, KW), F32)
        for h in range(GLA_HEADS):
            hm = hmask_ref[h:h + 1, :].astype(BF16)
            att = lax.dot_general(q_dec[rows] * hm, k_dec[rows], (((1,), (1,)), ((), ())),
                                  preferred_element_type=F32)
            o_acc = o_acc + _dot((att * tri_f).astype(BF16), v[rows] * hm)
        o_groups.append(o_acc)

    o_parts = []
    for ci, c0 in enumerate(range(0, ts, C)):
        cs = slice(c0, c0 + C)
        st = st_ref[...]
        o_parts.append(lax.dot_general(q_dec[cs], st.astype(BF16), (((1,), (1,)), ((), ())),
                                       preferred_element_type=F32))
        kv_t = lax.dot_general(v[cs], k_end[cs], (((0,), (0,)), ((), ())),
                               preferred_element_type=F32)
        st_ref[...] = st * jnp.exp2(ends[ci]) + kv_t * bdmask_ref[...]
    o = jnp.concatenate(o_groups, axis=0) + jnp.concatenate(o_parts, axis=0)

    ms = _dot((o * o).astype(BF16), ones_ref[...])
    o = o * lax.rsqrt(ms + EPS) * gain_ref[...]
    g = gla_ref[0, :, 3 * KW:4 * KW].astype(F32)
    o_ref[0] = (o * (g * _sigmoid(g))).astype(o_ref.dtype)


def _fox_gla_kernel(fox_ref, gla_ref, small_ref, bf_ref, qg_ref, kg_ref, fones_ref, rq_ref, rk_ref,
                    wa_ref, ba_ref, gain_ref, tri_ref, hmask_ref, bdmask_ref, gones_ref,
                    kaug_ref, qaugt_ref, vt_ref, oc_ref, cum_ref, st_ref):
    @pl.when(pl.program_id(1) == 0)
    def _():
        cum_ref[...] = jnp.zeros_like(cum_ref)
        st_ref[...] = jnp.zeros_like(st_ref)

    _fox_prep_body(fox_ref, small_ref, bf_ref, qg_ref, kg_ref, fones_ref, rq_ref, rk_ref,
                   kaug_ref, qaugt_ref, vt_ref, cum_ref)
    _gla_body(gla_ref, small_ref, wa_ref, ba_ref, gain_ref, tri_ref, hmask_ref, bdmask_ref,
              gones_ref, oc_ref, st_ref)


def _fox_prep_gla(fox, gla, small, fox_p, gla_p, tm, tk):
    B, S, _ = fox.shape
    H, W, KW = FOX_HEADS, FOX_WIDTH, GLA_KW
    tok = lambda n: pl.BlockSpec((1, tm, n), lambda b, i: (b, i, 0))
    fox_specs = [_const_spec((1, LANES)), _const_spec((1, W)), _const_spec((1, W)), _const_spec((W, W)),
                 _const_spec((LANES, H * AUG)), _const_spec((LANES, H * AUG))]
    gla_specs = [_const_spec((LANES, KW)), _const_spec((1, KW)), _const_spec((1, KW)),
                 _const_spec((GLA_ROWS, GLA_ROWS)), _const_spec((GLA_HEADS, KW)),
                 _const_spec((KW, KW)), _const_spec((KW, KW))]
    return pl.pallas_call(
        _fox_gla_kernel,
        grid=(B, S // tm),
        in_specs=[tok(FOX_COLS), tok(GLA_COLS), tok(SMALL_COLS)] + fox_specs + gla_specs,
        out_specs=[
            pl.BlockSpec((1, H, tm, AUG), lambda b, i: (b, 0, i, 0)),
            pl.BlockSpec((1, H, AUG, tm), lambda b, i: (b, 0, 0, i)),
            pl.BlockSpec((1, H, tm // tk, V_ROWS, tk), lambda b, i: (b, 0, i, 0, 0)),
            tok(KW),
        ],
        out_shape=[
            jax.ShapeDtypeStruct((B, H, S, AUG), BF16),
            jax.ShapeDtypeStruct((B, H, AUG, S), BF16),
            jax.ShapeDtypeStruct((B, H, S // tk, V_ROWS, tk), BF16),
            jax.ShapeDtypeStruct((B, S, KW), BF16),
        ],
        scratch_shapes=[pltpu.VMEM((SUBLANES, LANES), F32), pltpu.VMEM((KW, KW), F32)],
        compiler_params=_cparams(("arbitrary", "arbitrary")),
        name="fox_prep_gla",
    )(fox, gla, small, *fox_p, *gla_p)


def _ffn_chunks(d_ff):
    chunks, c0 = [], 0
    while c0 < d_ff:
        n = min(512, d_ff - c0)
        chunks.append((c0, n))
        c0 += n
    return chunks


def _out_ffn_kernel(x_ref, a_ref, b_ref, c_ref, mod_ref, g_ref, wo_ref, wgu_ref, wd_ref, o_ref,
                    acc_ref):
    d_ff = wd_ref.shape[0]
    mix = _dot(jnp.concatenate([a_ref[0], b_ref[0], c_ref[0]], axis=1), wo_ref[...])
    x1 = x_ref[0] + mod_ref[0, 2:3, :] * mix
    h = _rms_mod(x1, g_ref[...], mod_ref[0, 4:5, :], mod_ref[0, 3:4, :]).astype(BF16)
    for idx, (c0, n) in enumerate(_ffn_chunks(d_ff)):
        gt = _dot(h, wgu_ref[:, c0:c0 + n])
        up = _dot(h, wgu_ref[:, d_ff + c0:d_ff + c0 + n])
        act = (gt * _sigmoid(gt) * up).astype(BF16)
        y = _dot(act, wd_ref[c0:c0 + n, :])
        if idx == 0:
            acc_ref[...] = y
        else:
            acc_ref[...] += y
    o_ref[0] = x1 + mod_ref[0, 5:6, :] * acc_ref[...]


def _out_ffn(x, a, b, c, mod, gain, wo, wgu, wd, layer, tm):
    B, S, D = x.shape
    tok = lambda n: pl.BlockSpec((1, tm, n), lambda bb, i: (bb, i, 0))
    return pl.pallas_call(
        _out_ffn_kernel,
        grid=(B, S // tm),
        in_specs=[
            tok(D), tok(a.shape[2]), tok(b.shape[2]), tok(c.shape[2]),
            pl.BlockSpec((1, N_MOD, D), lambda bb, i: (bb, 0, 0)),
            _const_spec((1, D)),
            _layer_weight_spec(wo.shape, layer), _layer_weight_spec(wgu.shape, layer),
            _layer_weight_spec(wd.shape, layer),
        ],
        out_specs=tok(D),
        out_shape=jax.ShapeDtypeStruct((B, S, D), F32),
        scratch_shapes=[pltpu.VMEM((tm, D), F32)],
        compiler_params=_cparams(("arbitrary", "arbitrary")),
        name="out_proj_ffn",
    )(x, a, b, c, mod, gain, wo, wgu, wd)


def _block_diag(w):
    L, n, d, e = w.shape
    eye = jnp.eye(n, dtype=w.dtype)
    return jnp.einsum("lnde,nm->lndme", w, eye).reshape(L, n * d, n * e)


def _regroup_w_in(w_in):
    L, D, _ = w_in.shape
    sizes = (LRU_WIDTH, LRU_WIDTH, FOX_WIDTH, FOX_WIDTH, FOX_WIDTH, FOX_HEADS,
             GLA_KW, GLA_KW, GLA_VW, GLA_RANK, GLA_VW)
    offs = np.concatenate([[0], np.cumsum(sizes)])
    seg = lambda i: w_in[:, :, offs[i]:offs[i + 1]]
    z = lambda n: jnp.zeros((L, D, n), w_in.dtype)
    cols = [seg(0), seg(1), seg(2), seg(3), seg(4), seg(6), seg(7), seg(8), seg(10),
            seg(5), z(GLOW_LANE0 - FOX_HEADS), seg(9), z(SMALL_COLS - GLOW_LANE0 - GLA_RANK)]
    return jnp.concatenate(cols, axis=-1).astype(BF16)


def _head_ones(n_heads, dim):
    assert dim & (dim - 1) == 0
    return jnp.asarray(np.kron(np.eye(n_heads), np.full((dim, dim), 1.0 / dim)), BF16)


def _fox_constants():
    H, W = FOX_HEADS, FOX_WIDTH
    rq = np.zeros((LANES, H * AUG), np.float32)
    rk = np.zeros((LANES, H * AUG), np.float32)
    for h in range(H):
        for s in range(N_SPLIT):
            rq[s * H + h, h * AUG + HEAD_DIM + s] = 1.0
            rq[ONE_LANE, h * AUG + HEAD_DIM + N_SPLIT + s] = 1.0
            rk[ONE_LANE, h * AUG + HEAD_DIM + s] = 1.0
            rk[s * H + h, h * AUG + HEAD_DIM + N_SPLIT + s] = -1.0
    return jnp.asarray(rq, BF16), jnp.asarray(rk, BF16)


def _gla_constants():
    R, C, KW = GLA_ROWS, GLA_CHUNK, GLA_KW
    i = np.arange(R)
    tri = ((i[:, None] >= i[None, :]) & (i[:, None] // C == i[None, :] // C)).astype(np.float32)
    lane = np.arange(KW)
    hmask = (lane[None, :] // GLA_DK == np.arange(GLA_HEADS)[:, None]).astype(np.float32)
    bdmask = (lane[:, None] // GLA_DK == lane[None, :] // GLA_DK).astype(np.float32)
    return jnp.asarray(tri, BF16), jnp.asarray(hmask, F32), jnp.asarray(bdmask, F32)


def _pick_tile(n, want):
    t = min(n, want)
    while n % t:
        t //= 2
    return t


def kernel(x, c, norm1_gain, norm2_gain, w_mod, b_mod, w_in, conv_w, conv_b, lru_w_r, lru_b_r,
           lru_w_i, lru_b_i, lru_lambda, fox_b_f, fox_q_gain, fox_k_gain, gla_w_alpha, gla_b_alpha,
           gla_out_gain, w_out, ffn_w_gate_up, ffn_w_down):
    B, S, D = x.shape
    L = w_in.shape[0]
    tm = _pick_tile(S, 512)
    tq = _pick_tile(S, 256)

    mod = _modulation(c, w_mod, b_mod)
    w_in_p = _regroup_w_in(w_in)
    wr_bd = _block_diag(lru_w_r).astype(BF16)
    wi_bd = _block_diag(lru_w_i).astype(BF16)
    w_out_b = w_out.astype(BF16)
    wgu_b = ffn_w_gate_up.astype(BF16)
    wd_b = ffn_w_down.astype(BF16)
    row = lambda a: a.reshape(L, 1, a.shape[-1])
    bf_pad = jnp.pad(fox_b_f, ((0, 0), (0, LANES - FOX_HEADS))).reshape(L, 1, LANES)
    qg = jnp.tile(fox_q_gain * (HEAD_DIM ** -0.5 * LOG2E), (1, FOX_HEADS)).reshape(L, 1, FOX_WIDTH)
    kg = jnp.tile(fox_k_gain, (1, FOX_HEADS)).reshape(L, 1, FOX_WIDTH)
    wa_pad = jnp.pad(gla_w_alpha, ((0, 0), (GLOW_LANE0, LANES - GLOW_LANE0 - GLA_RANK), (0, 0))).astype(BF16)
    og = jnp.tile(gla_out_gain, (1, GLA_HEADS)).reshape(L, 1, GLA_VW)
    fox_ones = _head_ones(FOX_HEADS, HEAD_DIM)
    gla_ones = _head_ones(GLA_HEADS, GLA_DK)
    rq, rk = _fox_constants()
    tri, hmask, bdmask = _gla_constants()

    for l in range(L):
        lru_p = (conv_w[l], row(conv_b)[l], wr_bd[l], row(lru_b_r)[l], wi_bd[l], row(lru_b_i)[l],
                 row(lru_lambda)[l])
        out_a, fox, gla, small = _in_proj(x, mod[l], row(norm1_gain)[l], w_in_p, lru_p, l, tm)
        fox_p = (bf_pad[l], qg[l], kg[l], fox_ones, rq, rk)
        gla_p = (wa_pad[l], row(gla_b_alpha)[l], og[l], tri, hmask, bdmask, gla_ones)
        kaug, qaugt, vt, out_c = _fox_prep_gla(fox, gla, small, fox_p, gla_p, tm, tq)
        out_b = _fox_attention(kaug, qaugt, vt, tq)
        x = _out_ffn(x, out_a, out_b, out_c, mod[l], row(norm2_gain)[l], w_out_b, wgu_b, wd_b, l, tm)
    return x
```

```python
import functools

import numpy as np
import jax
import jax.numpy as jnp
from jax import lax
from jax.experimental import pallas as pl
from jax.experimental.pallas import tpu as pltpu

F32 = jnp.float32
BF16 = jnp.bfloat16

EPS = 1e-6
HEAD_DIM = 64
LRU_WIDTH = 384
LRU_BLOCKS = 6
CONV_WIDTH = 4
LRU_C = 8.0
FOX_HEADS = 6
FOX_WIDTH = FOX_HEADS * HEAD_DIM
GLA_HEADS = 4
GLA_DK = 64
GLA_KW = GLA_HEADS * GLA_DK
GLA_VW = GLA_KW
GLA_RANK = 16
GLA_TAU = 16.0
GLA_CHUNK = 64
N_MOD = 6

LANES = 128
SUBLANES = 8
VMEM_LIMIT_BYTES = 56 * 1024 * 1024

LRU_COLS = 2 * LRU_WIDTH
FOX_COLS = 3 * FOX_WIDTH
GLA_COLS = 3 * GLA_KW + GLA_VW
SMALL_COLS = LANES
GLOW_LANE0 = 8
NP_COLS = LRU_COLS + FOX_COLS + GLA_COLS + SMALL_COLS
MXU_COLS = 256
LRU_ROWS = 128

AUG = LANES
N_SPLIT = 3
ONE_LANE = N_SPLIT * FOX_HEADS
NEG_BIG = -1e30
V_ROWS = HEAD_DIM + 16
LOG2E = float(np.log2(np.e))


def _sigmoid(x):
    return 0.5 * jnp.tanh(0.5 * x) + 0.5


def _log_sigmoid(x):
    return jnp.minimum(x, 0.0) - jnp.log(1.0 + jnp.exp(-jnp.abs(x)))


def _softplus(x):
    return jnp.maximum(x, 0.0) + jnp.log(1.0 + jnp.exp(-jnp.abs(x)))


def _gelu_tanh(x):
    c = np.float32(np.sqrt(2.0 / np.pi))
    return 0.5 * x * (1.0 + jnp.tanh(c * (x + 0.044715 * (x * x * x))))


def _dot(a, b):
    return jnp.dot(a, b, preferred_element_type=F32)


def _cparams(sem):
    return pltpu.CompilerParams(dimension_semantics=sem, vmem_limit_bytes=VMEM_LIMIT_BYTES)


def _const_spec(shape):
    nd = len(shape)
    return pl.BlockSpec(shape, lambda *_: (0,) * nd)


def _layer_weight_spec(stacked_shape, layer):
    tail = tuple(stacked_shape[1:])
    return pl.BlockSpec((None,) + tail, lambda *_: (layer,) + (0,) * len(tail),
                        pipeline_mode=pl.Buffered(1))


def _mod_kernel(c_ref, w_ref, b_ref, o_ref):
    c = c_ref[...]
    ca = (c * _sigmoid(c)).astype(BF16)
    o_ref[0] = _dot(ca, w_ref[0].astype(BF16)) + b_ref[0]


def _modulation(c, w_mod, b_mod):
    L, D, N = w_mod.shape
    B = c.shape[0]
    bp = -(-B // SUBLANES) * SUBLANES
    cp = jnp.pad(c, ((0, bp - B), (0, 0)))
    tn = 1024
    out = pl.pallas_call(
        _mod_kernel,
        grid=(L, N // tn),
        in_specs=[
            pl.BlockSpec((bp, D), lambda l, n: (0, 0)),
            pl.BlockSpec((1, D, tn), lambda l, n: (l, 0, n)),
            pl.BlockSpec((1, 1, tn), lambda l, n: (l, 0, n)),
        ],
        out_specs=pl.BlockSpec((1, bp, tn), lambda l, n: (l, 0, n)),
        out_shape=jax.ShapeDtypeStruct((L, bp, N), F32),
        compiler_params=_cparams(("arbitrary", "arbitrary")),
        name="adaln_mod",
    )(cp, w_mod, b_mod.reshape(L, 1, N))
    return out[:, :B].reshape(L, B, N_MOD, D)


def _rms_mod(x, gain, scale, shift):
    ms = jnp.mean(x * x, axis=-1, keepdims=True)
    y = x * lax.rsqrt(ms + EPS) * gain
    return y * (1.0 + scale) + shift


def _in_proj_kernel(x_ref, mod_ref, g_ref, w_ref, cw_ref, cb_ref, wr_ref, br_ref, wi_ref, bi_ref,
                    lam_ref, oa_ref, fox_ref, gla_ref, small_ref, xs_ref, h_ref):
    tm, W = x_ref.shape[1], LRU_WIDTH

    @pl.when(pl.program_id(1) == 0)
    def _():
        xs_ref[0:SUBLANES, :] = jnp.zeros((SUBLANES, W), F32)
        h_ref[...] = jnp.zeros_like(h_ref)

    h = _rms_mod(x_ref[0], g_ref[...], mod_ref[0, 1:2, :], mod_ref[0, 0:1, :]).astype(BF16)
    segs, c0 = [], LRU_COLS
    for ref, n in ((fox_ref, FOX_COLS), (gla_ref, GLA_COLS), (small_ref, SMALL_COLS)):
        segs.append((ref, c0, c0 + n))
        c0 += n

    def project(g0):
        g1 = g0 + MXU_COLS
        y = _dot(h, w_ref[:, g0:g1])
        for ref, s0, s1 in segs:
            lo, hi = max(g0, s0), min(g1, s1)
            if lo < hi:
                ref[0, :, lo - s0:hi - s0] = y[:, lo - g0:hi - g0].astype(ref.dtype)

    y0 = _dot(h, w_ref[:, 0:LRU_COLS])
    xs_ref[SUBLANES:SUBLANES + tm, :] = y0[:, 0:W]
    pieces = [(s, r0) for s in range(W // LANES) for r0 in range(0, tm, LRU_ROWS)]
    gates = [_lru_gates(y0[r0:r0 + LRU_ROWS, s * LANES:(s + 1) * LANES], r0, slice(s * LANES, (s + 1) * LANES),
                        cw_ref, cb_ref, wr_ref, wi_ref, xs_ref) for s, r0 in pieces]
    xs_ref[0:SUBLANES, :] = xs_ref[tm:tm + SUBLANES, :]
    chunks = list(range(LRU_COLS, NP_COLS, MXU_COLS))
    carry = None
    for (s, r0), g in zip(pieces, gates):
        sl = slice(s * LANES, (s + 1) * LANES)
        rows = slice(r0, r0 + LRU_ROWS)
        if chunks:
            project(chunks.pop(0))
        if r0 == 0:
            carry = h_ref[0:1, sl]
        out, carry = _lru_finish(*g, y0[rows, W + s * LANES:W + (s + 1) * LANES], sl, carry,
                                 br_ref, bi_ref, lam_ref)
        oa_ref[0, rows, sl] = out.astype(oa_ref.dtype)
        if r0 + LRU_ROWS == tm:
            h_ref[:, sl] = jnp.broadcast_to(carry, (h_ref.shape[0], LANES))
    for g0 in chunks:
        project(g0)


def _in_proj(x, mod, gain, w, lru_p, layer, tm):
    B, S, D = x.shape
    W = LRU_WIDTH
    tok = lambda n: pl.BlockSpec((1, tm, n), lambda b, i: (b, i, 0))
    return pl.pallas_call(
        _in_proj_kernel,
        grid=(B, S // tm),
        in_specs=[
            tok(D),
            pl.BlockSpec((1, N_MOD, D), lambda b, i: (b, 0, 0)),
            _const_spec((1, D)),
            _layer_weight_spec(w.shape, layer),
            _const_spec((CONV_WIDTH, W)), _const_spec((1, W)),
            _const_spec((W, W)), _const_spec((1, W)),
            _const_spec((W, W)), _const_spec((1, W)),
            _const_spec((1, W)),
        ],
        out_specs=[tok(W), tok(FOX_COLS), tok(GLA_COLS), tok(SMALL_COLS)],
        out_shape=[
            jax.ShapeDtypeStruct((B, S, W), BF16),
            jax.ShapeDtypeStruct((B, S, FOX_COLS), BF16),
            jax.ShapeDtypeStruct((B, S, GLA_COLS), BF16),
            jax.ShapeDtypeStruct((B, S, SMALL_COLS), F32),
        ],
        scratch_shapes=[pltpu.VMEM((tm + SUBLANES, W), F32), pltpu.VMEM((SUBLANES, W), F32)],
        compiler_params=_cparams(("arbitrary", "arbitrary")),
        name="in_proj_lru",
    )(x, mod, gain, w, *lru_p)


def _scan_add(v):
    n = v.shape[0]
    row = lax.broadcasted_iota(jnp.int32, v.shape, 0)
    d = 1
    while d < n:
        v = v + jnp.where(row >= d, pltpu.roll(v, d, axis=0), 0.0)
        d *= 2
    return v


def _scan_linear(a, u, h0):
    n, w = a.shape
    g = n // SUBLANES
    a = a.reshape(g, SUBLANES, w)
    u = u.reshape(g, SUBLANES, w)
    sub = lax.broadcasted_iota(jnp.int32, a.shape, 1)
    d = 1
    while d < SUBLANES:
        m = sub >= d
        u = u + jnp.where(m, a * pltpu.roll(u, d, axis=1), 0.0)
        a = jnp.where(m, a * pltpu.roll(a, d, axis=1), a)
        d *= 2
    hs, h = [], h0
    for v in range(g):
        hv = a[v] * h + u[v]
        h = hv[SUBLANES - 1:SUBLANES, :]
        hs.append(hv)
    return jnp.concatenate(hs, axis=0), h


def _lru_gates(xa, r0, sl, cw_ref, cb_ref, wr_ref, wi_ref, xs_ref):
    n = xa.shape[0]
    xc = xa * cw_ref[CONV_WIDTH - 1:CONV_WIDTH, sl] + cb_ref[:, sl]
    for k in range(1, CONV_WIDTH):
        j = CONV_WIDTH - 1 - k
        xc = xc + xs_ref[pl.ds(SUBLANES - k + r0, n), sl] * cw_ref[j:j + 1, sl]
    xcb = xc.astype(BF16)
    return xc, _dot(xcb, wr_ref[sl, sl]), _dot(xcb, wi_ref[sl, sl])


def _lru_finish(xc, r_pre, i_pre, ya, sl, h0, br_ref, bi_ref, lam_ref):
    r = _sigmoid(r_pre + br_ref[:, sl])
    i = _sigmoid(i_pre + bi_ref[:, sl])
    a = jnp.exp2(r * ((-LRU_C * LOG2E) * _softplus(-lam_ref[:, sl])))
    mult = jnp.sqrt(1.0 - a * a)
    u = mult * (i * xc)
    h, h_last = _scan_linear(a, u, h0)
    return h * _gelu_tanh(ya), h_last


def _fox_prep_body(fox_ref, small_ref, bf_ref, qg_ref, kg_ref, ones_ref,
                   rq_ref, rk_ref, kaug_ref, qaugt_ref, vt_ref, cum_ref):
    tm = fox_ref.shape[1]
    W = FOX_WIDTH
    q = fox_ref[0, :, 0:W].astype(F32)
    k = fox_ref[0, :, W:2 * W].astype(F32)
    v = fox_ref[0, :, 2 * W:3 * W].astype(F32)

    def head_norm(t, gain):
        ms = _dot((t * t).astype(BF16), ones_ref[...])
        return t * lax.rsqrt(ms + EPS) * gain

    qn = head_norm(q, qg_ref[...])
    kn = head_norm(k, kg_ref[...])

    lf = _log_sigmoid(small_ref[0] + bf_ref[...])
    cum = _scan_add(lf) + cum_ref[0:1, :]
    cum_ref[...] = jnp.broadcast_to(cum[tm - 1:tm, :], cum_ref.shape)

    cs = cum * LOG2E
    c1 = cs.astype(BF16).astype(F32)
    r1 = cs - c1
    c2 = r1.astype(BF16).astype(F32)
    c3 = (r1 - c2).astype(BF16).astype(F32)
    lane = lax.broadcasted_iota(jnp.int32, cum.shape, 1)
    H = FOX_HEADS
    packed = jnp.where(lane < H, c1, 0.0)
    packed = jnp.where((lane >= H) & (lane < 2 * H), pltpu.roll(c2, H, axis=1), packed)
    packed = jnp.where((lane >= 2 * H) & (lane < 3 * H), pltpu.roll(c3, 2 * H, axis=1), packed)
    packed = jnp.where(lane == ONE_LANE, 1.0, packed).astype(BF16)

    kbias = _dot(packed, rk_ref[...])
    qbias = _dot(packed, rq_ref[...])
    feat = lane < HEAD_DIM

    def head_aug(t, bias, h):
        src = t[:, (h // 2) * LANES:(h // 2 + 1) * LANES]
        if h % 2:
            src = pltpu.roll(src, HEAD_DIM, axis=1)
        return jnp.where(feat, src, bias[:, h * AUG:(h + 1) * AUG])

    vt = v.T
    tk = vt_ref.shape[4]
    for h in range(H):
        kaug_ref[0, h] = head_aug(kn, kbias, h).astype(BF16)
        qaugt_ref[0, h] = head_aug(qn, qbias, h).T.astype(BF16)
        for cb in range(tm // tk):
            vt_ref[0, h, cb, 0:HEAD_DIM, :] = (
                vt[h * HEAD_DIM:(h + 1) * HEAD_DIM, cb * tk:(cb + 1) * tk].astype(BF16))
            vt_ref[0, h, cb, HEAD_DIM:V_ROWS, :] = jnp.ones((V_ROWS - HEAD_DIM, tk), BF16)


def _fox_attn_kernel(k_ref, qt_ref, vt_ref, o_ref, acc_ref, sa_ref, sb_ref, *, tq):
    qi = pl.program_id(1)
    H = FOX_HEADS
    acc_ref[...] = jnp.zeros_like(acc_ref)

    def scores(j, s_ref):
        start = pl.multiple_of(j * tq, tq)
        cmax = []
        for h in range(H):
            s = _dot(k_ref[0, h, pl.ds(start, tq), :], qt_ref[0, h])
            s_ref[h] = s
            cmax.append(jnp.max(s, axis=0, keepdims=True))
        return tuple(cmax)

    def update(h, j, s, m_old, m_new):
        p = jnp.exp2(s - m_new).astype(BF16)
        acc_ref[h] = jnp.exp2(m_old - m_new) * acc_ref[h] + _dot(vt_ref[0, h, j], p)

    def masked_step(j, cur_ref, mask, ms, nxt_ref=None, j_nxt=None):
        new_ms = []
        for h in range(H):
            if nxt_ref is not None:
                start = pl.multiple_of(j_nxt * tq, tq)
                nxt_ref[h] = _dot(k_ref[0, h, pl.ds(start, tq), :], qt_ref[0, h])
            s = jnp.where(mask, cur_ref[h], NEG_BIG)
            m_new = jnp.maximum(ms[h], jnp.max(s, axis=0, keepdims=True))
            new_ms.append(m_new)
            update(h, j, s, ms[h], m_new)
        return tuple(new_ms)

    def step(j, cur_ref, c_cur, nxt_ref, ms):
        start = pl.multiple_of((j + 1) * tq, tq)
        new_ms, c_nxt = [], []
        for h in range(H):
            s_n = _dot(k_ref[0, h, pl.ds(start, tq), :], qt_ref[0, h])
            nxt_ref[h] = s_n
            c_nxt.append(jnp.max(s_n, axis=0, keepdims=True))
            m_new = jnp.maximum(ms[h], c_cur[h])
            new_ms.append(m_new)
            update(h, j, cur_ref[h], ms[h], m_new)
        return tuple(new_ms), tuple(c_nxt)

    def pair(t, carry):
        ms, ca = carry
        j0 = 2 * t
        ms, cb = step(j0, sa_ref, ca, sb_ref, ms)
        ms, ca = step(j0 + 1, sb_ref, cb, sa_ref, ms)
        return ms, ca

    carry = (tuple(jnp.full((1, tq), NEG_BIG, F32) for _ in range(H)), scores(0, sa_ref))
    n_pairs = lax.div(qi, 2)
    n_quads = lax.div(n_pairs, 2)
    carry = lax.fori_loop(0, n_quads, lambda u, c: pair(2 * u + 1, pair(2 * u, c)), carry)
    carry = lax.fori_loop(2 * n_quads, n_pairs, pair, carry)
    rel = (lax.broadcasted_iota(jnp.int32, (tq, tq), 0)
           - lax.broadcasted_iota(jnp.int32, (tq, tq), 1))
    j0 = 2 * n_pairs
    ms = masked_step(j0, sa_ref, rel <= (qi - j0) * tq, carry[0], sb_ref, qi)

    @pl.when(qi > j0)
    def _():
        masked_step(qi, sb_ref, rel <= 0, ms)

    out = jnp.concatenate(
        [acc_ref[h, 0:HEAD_DIM, :] * (1.0 / acc_ref[h, HEAD_DIM:HEAD_DIM + 1, :]) for h in range(H)], axis=0)
    o_ref[0] = out.T.astype(o_ref.dtype)


def _fox_attention(kaug, qaugt, vt, tq):
    B, H, S, _ = kaug.shape
    return pl.pallas_call(
        functools.partial(_fox_attn_kernel, tq=tq),
        grid=(B, S // tq),
        in_specs=[
            pl.BlockSpec((1, H, S, AUG), lambda b, i: (b, 0, 0, 0)),
            pl.BlockSpec((1, H, AUG, tq), lambda b, i: (b, 0, 0, i)),
            pl.BlockSpec((1, H, S // tq, V_ROWS, tq), lambda b, i: (b, 0, 0, 0, 0)),
        ],
        out_specs=pl.BlockSpec((1, tq, H * HEAD_DIM), lambda b, i: (b, i, 0)),
        out_shape=jax.ShapeDtypeStruct((B, S, H * HEAD_DIM), BF16),
        scratch_shapes=[pltpu.VMEM((H, V_ROWS, tq), F32), pltpu.VMEM((H, tq, tq), F32),
                        pltpu.VMEM((H, tq, tq), F32)],
        compiler_params=_cparams(("arbitrary", "arbitrary")),
        name="fox_attention",
    )(kaug, qaugt, vt)


GLA_ROWS = 256


def _gla_body(gla_ref, small_ref, wa_ref, ba_ref, gain_ref, tri_ref, hmask_ref, bdmask_ref,
              ones_ref, o_ref, st_ref):
    ts = o_ref.shape[1]
    KW, C, R = GLA_KW, GLA_CHUNK, GLA_ROWS
    q = gla_ref[0, :, 0:KW].astype(F32) * (GLA_DK ** -0.5)
    k = gla_ref[0, :, KW:2 * KW].astype(F32)
    v = gla_ref[0, :, 2 * KW:3 * KW]
    low = small_ref[0].astype(BF16)
    la = _log_sigmoid(_dot(low, wa_ref[...]) + ba_ref[...]) * (LOG2E / GLA_TAU)
    la_hi = la.astype(BF16)
    la_lo = (la - la_hi.astype(F32)).astype(BF16)
    tri = tri_ref[...]
    bcum = jnp.concatenate(
        [_dot(tri, la_hi[g0:g0 + R]) + _dot(tri, la_lo[g0:g0 + R]) for g0 in range(0, ts, R)], axis=0)
    ends = [bcum[c0 + C - 1:c0 + C, :] for c0 in range(0, ts, C)]
    b_last = jnp.concatenate([jnp.broadcast_to(e, (C, KW)) for e in ends], axis=0)
    q_dec = (q * jnp.exp2(bcum)).astype(BF16)
    k_dec = (k * jnp.exp2(-bcum)).astype(BF16)
    k_end = (k * jnp.exp2(b_last - bcum)).astype(BF16)

    tri_f = tri.astype(F32)
    o_groups = []
    for g0 in range(0, ts, R):
        rows = slice(g0, g0 + R)
        o_acc = jnp.zeros((R, KW), F32)
        for h in range(GLA_HEADS):
            hm = hmask_ref[h:h + 1, :].astype(BF16)
            att = lax.dot_general(q_dec[rows] * hm, k_dec[rows], (((1,), (1,)), ((), ())),
                                  preferred_element_type=F32)
            o_acc = o_acc + _dot((att * tri_f).astype(BF16), v[rows] * hm)
        o_groups.append(o_acc)

    o_parts = []
    for ci, c0 in enumerate(range(0, ts, C)):
        cs = slice(c0, c0 + C)
        st = st_ref[...]
        o_parts.append(lax.dot_general(q_dec[cs], st.astype(BF16), (((1,), (1,)), ((), ())),
                                       preferred_element_type=F32))
        kv_t = lax.dot_general(v[cs], k_end[cs], (((0,), (0,)), ((), ())),
                               preferred_element_type=F32)
        st_ref[...] = st * jnp.exp2(ends[ci]) + kv_t * bdmask_ref[...]
    o = jnp.concatenate(o_groups, axis=0) + jnp.concatenate(o_parts, axis=0)

    ms = _dot((o * o).astype(BF16), ones_ref[...])
    o = o * lax.rsqrt(ms + EPS) * gain_ref[...]
    g = gla_ref[0, :, 3 * KW:4 * KW].astype(F32)
    o_ref[0] = (o * (g * _sigmoid(g))).astype(o_ref.dtype)


def _fox_gla_kernel(fox_ref, gla_ref, small_ref, bf_ref, qg_ref, kg_ref, fones_ref, rq_ref, rk_ref,
                    wa_ref, ba_ref, gain_ref, tri_ref, hmask_ref, bdmask_ref, gones_ref,
                    kaug_ref, qaugt_ref, vt_ref, oc_ref, cum_ref, st_ref):
    @pl.when(pl.program_id(1) == 0)
    def _():
        cum_ref[...] = jnp.zeros_like(cum_ref)
        st_ref[...] = jnp.zeros_like(st_ref)

    _fox_prep_body(fox_ref, small_ref, bf_ref, qg_ref, kg_ref, fones_ref, rq_ref, rk_ref,
                   kaug_ref, qaugt_ref, vt_ref, cum_ref)
    _gla_body(gla_ref, small_ref, wa_ref, ba_ref, gain_ref, tri_ref, hmask_ref, bdmask_ref,
              gones_ref, oc_ref, st_ref)


def _fox_prep_gla(fox, gla, small, fox_p, gla_p, tm, tk):
    B, S, _ = fox.shape
    H, W, KW = FOX_HEADS, FOX_WIDTH, GLA_KW
    tok = lambda n: pl.BlockSpec((1, tm, n), lambda b, i: (b, i, 0))
    fox_specs = [_const_spec((1, LANES)), _const_spec((1, W)), _const_spec((1, W)), _const_spec((W, W)),
                 _const_spec((LANES, H * AUG)), _const_spec((LANES, H * AUG))]
    gla_specs = [_const_spec((LANES, KW)), _const_spec((1, KW)), _const_spec((1, KW)),
                 _const_spec((GLA_ROWS, GLA_ROWS)), _const_spec((GLA_HEADS, KW)),
                 _const_spec((KW, KW)), _const_spec((KW, KW))]
    return pl.pallas_call(
        _fox_gla_kernel,
        grid=(B, S // tm),
        in_specs=[tok(FOX_COLS), tok(GLA_COLS), tok(SMALL_COLS)] + fox_specs + gla_specs,
        out_specs=[
            pl.BlockSpec((1, H, tm, AUG), lambda b, i: (b, 0, i, 0)),
            pl.BlockSpec((1, H, AUG, tm), lambda b, i: (b, 0, 0, i)),
            pl.BlockSpec((1, H, tm // tk, V_ROWS, tk), lambda b, i: (b, 0, i, 0, 0)),
            tok(KW),
        ],
        out_shape=[
            jax.ShapeDtypeStruct((B, H, S, AUG), BF16),
            jax.ShapeDtypeStruct((B, H, AUG, S), BF16),
            jax.ShapeDtypeStruct((B, H, S // tk, V_ROWS, tk), BF16),
            jax.ShapeDtypeStruct((B, S, KW), BF16),
        ],
        scratch_shapes=[pltpu.VMEM((SUBLANES, LANES), F32), pltpu.VMEM((KW, KW), F32)],
        compiler_params=_cparams(("arbitrary", "arbitrary")),
        name="fox_prep_gla",
    )(fox, gla, small, *fox_p, *gla_p)


def _ffn_chunks(d_ff):
    chunks, c0 = [], 0
    while c0 < d_ff:
        n = min(512, d_ff - c0)
        chunks.append((c0, n))
        c0 += n
    return chunks


def _out_ffn_kernel(x_ref, a_ref, b_ref, c_ref, mod_ref, g_ref, wo_ref, wgu_ref, wd_ref, o_ref,
                    acc_ref):
    d_ff = wd_ref.shape[0]
    mix = _dot(jnp.concatenate([a_ref[0], b_ref[0], c_ref[0]], axis=1), wo_ref[...])
    x1 = x_ref[0] + mod_ref[0, 2:3, :] * mix
    h = _rms_mod(x1, g_ref[...], mod_ref[0, 4:5, :], mod_ref[0, 3:4, :]).astype(BF16)
    for idx, (c0, n) in enumerate(_ffn_chunks(d_ff)):
        gt = _dot(h, wgu_ref[:, c0:c0 + n])
        up = _dot(h, wgu_ref[:, d_ff + c0:d_ff + c0 + n])
        act = (gt * _sigmoid(gt) * up).astype(BF16)
        y = _dot(act, wd_ref[c0:c0 + n, :])
        if idx == 0:
            acc_ref[...] = y
        else:
            acc_ref[...] += y
    o_ref[0] = x1 + mod_ref[0, 5:6, :] * acc_ref[...]


def _out_ffn(x, a, b, c, mod, gain, wo, wgu, wd, layer, tm):
    B, S, D = x.shape
    tok = lambda n: pl.BlockSpec((1, tm, n), lambda bb, i: (bb, i, 0))
    return pl.pallas_call(
        _out_ffn_kernel,
        grid=(B, S // tm),
        in_specs=[
            tok(D), tok(a.shape[2]), tok(b.shape[2]), tok(c.shape[2]),
            pl.BlockSpec((1, N_MOD, D), lambda bb, i: (bb, 0, 0)),
            _const_spec((1, D)),
            _layer_weight_spec(wo.shape, layer), _layer_weight_spec(wgu.shape, layer),
            _layer_weight_spec(wd.shape, layer),
        ],
        out_specs=tok(D),
        out_shape=jax.ShapeDtypeStruct((B, S, D), F32),
        scratch_shapes=[pltpu.VMEM((tm, D), F32)],
        compiler_params=_cparams(("arbitrary", "arbitrary")),
        name="out_proj_ffn",
    )(x, a, b, c, mod, gain, wo, wgu, wd)


def _block_diag(w):
    L, n, d, e = w.shape
    eye = jnp.eye(n, dtype=w.dtype)
    return jnp.einsum("lnde,nm->lndme", w, eye).reshape(L, n * d, n * e)


def _regroup_w_in(w_in):
    L, D, _ = w_in.shape
    sizes = (LRU_WIDTH, LRU_WIDTH, FOX_WIDTH, FOX_WIDTH, FOX_WIDTH, FOX_HEADS,
             GLA_KW, GLA_KW, GLA_VW, GLA_RANK, GLA_VW)
    offs = np.concatenate([[0], np.cumsum(sizes)])
    seg = lambda i: w_in[:, :, offs[i]:offs[i + 1]]
    z = lambda n: jnp.zeros((L, D, n), w_in.dtype)
    cols = [seg(0), seg(1), seg(2), seg(3), seg(4), seg(6), seg(7), seg(8), seg(10),
            seg(5), z(GLOW_LANE0 - FOX_HEADS), seg(9), z(SMALL_COLS - GLOW_LANE0 - GLA_RANK)]
    return jnp.concatenate(cols, axis=-1).astype(BF16)


def _head_ones(n_heads, dim):
    assert dim & (dim - 1) == 0
    return jnp.asarray(np.kron(np.eye(n_heads), np.full((dim, dim), 1.0 / dim)), BF16)


def _fox_constants():
    H, W = FOX_HEADS, FOX_WIDTH
    rq = np.zeros((LANES, H * AUG), np.float32)
    rk = np.zeros((LANES, H * AUG), np.float32)
    for h in range(H):
        for s in range(N_SPLIT):
            rq[s * H + h, h * AUG + HEAD_DIM + s] = 1.0
            rq[ONE_LANE, h * AUG + HEAD_DIM + N_SPLIT + s] = 1.0
            rk[ONE_LANE, h * AUG + HEAD_DIM + s] = 1.0
            rk[s * H + h, h * AUG + HEAD_DIM + N_SPLIT + s] = -1.0
    return jnp.asarray(rq, BF16), jnp.asarray(rk, BF16)


def _gla_constants():
    R, C, KW = GLA_ROWS, GLA_CHUNK, GLA_KW
    i = np.arange(R)
    tri = ((i[:, None] >= i[None, :]) & (i[:, None] // C == i[None, :] // C)).astype(np.float32)
    lane = np.arange(KW)
    hmask = (lane[None, :] // GLA_DK == np.arange(GLA_HEADS)[:, None]).astype(np.float32)
    bdmask = (lane[:, None] // GLA_DK == lane[None, :] // GLA_DK).astype(np.float32)
    return jnp.asarray(tri, BF16), jnp.asarray(hmask, F32), jnp.asarray(bdmask, F32)


def _pick_tile(n, want):
    t = min(n, want)
    while n % t:
        t //= 2
    return t


def kernel(x, c, norm1_gain, norm2_gain, w_mod, b_mod, w_in, conv_w, conv_b, lru_w_r, lru_b_r,
           lru_w_i, lru_b_i, lru_lambda, fox_b_f, fox_q_gain, fox_k_gain, gla_w_alpha, gla_b_alpha,
           gla_out_gain, w_out, ffn_w_gate_up, ffn_w_down):
    B, S, D = x.shape
    L = w_in.shape[0]
    tm = _pick_tile(S, 512)
    tp = _pick_tile(S, 1024)
    tq = _pick_tile(S, 256)

    mod = _modulation(c, w_mod, b_mod)
    w_in_p = _regroup_w_in(w_in)
    wr_bd = _block_diag(lru_w_r).astype(BF16)
    wi_bd = _block_diag(lru_w_i).astype(BF16)
    w_out_b = w_out.astype(BF16)
    wgu_b = ffn_w_gate_up.astype(BF16)
    wd_b = ffn_w_down.astype(BF16)
    row = lambda a: a.reshape(L, 1, a.shape[-1])
    bf_pad = jnp.pad(fox_b_f, ((0, 0), (0, LANES - FOX_HEADS))).reshape(L, 1, LANES)
    qg = jnp.tile(fox_q_gain * (HEAD_DIM ** -0.5 * LOG2E), (1, FOX_HEADS)).reshape(L, 1, FOX_WIDTH)
    kg = jnp.tile(fox_k_gain, (1, FOX_HEADS)).reshape(L, 1, FOX_WIDTH)
    wa_pad = jnp.pad(gla_w_alpha, ((0, 0), (GLOW_LANE0, LANES - GLOW_LANE0 - GLA_RANK), (0, 0))).astype(BF16)
    og = jnp.tile(gla_out_gain, (1, GLA_HEADS)).reshape(L, 1, GLA_VW)
    fox_ones = _head_ones(FOX_HEADS, HEAD_DIM)
    gla_ones = _head_ones(GLA_HEADS, GLA_DK)
    rq, rk = _fox_constants()
    tri, hmask, bdmask = _gla_constants()

    for l in range(L):
        lru_p = (conv_w[l], row(conv_b)[l], wr_bd[l], row(lru_b_r)[l], wi_bd[l], row(lru_b_i)[l],
                 row(lru_lambda)[l])
        out_a, fox, gla, small = _in_proj(x, mod[l], row(norm1_gain)[l], w_in_p, lru_p, l, tp)
        fox_p = (bf_pad[l], qg[l], kg[l], fox_ones, rq, rk)
        gla_p = (wa_pad[l], row(gla_b_alpha)[l], og[l], tri, hmask, bdmask, gla_ones)
        kaug, qaugt, vt, out_c = _fox_prep_gla(fox, gla, small, fox_p, gla_p, tp, tq)
        out_b = _fox_attention(kaug, qaugt, vt, tq)
        x = _out_ffn(x, out_a, out_b, out_c, mod[l], row(norm2_gain)[l], w_out_b, wgu_b, wd_b, l, tm)
    return x
```

```python
import functools

import numpy as np
import jax
import jax.numpy as jnp
from jax import lax
from jax.experimental import pallas as pl
from jax.experimental.pallas import tpu as pltpu

F32 = jnp.float32
BF16 = jnp.bfloat16

EPS = 1e-6
HEAD_DIM = 64
LRU_WIDTH = 384
LRU_BLOCKS = 6
CONV_WIDTH = 4
LRU_C = 8.0
FOX_HEADS = 6
FOX_WIDTH = FOX_HEADS * HEAD_DIM
GLA_HEADS = 4
GLA_DK = 64
GLA_KW = GLA_HEADS * GLA_DK
GLA_VW = GLA_KW
GLA_RANK = 16
GLA_TAU = 16.0
GLA_CHUNK = 64
N_MOD = 6

LANES = 128
SUBLANES = 8
VMEM_LIMIT_BYTES = 56 * 1024 * 1024

LRU_COLS = 2 * LRU_WIDTH
FOX_COLS = 3 * FOX_WIDTH
GLA_COLS = 3 * GLA_KW + GLA_VW
SMALL_COLS = LANES
GLOW_LANE0 = 8
NP_COLS = LRU_COLS + FOX_COLS + GLA_COLS + SMALL_COLS
MXU_COLS = 256
LRU_ROWS = 128

AUG = LANES
N_SPLIT = 3
ONE_LANE = N_SPLIT * FOX_HEADS
NEG_BIG = -1e30
V_ROWS = HEAD_DIM + 16
NQ_SUB = 2
LOG2E = float(np.log2(np.e))


def _sigmoid(x):
    return 0.5 * jnp.tanh(0.5 * x) + 0.5


def _log_sigmoid(x):
    return jnp.minimum(x, 0.0) - jnp.log(1.0 + jnp.exp(-jnp.abs(x)))


def _softplus(x):
    return jnp.maximum(x, 0.0) + jnp.log(1.0 + jnp.exp(-jnp.abs(x)))


def _gelu_tanh(x):
    c = np.float32(np.sqrt(2.0 / np.pi))
    return 0.5 * x * (1.0 + jnp.tanh(c * (x + 0.044715 * (x * x * x))))


def _dot(a, b):
    return jnp.dot(a, b, preferred_element_type=F32)


def _cparams(sem):
    return pltpu.CompilerParams(dimension_semantics=sem, vmem_limit_bytes=VMEM_LIMIT_BYTES)


def _const_spec(shape):
    nd = len(shape)
    return pl.BlockSpec(shape, lambda *_: (0,) * nd)


def _layer_weight_spec(stacked_shape, layer):
    tail = tuple(stacked_shape[1:])
    return pl.BlockSpec((None,) + tail, lambda *_: (layer,) + (0,) * len(tail),
                        pipeline_mode=pl.Buffered(1))


def _mod_kernel(c_ref, w_ref, b_ref, o_ref):
    c = c_ref[...]
    ca = (c * _sigmoid(c)).astype(BF16)
    o_ref[0] = _dot(ca, w_ref[0].astype(BF16)) + b_ref[0]


def _modulation(c, w_mod, b_mod):
    L, D, N = w_mod.shape
    B = c.shape[0]
    bp = -(-B // SUBLANES) * SUBLANES
    cp = jnp.pad(c, ((0, bp - B), (0, 0)))
    tn = 1024
    out = pl.pallas_call(
        _mod_kernel,
        grid=(L, N // tn),
        in_specs=[
            pl.BlockSpec((bp, D), lambda l, n: (0, 0)),
            pl.BlockSpec((1, D, tn), lambda l, n: (l, 0, n)),
            pl.BlockSpec((1, 1, tn), lambda l, n: (l, 0, n)),
        ],
        out_specs=pl.BlockSpec((1, bp, tn), lambda l, n: (l, 0, n)),
        out_shape=jax.ShapeDtypeStruct((L, bp, N), F32),
        compiler_params=_cparams(("arbitrary", "arbitrary")),
        name="adaln_mod",
    )(cp, w_mod, b_mod.reshape(L, 1, N))
    return out[:, :B].reshape(L, B, N_MOD, D)


def _rms_mod(x, gain, scale, shift):
    ms = jnp.mean(x * x, axis=-1, keepdims=True)
    y = x * lax.rsqrt(ms + EPS) * gain
    return y * (1.0 + scale) + shift


def _in_proj_kernel(x_ref, mod_ref, g_ref, w_ref, cw_ref, cb_ref, wr_ref, br_ref, wi_ref, bi_ref,
                    lam_ref, oa_ref, fox_ref, gla_ref, small_ref, xs_ref, h_ref):
    tm, W = x_ref.shape[1], LRU_WIDTH

    @pl.when(pl.program_id(1) == 0)
    def _():
        xs_ref[0:SUBLANES, :] = jnp.zeros((SUBLANES, W), F32)
        h_ref[...] = jnp.zeros_like(h_ref)

    h = _rms_mod(x_ref[0], g_ref[...], mod_ref[0, 1:2, :], mod_ref[0, 0:1, :]).astype(BF16)
    segs, c0 = [], LRU_COLS
    for ref, n in ((fox_ref, FOX_COLS), (gla_ref, GLA_COLS), (small_ref, SMALL_COLS)):
        segs.append((ref, c0, c0 + n))
        c0 += n

    def project(g0):
        g1 = g0 + MXU_COLS
        y = _dot(h, w_ref[:, g0:g1])
        for ref, s0, s1 in segs:
            lo, hi = max(g0, s0), min(g1, s1)
            if lo < hi:
                ref[0, :, lo - s0:hi - s0] = y[:, lo - g0:hi - g0].astype(ref.dtype)

    y0 = _dot(h, w_ref[:, 0:LRU_COLS])
    xs_ref[SUBLANES:SUBLANES + tm, :] = y0[:, 0:W]
    pieces = [(s, r0) for s in range(W // LANES) for r0 in range(0, tm, LRU_ROWS)]
    gates = [_lru_gates(y0[r0:r0 + LRU_ROWS, s * LANES:(s + 1) * LANES], r0, slice(s * LANES, (s + 1) * LANES),
                        cw_ref, cb_ref, wr_ref, wi_ref, xs_ref) for s, r0 in pieces]
    xs_ref[0:SUBLANES, :] = xs_ref[tm:tm + SUBLANES, :]
    chunks = list(range(LRU_COLS, NP_COLS, MXU_COLS))
    carry = None
    for (s, r0), g in zip(pieces, gates):
        sl = slice(s * LANES, (s + 1) * LANES)
        rows = slice(r0, r0 + LRU_ROWS)
        if chunks:
            project(chunks.pop(0))
        if r0 == 0:
            carry = h_ref[0:1, sl]
        out, carry = _lru_finish(*g, y0[rows, W + s * LANES:W + (s + 1) * LANES], sl, carry,
                                 br_ref, bi_ref, lam_ref)
        oa_ref[0, rows, sl] = out.astype(oa_ref.dtype)
        if r0 + LRU_ROWS == tm:
            h_ref[:, sl] = jnp.broadcast_to(carry, (h_ref.shape[0], LANES))
    for g0 in chunks:
        project(g0)


def _in_proj(x, mod, gain, w, lru_p, layer, tm):
    B, S, D = x.shape
    W = LRU_WIDTH
    tok = lambda n: pl.BlockSpec((1, tm, n), lambda b, i: (b, i, 0))
    return pl.pallas_call(
        _in_proj_kernel,
        grid=(B, S // tm),
        in_specs=[
            tok(D),
            pl.BlockSpec((1, N_MOD, D), lambda b, i: (b, 0, 0)),
            _const_spec((1, D)),
            _layer_weight_spec(w.shape, layer),
            _const_spec((CONV_WIDTH, W)), _const_spec((1, W)),
            _const_spec((W, W)), _const_spec((1, W)),
            _const_spec((W, W)), _const_spec((1, W)),
            _const_spec((1, W)),
        ],
        out_specs=[tok(W), tok(FOX_COLS), tok(GLA_COLS), tok(SMALL_COLS)],
        out_shape=[
            jax.ShapeDtypeStruct((B, S, W), BF16),
            jax.ShapeDtypeStruct((B, S, FOX_COLS), BF16),
            jax.ShapeDtypeStruct((B, S, GLA_COLS), BF16),
            jax.ShapeDtypeStruct((B, S, SMALL_COLS), F32),
        ],
        scratch_shapes=[pltpu.VMEM((tm + SUBLANES, W), F32), pltpu.VMEM((SUBLANES, W), F32)],
        compiler_params=_cparams(("arbitrary", "arbitrary")),
        name="in_proj_lru",
    )(x, mod, gain, w, *lru_p)


def _scan_add(v):
    n = v.shape[0]
    row = lax.broadcasted_iota(jnp.int32, v.shape, 0)
    d = 1
    while d < n:
        v = v + jnp.where(row >= d, pltpu.roll(v, d, axis=0), 0.0)
        d *= 2
    return v


def _scan_linear(a, u, h0):
    n, w = a.shape
    g = n // SUBLANES
    a = a.reshape(g, SUBLANES, w)
    u = u.reshape(g, SUBLANES, w)
    sub = lax.broadcasted_iota(jnp.int32, a.shape, 1)
    d = 1
    while d < SUBLANES:
        m = sub >= d
        u = u + jnp.where(m, a * pltpu.roll(u, d, axis=1), 0.0)
        a = jnp.where(m, a * pltpu.roll(a, d, axis=1), a)
        d *= 2
    hs, h = [], h0
    for v in range(g):
        hv = a[v] * h + u[v]
        h = hv[SUBLANES - 1:SUBLANES, :]
        hs.append(hv)
    return jnp.concatenate(hs, axis=0), h


def _lru_gates(xa, r0, sl, cw_ref, cb_ref, wr_ref, wi_ref, xs_ref):
    n = xa.shape[0]
    xc = xa * cw_ref[CONV_WIDTH - 1:CONV_WIDTH, sl] + cb_ref[:, sl]
    for k in range(1, CONV_WIDTH):
        j = CONV_WIDTH - 1 - k
        xc = xc + xs_ref[pl.ds(SUBLANES - k + r0, n), sl] * cw_ref[j:j + 1, sl]
    xcb = xc.astype(BF16)
    return xc, _dot(xcb, wr_ref[sl, sl]), _dot(xcb, wi_ref[sl, sl])


def _lru_finish(xc, r_pre, i_pre, ya, sl, h0, br_ref, bi_ref, lam_ref):
    r = _sigmoid(r_pre + br_ref[:, sl])
    i = _sigmoid(i_pre + bi_ref[:, sl])
    a = jnp.exp2(r * ((-LRU_C * LOG2E) * _softplus(-lam_ref[:, sl])))
    mult = jnp.sqrt(1.0 - a * a)
    u = mult * (i * xc)
    h, h_last = _scan_linear(a, u, h0)
    return h * _gelu_tanh(ya), h_last


def _fox_prep_body(fox_ref, small_ref, bf_ref, qg_ref, kg_ref, ones_ref,
                   rq_ref, rk_ref, kaug_ref, qaugt_ref, vt_ref, cum_ref):
    tm = fox_ref.shape[1]
    W = FOX_WIDTH
    q = fox_ref[0, :, 0:W].astype(F32)
    k = fox_ref[0, :, W:2 * W].astype(F32)
    v = fox_ref[0, :, 2 * W:3 * W].astype(F32)

    def head_norm(t, gain):
        ms = _dot((t * t).astype(BF16), ones_ref[...])
        return t * lax.rsqrt(ms + EPS) * gain

    qn = head_norm(q, qg_ref[...])
    kn = head_norm(k, kg_ref[...])

    lf = _log_sigmoid(small_ref[0] + bf_ref[...])
    cum = _scan_add(lf) + cum_ref[0:1, :]
    cum_ref[...] = jnp.broadcast_to(cum[tm - 1:tm, :], cum_ref.shape)

    cs = cum * LOG2E
    c1 = cs.astype(BF16).astype(F32)
    r1 = cs - c1
    c2 = r1.astype(BF16).astype(F32)
    c3 = (r1 - c2).astype(BF16).astype(F32)
    lane = lax.broadcasted_iota(jnp.int32, cum.shape, 1)
    H = FOX_HEADS
    packed = jnp.where(lane < H, c1, 0.0)
    packed = jnp.where((lane >= H) & (lane < 2 * H), pltpu.roll(c2, H, axis=1), packed)
    packed = jnp.where((lane >= 2 * H) & (lane < 3 * H), pltpu.roll(c3, 2 * H, axis=1), packed)
    packed = jnp.where(lane == ONE_LANE, 1.0, packed).astype(BF16)

    kbias = _dot(packed, rk_ref[...])
    qbias = _dot(packed, rq_ref[...])
    feat = lane < HEAD_DIM

    def head_aug(t, bias, h):
        src = t[:, (h // 2) * LANES:(h // 2 + 1) * LANES]
        if h % 2:
            src = pltpu.roll(src, HEAD_DIM, axis=1)
        return jnp.where(feat, src, bias[:, h * AUG:(h + 1) * AUG])

    vt = v.T
    tk = vt_ref.shape[4]
    for h in range(H):
        kaug_ref[0, h] = head_aug(kn, kbias, h).astype(BF16)
        qaugt_ref[0, h] = head_aug(qn, qbias, h).T.astype(BF16)
        for cb in range(tm // tk):
            vt_ref[0, h, cb, 0:HEAD_DIM, :] = (
                vt[h * HEAD_DIM:(h + 1) * HEAD_DIM, cb * tk:(cb + 1) * tk].astype(BF16))
            vt_ref[0, h, cb, HEAD_DIM:V_ROWS, :] = jnp.ones((V_ROWS - HEAD_DIM, tk), BF16)


def _fox_attn_kernel(k_ref, qt_ref, vt_ref, o_ref, acc_ref, sa_ref, sb_ref, *, tk):
    qi = pl.program_id(1)
    units = [(h, c) for h in range(FOX_HEADS) for c in range(NQ_SUB)]
    acc_ref[...] = jnp.zeros_like(acc_ref)

    def qk(h, c, j):
        start = pl.multiple_of(j * tk, tk)
        return _dot(k_ref[0, h, pl.ds(start, tk), :], qt_ref[0, h, :, c * tk:(c + 1) * tk])

    def update(h, c, j, s, m_old, m_new):
        p = jnp.exp2(s - m_new).astype(BF16)
        acc_ref[h, c] = jnp.exp2(m_old - m_new) * acc_ref[h, c] + _dot(vt_ref[0, h, j], p)

    def step(j, cur_ref, c_cur, nxt_ref, ms):
        new_ms, c_nxt = [], []
        for i, (h, c) in enumerate(units):
            s_n = qk(h, c, j + 1)
            nxt_ref[h, c] = s_n
            c_nxt.append(jnp.max(s_n, axis=0, keepdims=True))
            m_new = jnp.maximum(ms[i], c_cur[i])
            new_ms.append(m_new)
            update(h, c, j, cur_ref[h, c], ms[i], m_new)
        return tuple(new_ms), tuple(c_nxt)

    def pair(t, carry):
        ms, ca = carry
        ms, cb = step(2 * t, sa_ref, ca, sb_ref, ms)
        ms, ca = step(2 * t + 1, sb_ref, cb, sa_ref, ms)
        return ms, ca

    c0 = []
    for h, c in units:
        s = qk(h, c, 0)
        sa_ref[h, c] = s
        c0.append(jnp.max(s, axis=0, keepdims=True))
    carry = (tuple(jnp.full((1, tk), NEG_BIG, F32) for _ in units), tuple(c0))
    n_quads = lax.div(qi, 2)
    carry = lax.fori_loop(0, n_quads, lambda u, cr: pair(2 * u + 1, pair(2 * u, cr)), carry)
    carry = lax.fori_loop(2 * n_quads, qi, pair, carry)

    diag = (lax.broadcasted_iota(jnp.int32, (tk, tk), 0)
            <= lax.broadcasted_iota(jnp.int32, (tk, tk), 1))
    ms, cmax = list(carry[0]), list(carry[1])
    bufs = (sa_ref, sb_ref)
    for d in range(NQ_SUB):
        j = NQ_SUB * qi + d
        cur_ref, nxt_ref = bufs[d % 2], bufs[(d + 1) % 2]
        for i, (h, c) in enumerate(units):
            if c < d:
                continue
            if c > d:
                s_n = qk(h, c, j + 1)
                nxt_ref[h, c] = s_n
                c_next = jnp.max(s_n, axis=0, keepdims=True)
            if c == d:
                s = jnp.where(diag, cur_ref[h, c], NEG_BIG)
                m_new = jnp.maximum(ms[i], jnp.max(s, axis=0, keepdims=True))
            else:
                s = cur_ref[h, c]
                m_new = jnp.maximum(ms[i], cmax[i])
                cmax[i] = c_next
            update(h, c, j, s, ms[i], m_new)
            ms[i] = m_new

    for c in range(NQ_SUB):
        out = jnp.concatenate(
            [acc_ref[h, c, 0:HEAD_DIM, :] * (1.0 / acc_ref[h, c, HEAD_DIM:HEAD_DIM + 1, :])
             for h in range(FOX_HEADS)], axis=0)
        o_ref[0, c * tk:(c + 1) * tk, :] = out.T.astype(o_ref.dtype)


def _fox_attention(kaug, qaugt, vt, tk):
    B, H, S, _ = kaug.shape
    tq = NQ_SUB * tk
    assert S % tq == 0
    return pl.pallas_call(
        functools.partial(_fox_attn_kernel, tk=tk),
        grid=(B, S // tq),
        in_specs=[
            pl.BlockSpec((1, H, S, AUG), lambda b, i: (b, 0, 0, 0)),
            pl.BlockSpec((1, H, AUG, tq), lambda b, i: (b, 0, 0, i)),
            pl.BlockSpec((1, H, S // tk, V_ROWS, tk), lambda b, i: (b, 0, 0, 0, 0)),
        ],
        out_specs=pl.BlockSpec((1, tq, H * HEAD_DIM), lambda b, i: (b, i, 0)),
        out_shape=jax.ShapeDtypeStruct((B, S, H * HEAD_DIM), BF16),
        scratch_shapes=[pltpu.VMEM((H, NQ_SUB, V_ROWS, tk), F32), pltpu.VMEM((H, NQ_SUB, tk, tk), F32),
                        pltpu.VMEM((H, NQ_SUB, tk, tk), F32)],
        compiler_params=_cparams(("arbitrary", "arbitrary")),
        name="fox_attention",
    )(kaug, qaugt, vt)


GLA_ROWS = 256


def _gla_body(gla_ref, small_ref, wa_ref, ba_ref, gain_ref, tri_ref, hmask_ref, bdmask_ref,
              ones_ref, o_ref, st_ref):
    ts = o_ref.shape[1]
    KW, C, R = GLA_KW, GLA_CHUNK, GLA_ROWS
    q = gla_ref[0, :, 0:KW].astype(F32) * (GLA_DK ** -0.5)
    k = gla_ref[0, :, KW:2 * KW].astype(F32)
    v = gla_ref[0, :, 2 * KW:3 * KW]
    low = small_ref[0].astype(BF16)
    la = _log_sigmoid(_dot(low, wa_ref[...]) + ba_ref[...]) * (LOG2E / GLA_TAU)
    la_hi = la.astype(BF16)
    la_lo = (la - la_hi.astype(F32)).astype(BF16)
    tri = tri_ref[...]
    bcum = jnp.concatenate(
        [_dot(tri, la_hi[g0:g0 + R]) + _dot(tri, la_lo[g0:g0 + R]) for g0 in range(0, ts, R)], axis=0)
    ends = [bcum[c0 + C - 1:c0 + C, :] for c0 in range(0, ts, C)]
    b_last = jnp.concatenate([jnp.broadcast_to(e, (C, KW)) for e in ends], axis=0)
    q_dec = (q * jnp.exp2(bcum)).astype(BF16)
    k_dec = (k * jnp.exp2(-bcum)).astype(BF16)
    k_end = (k * jnp.exp2(b_last - bcum)).astype(BF16)

    tri_f = tri.astype(F32)
    o_groups = []
    for g0 in range(0, ts, R):
        rows = slice(g0, g0 + R)
        o_acc = jnp.zeros((R, KW), F32)
        for h in range(GLA_HEADS):
            hm = hmask_ref[h:h + 1, :].astype(BF16)
            att = lax.dot_general(q_dec[rows] * hm, k_dec[rows], (((1,), (1,)), ((), ())),
                                  preferred_element_type=F32)
            o_acc = o_acc + _dot((att * tri_f).astype(BF16), v[rows] * hm)
        o_groups.append(o_acc)

    o_parts = []
    for ci, c0 in enumerate(range(0, ts, C)):
        cs = slice(c0, c0 + C)
        st = st_ref[...]
        o_parts.append(lax.dot_general(q_dec[cs], st.astype(BF16), (((1,), (1,)), ((), ())),
                                       preferred_element_type=F32))
        kv_t = lax.dot_general(v[cs], k_end[cs], (((0,), (0,)), ((), ())),
                               preferred_element_type=F32)
        st_ref[...] = st * jnp.exp2(ends[ci]) + kv_t * bdmask_ref[...]
    o = jnp.concatenate(o_groups, axis=0) + jnp.concatenate(o_parts, axis=0)

    ms = _dot((o * o).astype(BF16), ones_ref[...])
    o = o * lax.rsqrt(ms + EPS) * gain_ref[...]
    g = gla_ref[0, :, 3 * KW:4 * KW].astype(F32)
    o_ref[0] = (o * (g * _sigmoid(g))).astype(o_ref.dtype)


def _fox_gla_kernel(fox_ref, gla_ref, small_ref, bf_ref, qg_ref, kg_ref, fones_ref, rq_ref, rk_ref,
                    wa_ref, ba_ref, gain_ref, tri_ref, hmask_ref, bdmask_ref, gones_ref,
                    kaug_ref, qaugt_ref, vt_ref, oc_ref, cum_ref, st_ref):
    @pl.when(pl.program_id(1) == 0)
    def _():
        cum_ref[...] = jnp.zeros_like(cum_ref)
        st_ref[...] = jnp.zeros_like(st_ref)

    _fox_prep_body(fox_ref, small_ref, bf_ref, qg_ref, kg_ref, fones_ref, rq_ref, rk_ref,
                   kaug_ref, qaugt_ref, vt_ref, cum_ref)
    _gla_body(gla_ref, small_ref, wa_ref, ba_ref, gain_ref, tri_ref, hmask_ref, bdmask_ref,
              gones_ref, oc_ref, st_ref)


def _fox_prep_gla(fox, gla, small, fox_p, gla_p, tm, tk):
    B, S, _ = fox.shape
    H, W, KW = FOX_HEADS, FOX_WIDTH, GLA_KW
    tok = lambda n: pl.BlockSpec((1, tm, n), lambda b, i: (b, i, 0))
    fox_specs = [_const_spec((1, LANES)), _const_spec((1, W)), _const_spec((1, W)), _const_spec((W, W)),
                 _const_spec((LANES, H * AUG)), _const_spec((LANES, H * AUG))]
    gla_specs = [_const_spec((LANES, KW)), _const_spec((1, KW)), _const_spec((1, KW)),
                 _const_spec((GLA_ROWS, GLA_ROWS)), _const_spec((GLA_HEADS, KW)),
                 _const_spec((KW, KW)), _const_spec((KW, KW))]
    return pl.pallas_call(
        _fox_gla_kernel,
        grid=(B, S // tm),
        in_specs=[tok(FOX_COLS), tok(GLA_COLS), tok(SMALL_COLS)] + fox_specs + gla_specs,
        out_specs=[
            pl.BlockSpec((1, H, tm, AUG), lambda b, i: (b, 0, i, 0)),
            pl.BlockSpec((1, H, AUG, tm), lambda b, i: (b, 0, 0, i)),
            pl.BlockSpec((1, H, tm // tk, V_ROWS, tk), lambda b, i: (b, 0, i, 0, 0)),
            tok(KW),
        ],
        out_shape=[
            jax.ShapeDtypeStruct((B, H, S, AUG), BF16),
            jax.ShapeDtypeStruct((B, H, AUG, S), BF16),
            jax.ShapeDtypeStruct((B, H, S // tk, V_ROWS, tk), BF16),
            jax.ShapeDtypeStruct((B, S, KW), BF16),
        ],
        scratch_shapes=[pltpu.VMEM((SUBLANES, LANES), F32), pltpu.VMEM((KW, KW), F32)],
        compiler_params=_cparams(("arbitrary", "arbitrary")),
        name="fox_prep_gla",
    )(fox, gla, small, *fox_p, *gla_p)


def _ffn_chunks(d_ff):
    chunks, c0 = [], 0
    while c0 < d_ff:
        n = min(512, d_ff - c0)
        chunks.append((c0, n))
        c0 += n
    return chunks


def _out_ffn_kernel(x_ref, a_ref, b_ref, c_ref, mod_ref, g_ref, wo_ref, wgu_ref, wd_ref, o_ref,
                    acc_ref):
    d_ff = wd_ref.shape[0]
    mix = _dot(jnp.concatenate([a_ref[0], b_ref[0], c_ref[0]], axis=1), wo_ref[...])
    x1 = x_ref[0] + mod_ref[0, 2:3, :] * mix
    h = _rms_mod(x1, g_ref[...], mod_ref[0, 4:5, :], mod_ref[0, 3:4, :]).astype(BF16)
    for idx, (c0, n) in enumerate(_ffn_chunks(d_ff)):
        gt = _dot(h, wgu_ref[:, c0:c0 + n])
        up = _dot(h, wgu_ref[:, d_ff + c0:d_ff + c0 + n])
        act = (gt * _sigmoid(gt) * up).astype(BF16)
        y = _dot(act, wd_ref[c0:c0 + n, :])
        if idx == 0:
            acc_ref[...] = y
        else:
            acc_ref[...] += y
    o_ref[0] = x1 + mod_ref[0, 5:6, :] * acc_ref[...]


def _out_ffn(x, a, b, c, mod, gain, wo, wgu, wd, layer, tm):
    B, S, D = x.shape
    tok = lambda n: pl.BlockSpec((1, tm, n), lambda bb, i: (bb, i, 0))
    return pl.pallas_call(
        _out_ffn_kernel,
        grid=(B, S // tm),
        in_specs=[
            tok(D), tok(a.shape[2]), tok(b.shape[2]), tok(c.shape[2]),
            pl.BlockSpec((1, N_MOD, D), lambda bb, i: (bb, 0, 0)),
            _const_spec((1, D)),
            _layer_weight_spec(wo.shape, layer), _layer_weight_spec(wgu.shape, layer),
            _layer_weight_spec(wd.shape, layer),
        ],
        out_specs=tok(D),
        out_shape=jax.ShapeDtypeStruct((B, S, D), F32),
        scratch_shapes=[pltpu.VMEM((tm, D), F32)],
        compiler_params=_cparams(("arbitrary", "arbitrary")),
        name="out_proj_ffn",
    )(x, a, b, c, mod, gain, wo, wgu, wd)


def _block_diag(w):
    L, n, d, e = w.shape
    eye = jnp.eye(n, dtype=w.dtype)
    return jnp.einsum("lnde,nm->lndme", w, eye).reshape(L, n * d, n * e)


def _regroup_w_in(w_in):
    L, D, _ = w_in.shape
    sizes = (LRU_WIDTH, LRU_WIDTH, FOX_WIDTH, FOX_WIDTH, FOX_WIDTH, FOX_HEADS,
             GLA_KW, GLA_KW, GLA_VW, GLA_RANK, GLA_VW)
    offs = np.concatenate([[0], np.cumsum(sizes)])
    seg = lambda i: w_in[:, :, offs[i]:offs[i + 1]]
    z = lambda n: jnp.zeros((L, D, n), w_in.dtype)
    cols = [seg(0), seg(1), seg(2), seg(3), seg(4), seg(6), seg(7), seg(8), seg(10),
            seg(5), z(GLOW_LANE0 - FOX_HEADS), seg(9), z(SMALL_COLS - GLOW_LANE0 - GLA_RANK)]
    return jnp.concatenate(cols, axis=-1).astype(BF16)


def _head_ones(n_heads, dim):
    assert dim & (dim - 1) == 0
    return jnp.asarray(np.kron(np.eye(n_heads), np.full((dim, dim), 1.0 / dim)), BF16)


def _fox_constants():
    H, W = FOX_HEADS, FOX_WIDTH
    rq = np.zeros((LANES, H * AUG), np.float32)
    rk = np.zeros((LANES, H * AUG), np.float32)
    for h in range(H):
        for s in range(N_SPLIT):
            rq[s * H + h, h * AUG + HEAD_DIM + s] = 1.0
            rq[ONE_LANE, h * AUG + HEAD_DIM + N_SPLIT + s] = 1.0
            rk[ONE_LANE, h * AUG + HEAD_DIM + s] = 1.0
            rk[s * H + h, h * AUG + HEAD_DIM + N_SPLIT + s] = -1.0
    return jnp.asarray(rq, BF16), jnp.asarray(rk, BF16)


def _gla_constants():
    R, C, KW = GLA_ROWS, GLA_CHUNK, GLA_KW
    i = np.arange(R)
    tri = ((i[:, None] >= i[None, :]) & (i[:, None] // C == i[None, :] // C)).astype(np.float32)
    lane = np.arange(KW)
    hmask = (lane[None, :] // GLA_DK == np.arange(GLA_HEADS)[:, None]).astype(np.float32)
    bdmask = (lane[:, None] // GLA_DK == lane[None, :] // GLA_DK).astype(np.float32)
    return jnp.asarray(tri, BF16), jnp.asarray(hmask, F32), jnp.asarray(bdmask, F32)


def _pick_tile(n, want):
    t = min(n, want)
    while n % t:
        t //= 2
    return t


def kernel(x, c, norm1_gain, norm2_gain, w_mod, b_mod, w_in, conv_w, conv_b, lru_w_r, lru_b_r,
           lru_w_i, lru_b_i, lru_lambda, fox_b_f, fox_q_gain, fox_k_gain, gla_w_alpha, gla_b_alpha,
           gla_out_gain, w_out, ffn_w_gate_up, ffn_w_down):
    B, S, D = x.shape
    L = w_in.shape[0]
    tm = _pick_tile(S, 512)
    tp = _pick_tile(S, 1024)
    tq = _pick_tile(S, 256)

    mod = _modulation(c, w_mod, b_mod)
    w_in_p = _regroup_w_in(w_in)
    wr_bd = _block_diag(lru_w_r).astype(BF16)
    wi_bd = _block_diag(lru_w_i).astype(BF16)
    w_out_b = w_out.astype(BF16)
    wgu_b = ffn_w_gate_up.astype(BF16)
    wd_b = ffn_w_down.astype(BF16)
    row = lambda a: a.reshape(L, 1, a.shape[-1])
    bf_pad = jnp.pad(fox_b_f, ((0, 0), (0, LANES - FOX_HEADS))).reshape(L, 1, LANES)
    qg = jnp.tile(fox_q_gain * (HEAD_DIM ** -0.5 * LOG2E), (1, FOX_HEADS)).reshape(L, 1, FOX_WIDTH)
    kg = jnp.tile(fox_k_gain, (1, FOX_HEADS)).reshape(L, 1, FOX_WIDTH)
    wa_pad = jnp.pad(gla_w_alpha, ((0, 0), (GLOW_LANE0, LANES - GLOW_LANE0 - GLA_RANK), (0, 0))).astype(BF16)
    og = jnp.tile(gla_out_gain, (1, GLA_HEADS)).reshape(L, 1, GLA_VW)
    fox_ones = _head_ones(FOX_HEADS, HEAD_DIM)
    gla_ones = _head_ones(GLA_HEADS, GLA_DK)
    rq, rk = _fox_constants()
    tri, hmask, bdmask = _gla_constants()

    for l in range(L):
        lru_p = (conv_w[l], row(conv_b)[l], wr_bd[l], row(lru_b_r)[l], wi_bd[l], row(lru_b_i)[l],
                 row(lru_lambda)[l])
        out_a, fox, gla, small = _in_proj(x, mod[l], row(norm1_gain)[l], w_in_p, lru_p, l, tp)
        fox_p = (bf_pad[l], qg[l], kg[l], fox_ones, rq, rk)
        gla_p = (wa_pad[l], row(gla_b_alpha)[l], og[l], tri, hmask, bdmask, gla_ones)
        kaug, qaugt, vt, out_c = _fox_prep_gla(fox, gla, small, fox_p, gla_p, tp, tq)
        out_b = _fox_attention(kaug, qaugt, vt, tq)
        x = _out_ffn(x, out_a, out_b, out_c, mod[l], row(norm2_gain)[l], w_out_b, wgu_b, wd_b, l, tm)
    return x
```

```python
import functools

import numpy as np
import jax
import jax.numpy as jnp
from jax import lax
from jax.experimental import pallas as pl
from jax.experimental.pallas import tpu as pltpu

F32 = jnp.float32
BF16 = jnp.bfloat16

EPS = 1e-6
HEAD_DIM = 64
LRU_WIDTH = 384
LRU_BLOCKS = 6
CONV_WIDTH = 4
LRU_C = 8.0
FOX_HEADS = 6
FOX_WIDTH = FOX_HEADS * HEAD_DIM
GLA_HEADS = 4
GLA_DK = 64
GLA_KW = GLA_HEADS * GLA_DK
GLA_VW = GLA_KW
GLA_RANK = 16
GLA_TAU = 16.0
GLA_CHUNK = 64
N_MOD = 6

LANES = 128
SUBLANES = 8
VMEM_LIMIT_BYTES = 56 * 1024 * 1024

LRU_COLS = 2 * LRU_WIDTH
FOX_COLS = 3 * FOX_WIDTH
GLA_COLS = 3 * GLA_KW + GLA_VW
SMALL_COLS = LANES
GLOW_LANE0 = 8
NP_COLS = LRU_COLS + FOX_COLS + GLA_COLS + SMALL_COLS
MXU_COLS = 256
LRU_ROWS = 128

AUG = LANES
N_SPLIT = 3
ONE_LANE = N_SPLIT * FOX_HEADS
NEG_BIG = -1e30
V_ROWS = HEAD_DIM + 16
NQ_SUB = 2
LOG2E = float(np.log2(np.e))


def _sigmoid(x):
    return 0.5 * jnp.tanh(0.5 * x) + 0.5


def _log_sigmoid(x):
    return jnp.minimum(x, 0.0) - jnp.log(1.0 + jnp.exp(-jnp.abs(x)))


def _softplus(x):
    return jnp.maximum(x, 0.0) + jnp.log(1.0 + jnp.exp(-jnp.abs(x)))


def _gelu_tanh(x):
    c = np.float32(np.sqrt(2.0 / np.pi))
    return 0.5 * x * (1.0 + jnp.tanh(c * (x + 0.044715 * (x * x * x))))


def _dot(a, b):
    return jnp.dot(a, b, preferred_element_type=F32)


def _cparams(sem):
    return pltpu.CompilerParams(dimension_semantics=sem, vmem_limit_bytes=VMEM_LIMIT_BYTES)


def _const_spec(shape):
    nd = len(shape)
    return pl.BlockSpec(shape, lambda *_: (0,) * nd)


def _layer_spec(stacked_shape, layer):
    tail = tuple(stacked_shape[1:])
    return pl.BlockSpec((None,) + tail, lambda *_: (layer,) + (0,) * len(tail))


def _layer_weight_spec(stacked_shape, layer):
    tail = tuple(stacked_shape[1:])
    return pl.BlockSpec((None,) + tail, lambda *_: (layer,) + (0,) * len(tail),
                        pipeline_mode=pl.Buffered(1))


def _mod_spec(mod_shape, layer):
    return pl.BlockSpec((None, 1) + tuple(mod_shape[2:]), lambda b, i: (layer, b, 0, 0))


def _mod_kernel(c_ref, w_ref, b_ref, o_ref):
    c = c_ref[...]
    ca = (c * _sigmoid(c)).astype(BF16)
    o_ref[0] = _dot(ca, w_ref[0].astype(BF16)) + b_ref[0]


def _modulation(c, w_mod, b_mod):
    L, D, N = w_mod.shape
    B = c.shape[0]
    bp = -(-B // SUBLANES) * SUBLANES
    cp = jnp.pad(c, ((0, bp - B), (0, 0)))
    tn = 1024
    out = pl.pallas_call(
        _mod_kernel,
        grid=(L, N // tn),
        in_specs=[
            pl.BlockSpec((bp, D), lambda l, n: (0, 0)),
            pl.BlockSpec((1, D, tn), lambda l, n: (l, 0, n)),
            pl.BlockSpec((1, 1, tn), lambda l, n: (l, 0, n)),
        ],
        out_specs=pl.BlockSpec((1, bp, tn), lambda l, n: (l, 0, n)),
        out_shape=jax.ShapeDtypeStruct((L, bp, N), F32),
        compiler_params=_cparams(("arbitrary", "arbitrary")),
        name="adaln_mod",
    )(cp, w_mod, b_mod.reshape(L, 1, N))
    return out.reshape(L, bp, N_MOD, D)


def _rms_mod(x, gain, scale, shift):
    ms = jnp.mean(x * x, axis=-1, keepdims=True)
    y = x * lax.rsqrt(ms + EPS) * gain
    return y * (1.0 + scale) + shift


def _in_proj_kernel(x_ref, mod_ref, g_ref, w_ref, cw_ref, cb_ref, wr_ref, br_ref, wi_ref, bi_ref,
                    lam_ref, oa_ref, fox_ref, gla_ref, small_ref, xs_ref, h_ref):
    tm, W = x_ref.shape[1], LRU_WIDTH

    @pl.when(pl.program_id(1) == 0)
    def _():
        xs_ref[0:SUBLANES, :] = jnp.zeros((SUBLANES, W), F32)
        h_ref[...] = jnp.zeros_like(h_ref)

    h = _rms_mod(x_ref[0], g_ref[...], mod_ref[0, 1:2, :], mod_ref[0, 0:1, :]).astype(BF16)
    segs, c0 = [], LRU_COLS
    for ref, n in ((fox_ref, FOX_COLS), (gla_ref, GLA_COLS), (small_ref, SMALL_COLS)):
        segs.append((ref, c0, c0 + n))
        c0 += n

    def project(g0):
        g1 = g0 + MXU_COLS
        y = _dot(h, w_ref[:, g0:g1])
        for ref, s0, s1 in segs:
            lo, hi = max(g0, s0), min(g1, s1)
            if lo < hi:
                ref[0, :, lo - s0:hi - s0] = y[:, lo - g0:hi - g0].astype(ref.dtype)

    y0 = _dot(h, w_ref[:, 0:LRU_COLS])
    xs_ref[SUBLANES:SUBLANES + tm, :] = y0[:, 0:W]
    pieces = [(s, r0) for s in range(W // LANES) for r0 in range(0, tm, LRU_ROWS)]
    gates = [_lru_gates(y0[r0:r0 + LRU_ROWS, s * LANES:(s + 1) * LANES], r0, slice(s * LANES, (s + 1) * LANES),
                        cw_ref, cb_ref, wr_ref, wi_ref, xs_ref) for s, r0 in pieces]
    xs_ref[0:SUBLANES, :] = xs_ref[tm:tm + SUBLANES, :]
    chunks = list(range(LRU_COLS, NP_COLS, MXU_COLS))
    carry = None
    for (s, r0), g in zip(pieces, gates):
        sl = slice(s * LANES, (s + 1) * LANES)
        rows = slice(r0, r0 + LRU_ROWS)
        if chunks:
            project(chunks.pop(0))
        if r0 == 0:
            carry = h_ref[0:1, sl]
        out, carry = _lru_finish(*g, y0[rows, W + s * LANES:W + (s + 1) * LANES], sl, carry,
                                 br_ref, bi_ref, lam_ref)
        oa_ref[0, rows, sl] = out.astype(oa_ref.dtype)
        if r0 + LRU_ROWS == tm:
            h_ref[:, sl] = jnp.broadcast_to(carry, (h_ref.shape[0], LANES))
    for g0 in chunks:
        project(g0)


def _in_proj(x, mod, gain, w, lru_p, layer, tm):
    B, S, D = x.shape
    W = LRU_WIDTH
    tok = lambda n: pl.BlockSpec((1, tm, n), lambda b, i: (b, i, 0))
    return pl.pallas_call(
        _in_proj_kernel,
        grid=(B, S // tm),
        in_specs=[tok(D), _mod_spec(mod.shape, layer), _layer_spec(gain.shape, layer),
                  _layer_weight_spec(w.shape, layer)] + [_layer_spec(p.shape, layer) for p in lru_p],
        out_specs=[tok(W), tok(FOX_COLS), tok(GLA_COLS), tok(SMALL_COLS)],
        out_shape=[
            jax.ShapeDtypeStruct((B, S, W), BF16),
            jax.ShapeDtypeStruct((B, S, FOX_COLS), BF16),
            jax.ShapeDtypeStruct((B, S, GLA_COLS), BF16),
            jax.ShapeDtypeStruct((B, S, SMALL_COLS), F32),
        ],
        scratch_shapes=[pltpu.VMEM((tm + SUBLANES, W), F32), pltpu.VMEM((SUBLANES, W), F32)],
        compiler_params=_cparams(("arbitrary", "arbitrary")),
        name="in_proj_lru",
    )(x, mod, gain, w, *lru_p)


def _scan_add(v):
    n = v.shape[0]
    row = lax.broadcasted_iota(jnp.int32, v.shape, 0)
    d = 1
    while d < n:
        v = v + jnp.where(row >= d, pltpu.roll(v, d, axis=0), 0.0)
        d *= 2
    return v


def _scan_linear(a, u, h0):
    n, w = a.shape
    g = n // SUBLANES
    a = a.reshape(g, SUBLANES, w)
    u = u.reshape(g, SUBLANES, w)
    sub = lax.broadcasted_iota(jnp.int32, a.shape, 1)
    d = 1
    while d < SUBLANES:
        m = sub >= d
        u = u + jnp.where(m, a * pltpu.roll(u, d, axis=1), 0.0)
        a = jnp.where(m, a * pltpu.roll(a, d, axis=1), a)
        d *= 2
    hs, h = [], h0
    for v in range(g):
        hv = a[v] * h + u[v]
        h = hv[SUBLANES - 1:SUBLANES, :]
        hs.append(hv)
    return jnp.concatenate(hs, axis=0), h


def _lru_gates(xa, r0, sl, cw_ref, cb_ref, wr_ref, wi_ref, xs_ref):
    n = xa.shape[0]
    xc = xa * cw_ref[CONV_WIDTH - 1:CONV_WIDTH, sl] + cb_ref[:, sl]
    for k in range(1, CONV_WIDTH):
        j = CONV_WIDTH - 1 - k
        xc = xc + xs_ref[pl.ds(SUBLANES - k + r0, n), sl] * cw_ref[j:j + 1, sl]
    xcb = xc.astype(BF16)
    return xc, _dot(xcb, wr_ref[sl, sl]), _dot(xcb, wi_ref[sl, sl])


def _lru_finish(xc, r_pre, i_pre, ya, sl, h0, br_ref, bi_ref, lam_ref):
    r = _sigmoid(r_pre + br_ref[:, sl])
    i = _sigmoid(i_pre + bi_ref[:, sl])
    a = jnp.exp2(r * ((-LRU_C * LOG2E) * _softplus(-lam_ref[:, sl])))
    mult = jnp.sqrt(1.0 - a * a)
    u = mult * (i * xc)
    h, h_last = _scan_linear(a, u, h0)
    return h * _gelu_tanh(ya), h_last


def _fox_prep_body(fox_ref, small_ref, bf_ref, qg_ref, kg_ref, ones_ref,
                   rq_ref, rk_ref, kaug_ref, qaugt_ref, vt_ref, cum_ref):
    tm = fox_ref.shape[1]
    W = FOX_WIDTH
    q = fox_ref[0, :, 0:W].astype(F32)
    k = fox_ref[0, :, W:2 * W].astype(F32)
    v = fox_ref[0, :, 2 * W:3 * W].astype(F32)

    def head_norm(t, gain):
        ms = _dot((t * t).astype(BF16), ones_ref[...])
        return t * lax.rsqrt(ms + EPS) * gain

    qn = head_norm(q, qg_ref[...])
    kn = head_norm(k, kg_ref[...])

    lf = _log_sigmoid(small_ref[0] + bf_ref[...])
    cum = _scan_add(lf) + cum_ref[0:1, :]
    cum_ref[...] = jnp.broadcast_to(cum[tm - 1:tm, :], cum_ref.shape)

    cs = cum * LOG2E
    c1 = cs.astype(BF16).astype(F32)
    r1 = cs - c1
    c2 = r1.astype(BF16).astype(F32)
    c3 = (r1 - c2).astype(BF16).astype(F32)
    lane = lax.broadcasted_iota(jnp.int32, cum.shape, 1)
    H = FOX_HEADS
    packed = jnp.where(lane < H, c1, 0.0)
    packed = jnp.where((lane >= H) & (lane < 2 * H), pltpu.roll(c2, H, axis=1), packed)
    packed = jnp.where((lane >= 2 * H) & (lane < 3 * H), pltpu.roll(c3, 2 * H, axis=1), packed)
    packed = jnp.where(lane == ONE_LANE, 1.0, packed).astype(BF16)

    kbias = _dot(packed, rk_ref[...])
    qbias = _dot(packed, rq_ref[...])
    feat = lane < HEAD_DIM

    def head_aug(t, bias, h):
        src = t[:, (h // 2) * LANES:(h // 2 + 1) * LANES]
        if h % 2:
            src = pltpu.roll(src, HEAD_DIM, axis=1)
        return jnp.where(feat, src, bias[:, h * AUG:(h + 1) * AUG])

    vt = v.T
    tk = vt_ref.shape[4]
    for h in range(H):
        kaug_ref[0, h] = head_aug(kn, kbias, h).astype(BF16)
        qaugt_ref[0, h] = head_aug(qn, qbias, h).T.astype(BF16)
        for cb in range(tm // tk):
            vt_ref[0, h, cb, 0:HEAD_DIM, :] = (
                vt[h * HEAD_DIM:(h + 1) * HEAD_DIM, cb * tk:(cb + 1) * tk].astype(BF16))
            vt_ref[0, h, cb, HEAD_DIM:V_ROWS, :] = jnp.ones((V_ROWS - HEAD_DIM, tk), BF16)


def _fox_attn_kernel(k_ref, qt_ref, vt_ref, o_ref, acc_ref, sa_ref, sb_ref, *, tk):
    qi = pl.program_id(1)
    units = [(h, c) for h in range(FOX_HEADS) for c in range(NQ_SUB)]
    acc_ref[...] = jnp.zeros_like(acc_ref)

    def qk(h, c, j):
        start = pl.multiple_of(j * tk, tk)
        return _dot(k_ref[0, h, pl.ds(start, tk), :], qt_ref[0, h, :, c * tk:(c + 1) * tk])

    def update(h, c, j, s, m_old, m_new):
        p = jnp.exp2(s - m_new).astype(BF16)
        acc_ref[h, c] = jnp.exp2(m_old - m_new) * acc_ref[h, c] + _dot(vt_ref[0, h, j], p)

    def step(j, cur_ref, c_cur, nxt_ref, ms):
        new_ms, c_nxt = [], []
        for i, (h, c) in enumerate(units):
            s_n = qk(h, c, j + 1)
            nxt_ref[h, c] = s_n
            c_nxt.append(jnp.max(s_n, axis=0, keepdims=True))
            m_new = jnp.maximum(ms[i], c_cur[i])
            new_ms.append(m_new)
            update(h, c, j, cur_ref[h, c], ms[i], m_new)
        return tuple(new_ms), tuple(c_nxt)

    def pair(t, carry):
        ms, ca = carry
        ms, cb = step(2 * t, sa_ref, ca, sb_ref, ms)
        ms, ca = step(2 * t + 1, sb_ref, cb, sa_ref, ms)
        return ms, ca

    c0 = []
    for h, c in units:
        s = qk(h, c, 0)
        sa_ref[h, c] = s
        c0.append(jnp.max(s, axis=0, keepdims=True))
    carry = (tuple(jnp.full((1, tk), NEG_BIG, F32) for _ in units), tuple(c0))
    n_quads = lax.div(qi, 2)
    carry = lax.fori_loop(0, n_quads, lambda u, cr: pair(2 * u + 1, pair(2 * u, cr)), carry)
    carry = lax.fori_loop(2 * n_quads, qi, pair, carry)

    diag = (lax.broadcasted_iota(jnp.int32, (tk, tk), 0)
            <= lax.broadcasted_iota(jnp.int32, (tk, tk), 1))
    ms, cmax = list(carry[0]), list(carry[1])
    bufs = (sa_ref, sb_ref)
    for d in range(NQ_SUB):
        j = NQ_SUB * qi + d
        cur_ref, nxt_ref = bufs[d % 2], bufs[(d + 1) % 2]
        for i, (h, c) in enumerate(units):
            if c < d:
                continue
            if c > d:
                s_n = qk(h, c, j + 1)
                nxt_ref[h, c] = s_n
                c_next = jnp.max(s_n, axis=0, keepdims=True)
            if c == d:
                s = jnp.where(diag, cur_ref[h, c], NEG_BIG)
                m_new = jnp.maximum(ms[i], jnp.max(s, axis=0, keepdims=True))
            else:
                s = cur_ref[h, c]
                m_new = jnp.maximum(ms[i], cmax[i])
                cmax[i] = c_next
            update(h, c, j, s, ms[i], m_new)
            ms[i] = m_new

    for c in range(NQ_SUB):
        out = jnp.concatenate(
            [acc_ref[h, c, 0:HEAD_DIM, :] * (1.0 / acc_ref[h, c, HEAD_DIM:HEAD_DIM + 1, :])
             for h in range(FOX_HEADS)], axis=0)
        o_ref[0, c * tk:(c + 1) * tk, :] = out.T.astype(o_ref.dtype)


def _fox_attention(kaug, qaugt, vt, tk):
    B, H, S, _ = kaug.shape
    tq = NQ_SUB * tk
    assert S % tq == 0
    return pl.pallas_call(
        functools.partial(_fox_attn_kernel, tk=tk),
        grid=(B, S // tq),
        in_specs=[
            pl.BlockSpec((1, H, S, AUG), lambda b, i: (b, 0, 0, 0)),
            pl.BlockSpec((1, H, AUG, tq), lambda b, i: (b, 0, 0, i)),
            pl.BlockSpec((1, H, S // tk, V_ROWS, tk), lambda b, i: (b, 0, 0, 0, 0)),
        ],
        out_specs=pl.BlockSpec((1, tq, H * HEAD_DIM), lambda b, i: (b, i, 0)),
        out_shape=jax.ShapeDtypeStruct((B, S, H * HEAD_DIM), BF16),
        scratch_shapes=[pltpu.VMEM((H, NQ_SUB, V_ROWS, tk), F32), pltpu.VMEM((H, NQ_SUB, tk, tk), F32),
                        pltpu.VMEM((H, NQ_SUB, tk, tk), F32)],
        compiler_params=_cparams(("arbitrary", "arbitrary")),
        name="fox_attention",
    )(kaug, qaugt, vt)


GLA_ROWS = 256


def _gla_body(gla_ref, small_ref, wa_ref, ba_ref, gain_ref, tri_ref, hmask_ref, bdmask_ref,
              ones_ref, o_ref, st_ref):
    ts = o_ref.shape[1]
    KW, C, R = GLA_KW, GLA_CHUNK, GLA_ROWS
    q = gla_ref[0, :, 0:KW].astype(F32) * (GLA_DK ** -0.5)
    k = gla_ref[0, :, KW:2 * KW].astype(F32)
    v = gla_ref[0, :, 2 * KW:3 * KW]
    low = small_ref[0].astype(BF16)
    la = _log_sigmoid(_dot(low, wa_ref[...]) + ba_ref[...]) * (LOG2E / GLA_TAU)
    la_hi = la.astype(BF16)
    la_lo = (la - la_hi.astype(F32)).astype(BF16)
    tri = tri_ref[...]
    bcum = jnp.concatenate(
        [_dot(tri, la_hi[g0:g0 + R]) + _dot(tri, la_lo[g0:g0 + R]) for g0 in range(0, ts, R)], axis=0)
    ends = [bcum[c0 + C - 1:c0 + C, :] for c0 in range(0, ts, C)]
    b_last = jnp.concatenate([jnp.broadcast_to(e, (C, KW)) for e in ends], axis=0)
    q_dec = (q * jnp.exp2(bcum)).astype(BF16)
    k_dec = (k * jnp.exp2(-bcum)).astype(BF16)
    k_end = (k * jnp.exp2(b_last - bcum)).astype(BF16)

    tri_f = tri.astype(F32)
    o_groups = []
    for g0 in range(0, ts, R):
        rows = slice(g0, g0 + R)
        o_acc = jnp.zeros((R, KW), F32)
        for h in range(GLA_HEADS):
            hm = hmask_ref[h:h + 1, :].astype(BF16)
            att = lax.dot_general(q_dec[rows] * hm, k_dec[rows], (((1,), (1,)), ((), ())),
                                  preferred_element_type=F32)
            o_acc = o_acc + _dot((att * tri_f).astype(BF16), v[rows] * hm)
        o_groups.append(o_acc)

    o_parts = []
    for ci, c0 in enumerate(range(0, ts, C)):
        cs = slice(c0, c0 + C)
        st = st_ref[...]
        o_parts.append(lax.dot_general(q_dec[cs], st.astype(BF16), (((1,), (1,)), ((), ())),
                                       preferred_element_type=F32))
        kv_t = lax.dot_general(v[cs], k_end[cs], (((0,), (0,)), ((), ())),
                               preferred_element_type=F32)
        st_ref[...] = st * jnp.exp2(ends[ci]) + kv_t * bdmask_ref[...]
    o = jnp.concatenate(o_groups, axis=0) + jnp.concatenate(o_parts, axis=0)

    ms = _dot((o * o).astype(BF16), ones_ref[...])
    o = o * lax.rsqrt(ms + EPS) * gain_ref[...]
    g = gla_ref[0, :, 3 * KW:4 * KW].astype(F32)
    o_ref[0] = (o * (g * _sigmoid(g))).astype(o_ref.dtype)


def _fox_gla_kernel(fox_ref, gla_ref, small_ref, bf_ref, qg_ref, kg_ref, fones_ref, rq_ref, rk_ref,
                    wa_ref, ba_ref, gain_ref, tri_ref, hmask_ref, bdmask_ref, gones_ref,
                    kaug_ref, qaugt_ref, vt_ref, oc_ref, cum_ref, st_ref):
    @pl.when(pl.program_id(1) == 0)
    def _():
        cum_ref[...] = jnp.zeros_like(cum_ref)
        st_ref[...] = jnp.zeros_like(st_ref)

    _fox_prep_body(fox_ref, small_ref, bf_ref, qg_ref, kg_ref, fones_ref, rq_ref, rk_ref,
                   kaug_ref, qaugt_ref, vt_ref, cum_ref)
    _gla_body(gla_ref, small_ref, wa_ref, ba_ref, gain_ref, tri_ref, hmask_ref, bdmask_ref,
              gones_ref, oc_ref, st_ref)


def _fox_prep_gla(fox, gla, small, fox_p, gla_p, layer, tm, tk):
    B, S, _ = fox.shape
    H, KW = FOX_HEADS, GLA_KW
    tok = lambda n: pl.BlockSpec((1, tm, n), lambda b, i: (b, i, 0))
    specs = lambda stacked, consts: ([_layer_spec(p.shape, layer) for p in stacked]
                                     + [_const_spec(p.shape) for p in consts])
    fox_specs, gla_specs = specs(*fox_p), specs(*gla_p)
    fox_p, gla_p = fox_p[0] + fox_p[1], gla_p[0] + gla_p[1]
    return pl.pallas_call(
        _fox_gla_kernel,
        grid=(B, S // tm),
        in_specs=[tok(FOX_COLS), tok(GLA_COLS), tok(SMALL_COLS)] + fox_specs + gla_specs,
        out_specs=[
            pl.BlockSpec((1, H, tm, AUG), lambda b, i: (b, 0, i, 0)),
            pl.BlockSpec((1, H, AUG, tm), lambda b, i: (b, 0, 0, i)),
            pl.BlockSpec((1, H, tm // tk, V_ROWS, tk), lambda b, i: (b, 0, i, 0, 0)),
            tok(KW),
        ],
        out_shape=[
            jax.ShapeDtypeStruct((B, H, S, AUG), BF16),
            jax.ShapeDtypeStruct((B, H, AUG, S), BF16),
            jax.ShapeDtypeStruct((B, H, S // tk, V_ROWS, tk), BF16),
            jax.ShapeDtypeStruct((B, S, KW), BF16),
        ],
        scratch_shapes=[pltpu.VMEM((SUBLANES, LANES), F32), pltpu.VMEM((KW, KW), F32)],
        compiler_params=_cparams(("arbitrary", "arbitrary")),
        name="fox_prep_gla",
    )(fox, gla, small, *fox_p, *gla_p)


def _ffn_chunks(d_ff):
    chunks, c0 = [], 0
    while c0 < d_ff:
        n = min(512, d_ff - c0)
        chunks.append((c0, n))
        c0 += n
    return chunks


def _out_ffn_kernel(x_ref, a_ref, b_ref, c_ref, mod_ref, g_ref, wo_ref, wgu_ref, wd_ref, o_ref,
                    acc_ref):
    d_ff = wd_ref.shape[0]
    mix = _dot(jnp.concatenate([a_ref[0], b_ref[0], c_ref[0]], axis=1), wo_ref[...])
    x1 = x_ref[0] + mod_ref[0, 2:3, :] * mix
    h = _rms_mod(x1, g_ref[...], mod_ref[0, 4:5, :], mod_ref[0, 3:4, :]).astype(BF16)
    for idx, (c0, n) in enumerate(_ffn_chunks(d_ff)):
        gt = _dot(h, wgu_ref[:, c0:c0 + n])
        up = _dot(h, wgu_ref[:, d_ff + c0:d_ff + c0 + n])
        act = (gt * _sigmoid(gt) * up).astype(BF16)
        y = _dot(act, wd_ref[c0:c0 + n, :])
        if idx == 0:
            acc_ref[...] = y
        else:
            acc_ref[...] += y
    o_ref[0] = x1 + mod_ref[0, 5:6, :] * acc_ref[...]


def _out_ffn(x, a, b, c, mod, gain, wo, wgu, wd, layer, tm):
    B, S, D = x.shape
    tok = lambda n: pl.BlockSpec((1, tm, n), lambda bb, i: (bb, i, 0))
    return pl.pallas_call(
        _out_ffn_kernel,
        grid=(B, S // tm),
        in_specs=[
            tok(D), tok(a.shape[2]), tok(b.shape[2]), tok(c.shape[2]),
            _mod_spec(mod.shape, layer), _layer_spec(gain.shape, layer),
            _layer_weight_spec(wo.shape, layer), _layer_weight_spec(wgu.shape, layer),
            _layer_weight_spec(wd.shape, layer),
        ],
        out_specs=tok(D),
        out_shape=jax.ShapeDtypeStruct((B, S, D), F32),
        scratch_shapes=[pltpu.VMEM((tm, D), F32)],
        compiler_params=_cparams(("arbitrary", "arbitrary")),
        name="out_proj_ffn",
    )(x, a, b, c, mod, gain, wo, wgu, wd)


def _block_diag(w):
    L, n, d, e = w.shape
    eye = jnp.eye(n, dtype=w.dtype)
    return jnp.einsum("lnde,nm->lndme", w, eye).reshape(L, n * d, n * e)


def _regroup_w_in(w_in):
    L, D, _ = w_in.shape
    sizes = (LRU_WIDTH, LRU_WIDTH, FOX_WIDTH, FOX_WIDTH, FOX_WIDTH, FOX_HEADS,
             GLA_KW, GLA_KW, GLA_VW, GLA_RANK, GLA_VW)
    offs = np.concatenate([[0], np.cumsum(sizes)])
    w = w_in.astype(BF16)
    seg = lambda i: w[:, :, offs[i]:offs[i + 1]]
    z = lambda n: jnp.zeros((L, D, n), BF16)
    cols = [seg(0), seg(1), seg(2), seg(3), seg(4), seg(6), seg(7), seg(8), seg(10),
            seg(5), z(GLOW_LANE0 - FOX_HEADS), seg(9), z(SMALL_COLS - GLOW_LANE0 - GLA_RANK)]
    return jnp.concatenate(cols, axis=-1)


def _head_ones(n_heads, dim):
    assert dim & (dim - 1) == 0
    return jnp.asarray(np.kron(np.eye(n_heads), np.full((dim, dim), 1.0 / dim)), BF16)


def _fox_constants():
    H, W = FOX_HEADS, FOX_WIDTH
    rq = np.zeros((LANES, H * AUG), np.float32)
    rk = np.zeros((LANES, H * AUG), np.float32)
    for h in range(H):
        for s in range(N_SPLIT):
            rq[s * H + h, h * AUG + HEAD_DIM + s] = 1.0
            rq[ONE_LANE, h * AUG + HEAD_DIM + N_SPLIT + s] = 1.0
            rk[ONE_LANE, h * AUG + HEAD_DIM + s] = 1.0
            rk[s * H + h, h * AUG + HEAD_DIM + N_SPLIT + s] = -1.0
    return jnp.asarray(rq, BF16), jnp.asarray(rk, BF16)


def _gla_constants():
    R, C, KW = GLA_ROWS, GLA_CHUNK, GLA_KW
    i = np.arange(R)
    tri = ((i[:, None] >= i[None, :]) & (i[:, None] // C == i[None, :] // C)).astype(np.float32)
    lane = np.arange(KW)
    hmask = (lane[None, :] // GLA_DK == np.arange(GLA_HEADS)[:, None]).astype(np.float32)
    bdmask = (lane[:, None] // GLA_DK == lane[None, :] // GLA_DK).astype(np.float32)
    return jnp.asarray(tri, BF16), jnp.asarray(hmask, F32), jnp.asarray(bdmask, F32)


def _pick_tile(n, want):
    t = min(n, want)
    while n % t:
        t //= 2
    return t


def kernel(x, c, norm1_gain, norm2_gain, w_mod, b_mod, w_in, conv_w, conv_b, lru_w_r, lru_b_r,
           lru_w_i, lru_b_i, lru_lambda, fox_b_f, fox_q_gain, fox_k_gain, gla_w_alpha, gla_b_alpha,
           gla_out_gain, w_out, ffn_w_gate_up, ffn_w_down):
    B, S, D = x.shape
    L = w_in.shape[0]
    tm = _pick_tile(S, 512)
    tp = _pick_tile(S, 1024)
    tq = _pick_tile(S, 256)

    mod = _modulation(c, w_mod, b_mod)
    w_in_p = _regroup_w_in(w_in)
    wr_bd = _block_diag(lru_w_r).astype(BF16)
    wi_bd = _block_diag(lru_w_i).astype(BF16)
    w_out_b = w_out.astype(BF16)
    wgu_b = ffn_w_gate_up.astype(BF16)
    wd_b = ffn_w_down.astype(BF16)
    row = lambda a: a.reshape(L, 1, a.shape[-1])
    bf_pad = jnp.pad(fox_b_f, ((0, 0), (0, LANES - FOX_HEADS))).reshape(L, 1, LANES)
    qg = jnp.tile(fox_q_gain * (HEAD_DIM ** -0.5 * LOG2E), (1, FOX_HEADS)).reshape(L, 1, FOX_WIDTH)
    kg = jnp.tile(fox_k_gain, (1, FOX_HEADS)).reshape(L, 1, FOX_WIDTH)
    wa_pad = jnp.pad(gla_w_alpha, ((0, 0), (GLOW_LANE0, LANES - GLOW_LANE0 - GLA_RANK), (0, 0))).astype(BF16)
    og = jnp.tile(gla_out_gain, (1, GLA_HEADS)).reshape(L, 1, GLA_VW)
    fox_ones = _head_ones(FOX_HEADS, HEAD_DIM)
    gla_ones = _head_ones(GLA_HEADS, GLA_DK)
    rq, rk = _fox_constants()
    tri, hmask, bdmask = _gla_constants()

    g1, g2 = row(norm1_gain), row(norm2_gain)
    lru_p = (conv_w, row(conv_b), wr_bd, row(lru_b_r), wi_bd, row(lru_b_i), row(lru_lambda))
    fox_p = ((bf_pad, qg, kg), (fox_ones, rq, rk))
    gla_p = ((wa_pad, row(gla_b_alpha), og), (tri, hmask, bdmask, gla_ones))
    for l in range(L):
        out_a, fox, gla, small = _in_proj(x, mod, g1, w_in_p, lru_p, l, tp)
        kaug, qaugt, vt, out_c = _fox_prep_gla(fox, gla, small, fox_p, gla_p, l, tp, tq)
        out_b = _fox_attention(kaug, qaugt, vt, tq)
        x = _out_ffn(x, out_a, out_b, out_c, mod, g2, w_out_b, wgu_b, wd_b, l, tm)
    return x
```

```python
import functools

import numpy as np
import jax
import jax.numpy as jnp
from jax import lax
from jax.experimental import pallas as pl
from jax.experimental.pallas import tpu as pltpu

F32 = jnp.float32
BF16 = jnp.bfloat16

EPS = 1e-6
HEAD_DIM = 64
LRU_WIDTH = 384
LRU_BLOCKS = 6
CONV_WIDTH = 4
LRU_C = 8.0
FOX_HEADS = 6
FOX_WIDTH = FOX_HEADS * HEAD_DIM
GLA_HEADS = 4
GLA_DK = 64
GLA_KW = GLA_HEADS * GLA_DK
GLA_VW = GLA_KW
GLA_RANK = 16
GLA_TAU = 16.0
GLA_CHUNK = 64
N_MOD = 6

LANES = 128
SUBLANES = 8
VMEM_LIMIT_BYTES = 56 * 1024 * 1024

LRU_COLS = 2 * LRU_WIDTH
FOX_COLS = 3 * FOX_WIDTH
GLA_COLS = 3 * GLA_KW + GLA_VW
SMALL_COLS = LANES
GLOW_LANE0 = 8
NP_COLS = LRU_COLS + FOX_COLS + GLA_COLS + SMALL_COLS
MXU_COLS = 256
LRU_ROWS = 128

AUG = LANES
N_SPLIT = 3
ONE_LANE = N_SPLIT * FOX_HEADS
NEG_BIG = -1e30
V_ROWS = HEAD_DIM + 16
NQ_SUB = 2
LOG2E = float(np.log2(np.e))


def _sigmoid(x):
    return 0.5 * jnp.tanh(0.5 * x) + 0.5


def _log_sigmoid(x):
    return jnp.minimum(x, 0.0) - jnp.log(1.0 + jnp.exp(-jnp.abs(x)))


def _softplus(x):
    return jnp.maximum(x, 0.0) + jnp.log(1.0 + jnp.exp(-jnp.abs(x)))


def _gelu_tanh(x):
    c = np.float32(np.sqrt(2.0 / np.pi))
    return 0.5 * x * (1.0 + jnp.tanh(c * (x + 0.044715 * (x * x * x))))


def _dot(a, b):
    return jnp.dot(a, b, preferred_element_type=F32)


def _cparams(sem):
    return pltpu.CompilerParams(dimension_semantics=sem, vmem_limit_bytes=VMEM_LIMIT_BYTES)


def _const_spec(shape):
    nd = len(shape)
    return pl.BlockSpec(shape, lambda *_: (0,) * nd)


def _layer_spec(stacked_shape, layer):
    tail = tuple(stacked_shape[1:])
    return pl.BlockSpec((None,) + tail, lambda *_: (layer,) + (0,) * len(tail))


def _layer_weight_spec(stacked_shape, layer):
    tail = tuple(stacked_shape[1:])
    return pl.BlockSpec((None,) + tail, lambda *_: (layer,) + (0,) * len(tail),
                        pipeline_mode=pl.Buffered(1))


def _mod_spec(mod_shape, layer):
    return pl.BlockSpec((None, 1) + tuple(mod_shape[2:]), lambda b, i: (layer, b, 0, 0))


def _mod_kernel(c_ref, w_ref, b_ref, o_ref):
    c = c_ref[...]
    ca = (c * _sigmoid(c)).astype(BF16)
    o_ref[0] = _dot(ca, w_ref[0].astype(BF16)) + b_ref[0]


def _modulation(c, w_mod, b_mod):
    L, D, N = w_mod.shape
    B = c.shape[0]
    bp = -(-B // SUBLANES) * SUBLANES
    cp = jnp.pad(c, ((0, bp - B), (0, 0)))
    tn = 1024
    out = pl.pallas_call(
        _mod_kernel,
        grid=(L, N // tn),
        in_specs=[
            pl.BlockSpec((bp, D), lambda l, n: (0, 0)),
            pl.BlockSpec((1, D, tn), lambda l, n: (l, 0, n)),
            pl.BlockSpec((1, 1, tn), lambda l, n: (l, 0, n)),
        ],
        out_specs=pl.BlockSpec((1, bp, tn), lambda l, n: (l, 0, n)),
        out_shape=jax.ShapeDtypeStruct((L, bp, N), F32),
        compiler_params=_cparams(("arbitrary", "arbitrary")),
        name="adaln_mod",
    )(cp, w_mod, b_mod.reshape(L, 1, N))
    return out.reshape(L, bp, N_MOD, D)


def _rms_mod(x, gain, scale, shift):
    ms = jnp.mean(x * x, axis=-1, keepdims=True)
    y = x * lax.rsqrt(ms + EPS) * gain
    return y * (1.0 + scale) + shift


def _in_proj_kernel(x_ref, mod_ref, g_ref, w_ref, cw_ref, cb_ref, wr_ref, br_ref, wi_ref, bi_ref,
                    lam_ref, oa_ref, fox_ref, gla_ref, small_ref, xs_ref, h_ref):
    tm, W = x_ref.shape[1], LRU_WIDTH

    @pl.when(pl.program_id(1) == 0)
    def _():
        xs_ref[0:SUBLANES, :] = jnp.zeros((SUBLANES, W), F32)
        h_ref[...] = jnp.zeros_like(h_ref)

    h = _rms_mod(x_ref[0], g_ref[...], mod_ref[0, 1:2, :], mod_ref[0, 0:1, :]).astype(BF16)
    segs, c0 = [], LRU_COLS
    for ref, n in ((fox_ref, FOX_COLS), (gla_ref, GLA_COLS), (small_ref, SMALL_COLS)):
        segs.append((ref, c0, c0 + n))
        c0 += n

    def project(g0):
        g1 = g0 + MXU_COLS
        y = _dot(h, w_ref[:, g0:g1])
        for ref, s0, s1 in segs:
            lo, hi = max(g0, s0), min(g1, s1)
            if lo < hi:
                ref[0, :, lo - s0:hi - s0] = y[:, lo - g0:hi - g0].astype(ref.dtype)

    y0 = _dot(h, w_ref[:, 0:LRU_COLS])
    xs_ref[SUBLANES:SUBLANES + tm, :] = y0[:, 0:W]
    pieces = [(s, r0) for s in range(W // LANES) for r0 in range(0, tm, LRU_ROWS)]
    gates = [_lru_gates(y0[r0:r0 + LRU_ROWS, s * LANES:(s + 1) * LANES], r0, slice(s * LANES, (s + 1) * LANES),
                        cw_ref, cb_ref, wr_ref, wi_ref, xs_ref) for s, r0 in pieces]
    xs_ref[0:SUBLANES, :] = xs_ref[tm:tm + SUBLANES, :]
    chunks = list(range(LRU_COLS, NP_COLS, MXU_COLS))
    carry = None
    for (s, r0), g in zip(pieces, gates):
        sl = slice(s * LANES, (s + 1) * LANES)
        rows = slice(r0, r0 + LRU_ROWS)
        if chunks:
            project(chunks.pop(0))
        if r0 == 0:
            carry = h_ref[0:1, sl]
        out, carry = _lru_finish(*g, y0[rows, W + s * LANES:W + (s + 1) * LANES], sl, carry,
                                 br_ref, bi_ref, lam_ref)
        oa_ref[0, rows, sl] = out.astype(oa_ref.dtype)
        if r0 + LRU_ROWS == tm:
            h_ref[:, sl] = jnp.broadcast_to(carry, (h_ref.shape[0], LANES))
    for g0 in chunks:
        project(g0)


def _in_proj(x, mod, gain, w, lru_p, layer, tm):
    B, S, D = x.shape
    W = LRU_WIDTH
    tok = lambda n: pl.BlockSpec((1, tm, n), lambda b, i: (b, i, 0))
    return pl.pallas_call(
        _in_proj_kernel,
        grid=(B, S // tm),
        in_specs=[tok(D), _mod_spec(mod.shape, layer), _layer_spec(gain.shape, layer),
                  _layer_weight_spec(w.shape, layer)] + [_layer_spec(p.shape, layer) for p in lru_p],
        out_specs=[tok(W), tok(FOX_COLS), tok(GLA_COLS), tok(SMALL_COLS)],
        out_shape=[
            jax.ShapeDtypeStruct((B, S, W), BF16),
            jax.ShapeDtypeStruct((B, S, FOX_COLS), BF16),
            jax.ShapeDtypeStruct((B, S, GLA_COLS), BF16),
            jax.ShapeDtypeStruct((B, S, SMALL_COLS), F32),
        ],
        scratch_shapes=[pltpu.VMEM((tm + SUBLANES, W), F32), pltpu.VMEM((SUBLANES, W), F32)],
        compiler_params=_cparams(("arbitrary", "arbitrary")),
        name="in_proj_lru",
    )(x, mod, gain, w, *lru_p)


def _scan_add(v):
    n = v.shape[0]
    row = lax.broadcasted_iota(jnp.int32, v.shape, 0)
    d = 1
    while d < n:
        v = v + jnp.where(row >= d, pltpu.roll(v, d, axis=0), 0.0)
        d *= 2
    return v


def _scan_linear(a, u, h0):
    n, w = a.shape
    g = n // SUBLANES
    a = a.reshape(g, SUBLANES, w)
    u = u.reshape(g, SUBLANES, w)
    sub = lax.broadcasted_iota(jnp.int32, a.shape, 1)
    d = 1
    while d < SUBLANES:
        m = sub >= d
        u = u + jnp.where(m, a * pltpu.roll(u, d, axis=1), 0.0)
        a = jnp.where(m, a * pltpu.roll(a, d, axis=1), a)
        d *= 2
    hs, h = [], h0
    for v in range(g):
        hv = a[v] * h + u[v]
        h = hv[SUBLANES - 1:SUBLANES, :]
        hs.append(hv)
    return jnp.concatenate(hs, axis=0), h


def _lru_gates(xa, r0, sl, cw_ref, cb_ref, wr_ref, wi_ref, xs_ref):
    n = xa.shape[0]
    xc = xa * cw_ref[CONV_WIDTH - 1:CONV_WIDTH, sl] + cb_ref[:, sl]
    for k in range(1, CONV_WIDTH):
        j = CONV_WIDTH - 1 - k
        xc = xc + xs_ref[pl.ds(SUBLANES - k + r0, n), sl] * cw_ref[j:j + 1, sl]
    xcb = xc.astype(BF16)
    return xc, _dot(xcb, wr_ref[sl, sl]), _dot(xcb, wi_ref[sl, sl])


def _lru_finish(xc, r_pre, i_pre, ya, sl, h0, br_ref, bi_ref, lam_ref):
    r = _sigmoid(r_pre + br_ref[:, sl])
    i = _sigmoid(i_pre + bi_ref[:, sl])
    a = jnp.exp2(r * ((-LRU_C * LOG2E) * _softplus(-lam_ref[:, sl])))
    mult = jnp.sqrt(1.0 - a * a)
    u = mult * (i * xc)
    h, h_last = _scan_linear(a, u, h0)
    return h * _gelu_tanh(ya), h_last


def _fox_prep_body(fox_ref, small_ref, bf_ref, qg_ref, kg_ref, ones_ref,
                   rq_ref, rk_ref, kaug_ref, qaugt_ref, vt_ref, cum_ref):
    tm = fox_ref.shape[1]
    W = FOX_WIDTH
    q = fox_ref[0, :, 0:W].astype(F32)
    k = fox_ref[0, :, W:2 * W].astype(F32)
    v = fox_ref[0, :, 2 * W:3 * W].astype(F32)

    def head_norm(t, gain):
        ms = _dot((t * t).astype(BF16), ones_ref[...])
        return t * lax.rsqrt(ms + EPS) * gain

    qn = head_norm(q, qg_ref[...])
    kn = head_norm(k, kg_ref[...])

    lf = _log_sigmoid(small_ref[0] + bf_ref[...])
    cum = _scan_add(lf) + cum_ref[0:1, :]
    cum_ref[...] = jnp.broadcast_to(cum[tm - 1:tm, :], cum_ref.shape)

    cs = cum * LOG2E
    c1 = cs.astype(BF16).astype(F32)
    r1 = cs - c1
    c2 = r1.astype(BF16).astype(F32)
    c3 = (r1 - c2).astype(BF16).astype(F32)
    lane = lax.broadcasted_iota(jnp.int32, cum.shape, 1)
    H = FOX_HEADS
    packed = jnp.where(lane < H, c1, 0.0)
    packed = jnp.where((lane >= H) & (lane < 2 * H), pltpu.roll(c2, H, axis=1), packed)
    packed = jnp.where((lane >= 2 * H) & (lane < 3 * H), pltpu.roll(c3, 2 * H, axis=1), packed)
    packed = jnp.where(lane == ONE_LANE, 1.0, packed).astype(BF16)

    kbias = _dot(packed, rk_ref[...])
    qbias = _dot(packed, rq_ref[...])
    feat = lane < HEAD_DIM

    def head_aug(t, bias, h):
        src = t[:, (h // 2) * LANES:(h // 2 + 1) * LANES]
        if h % 2:
            src = pltpu.roll(src, HEAD_DIM, axis=1)
        return jnp.where(feat, src, bias[:, h * AUG:(h + 1) * AUG])

    vt = v.T
    tk = vt_ref.shape[4]
    for h in range(H):
        kaug_ref[0, h] = head_aug(kn, kbias, h).astype(BF16)
        qaugt_ref[0, h] = head_aug(qn, qbias, h).T.astype(BF16)
        for cb in range(tm // tk):
            vt_ref[0, h, cb, 0:HEAD_DIM, :] = (
                vt[h * HEAD_DIM:(h + 1) * HEAD_DIM, cb * tk:(cb + 1) * tk].astype(BF16))
            vt_ref[0, h, cb, HEAD_DIM:V_ROWS, :] = jnp.ones((V_ROWS - HEAD_DIM, tk), BF16)


def _fox_attn_kernel(k_ref, qt_ref, vt_ref, o_ref, acc_ref, sa_ref, sb_ref, *, tk):
    qi = pl.program_id(1)
    units = [(h, c) for h in range(FOX_HEADS) for c in range(NQ_SUB)]
    acc_ref[...] = jnp.zeros_like(acc_ref)

    def qk(h, c, j):
        start = pl.multiple_of(j * tk, tk)
        return _dot(k_ref[0, h, pl.ds(start, tk), :], qt_ref[0, h, :, c * tk:(c + 1) * tk])

    def update(h, c, j, s, m_old, m_new):
        p = jnp.exp2(s - m_new).astype(BF16)
        acc_ref[h, c] = jnp.exp2(m_old - m_new) * acc_ref[h, c] + _dot(vt_ref[0, h, j], p)

    def step(j, cur_ref, c_cur, nxt_ref, ms):
        new_ms, c_nxt = [], []
        for i, (h, c) in enumerate(units):
            s_n = qk(h, c, j + 1)
            nxt_ref[h, c] = s_n
            c_nxt.append(jnp.max(s_n, axis=0, keepdims=True))
            m_new = jnp.maximum(ms[i], c_cur[i])
            new_ms.append(m_new)
            update(h, c, j, cur_ref[h, c], ms[i], m_new)
        return tuple(new_ms), tuple(c_nxt)

    def pair(t, carry):
        ms, ca = carry
        ms, cb = step(2 * t, sa_ref, ca, sb_ref, ms)
        ms, ca = step(2 * t + 1, sb_ref, cb, sa_ref, ms)
        return ms, ca

    c0 = []
    for h, c in units:
        s = qk(h, c, 0)
        sa_ref[h, c] = s
        c0.append(jnp.max(s, axis=0, keepdims=True))
    carry = (tuple(jnp.full((1, tk), NEG_BIG, F32) for _ in units), tuple(c0))
    assert NQ_SUB % 2 == 0
    n_pairs = (NQ_SUB // 2) * qi
    n_quads = lax.div(n_pairs, 2)
    carry = lax.fori_loop(0, n_quads, lambda u, cr: pair(2 * u + 1, pair(2 * u, cr)), carry)
    carry = lax.fori_loop(2 * n_quads, n_pairs, pair, carry)

    diag = (lax.broadcasted_iota(jnp.int32, (tk, tk), 0)
            <= lax.broadcasted_iota(jnp.int32, (tk, tk), 1))
    ms, cmax = list(carry[0]), list(carry[1])
    bufs = (sa_ref, sb_ref)
    for d in range(NQ_SUB):
        j = NQ_SUB * qi + d
        cur_ref, nxt_ref = bufs[d % 2], bufs[(d + 1) % 2]
        for i, (h, c) in enumerate(units):
            if c < d:
                continue
            if c > d:
                s_n = qk(h, c, j + 1)
                nxt_ref[h, c] = s_n
                c_next = jnp.max(s_n, axis=0, keepdims=True)
            if c == d:
                s = jnp.where(diag, cur_ref[h, c], NEG_BIG)
                m_new = jnp.maximum(ms[i], jnp.max(s, axis=0, keepdims=True))
            else:
                s = cur_ref[h, c]
                m_new = jnp.maximum(ms[i], cmax[i])
                cmax[i] = c_next
            update(h, c, j, s, ms[i], m_new)
            ms[i] = m_new

    for c in range(NQ_SUB):
        out = jnp.concatenate(
            [acc_ref[h, c, 0:HEAD_DIM, :] * (1.0 / acc_ref[h, c, HEAD_DIM:HEAD_DIM + 1, :])
             for h in range(FOX_HEADS)], axis=0)
        o_ref[0, c * tk:(c + 1) * tk, :] = out.T.astype(o_ref.dtype)


def _fox_attention(kaug, qaugt, vt, tk):
    B, H, S, _ = kaug.shape
    tq = NQ_SUB * tk
    assert S % tq == 0
    return pl.pallas_call(
        functools.partial(_fox_attn_kernel, tk=tk),
        grid=(B, S // tq),
        in_specs=[
            pl.BlockSpec((1, H, S, AUG), lambda b, i: (b, 0, 0, 0)),
            pl.BlockSpec((1, H, AUG, tq), lambda b, i: (b, 0, 0, i)),
            pl.BlockSpec((1, H, S // tk, V_ROWS, tk), lambda b, i: (b, 0, 0, 0, 0)),
        ],
        out_specs=pl.BlockSpec((1, tq, H * HEAD_DIM), lambda b, i: (b, i, 0)),
        out_shape=jax.ShapeDtypeStruct((B, S, H * HEAD_DIM), BF16),
        scratch_shapes=[pltpu.VMEM((H, NQ_SUB, V_ROWS, tk), F32), pltpu.VMEM((H, NQ_SUB, tk, tk), F32),
                        pltpu.VMEM((H, NQ_SUB, tk, tk), F32)],
        compiler_params=_cparams(("arbitrary", "arbitrary")),
        name="fox_attention",
    )(kaug, qaugt, vt)


GLA_ROWS = 256


def _gla_body(gla_ref, small_ref, wa_ref, ba_ref, gain_ref, tri_ref, hmask_ref, bdmask_ref,
              ones_ref, o_ref, st_ref):
    ts = o_ref.shape[1]
    KW, C, R = GLA_KW, GLA_CHUNK, GLA_ROWS
    q = gla_ref[0, :, 0:KW].astype(F32)
    k = gla_ref[0, :, KW:2 * KW].astype(F32)
    v = gla_ref[0, :, 2 * KW:3 * KW]
    low = small_ref[0].astype(BF16)
    la = _log_sigmoid(_dot(low, wa_ref[...]) + ba_ref[...]) * (LOG2E / GLA_TAU)
    la_hi = la.astype(BF16)
    la_lo = (la - la_hi.astype(F32)).astype(BF16)
    tri = tri_ref[...]
    bcum = jnp.concatenate(
        [_dot(tri, la_hi[g0:g0 + R]) + _dot(tri, la_lo[g0:g0 + R]) for g0 in range(0, ts, R)], axis=0)
    ends = [bcum[c0 + C - 1:c0 + C, :] for c0 in range(0, ts, C)]
    b_last = jnp.concatenate([jnp.broadcast_to(e, (C, KW)) for e in ends], axis=0)
    q_dec = (q * jnp.exp2(bcum)).astype(BF16)
    k_dec = (k * jnp.exp2(-bcum)).astype(BF16)
    k_end = (k * jnp.exp2(b_last - bcum)).astype(BF16)

    o_groups = []
    for g0 in range(0, ts, R):
        rows = slice(g0, g0 + R)
        o_acc = jnp.zeros((R, KW), F32)
        for h in range(GLA_HEADS):
            hm = hmask_ref[h:h + 1, :].astype(BF16)
            att = lax.dot_general(q_dec[rows] * hm, k_dec[rows], (((1,), (1,)), ((), ())),
                                  preferred_element_type=F32)
            o_acc = o_acc + _dot(att.astype(BF16) * tri, v[rows] * hm)
        o_groups.append(o_acc)

    o_parts = []
    for ci, c0 in enumerate(range(0, ts, C)):
        cs = slice(c0, c0 + C)
        st = st_ref[...]
        o_parts.append(lax.dot_general(q_dec[cs], st.astype(BF16), (((1,), (1,)), ((), ())),
                                       preferred_element_type=F32))
        kv_t = lax.dot_general(v[cs], k_end[cs], (((0,), (0,)), ((), ())),
                               preferred_element_type=F32)
        st_ref[...] = st * jnp.exp2(ends[ci]) + kv_t * bdmask_ref[...]
    o = jnp.concatenate(o_groups, axis=0) + jnp.concatenate(o_parts, axis=0)

    ms = _dot((o * o).astype(BF16), ones_ref[...])
    o = o * lax.rsqrt(ms + EPS) * gain_ref[...]
    g = gla_ref[0, :, 3 * KW:4 * KW].astype(F32)
    o_ref[0] = (o * (g * _sigmoid(g))).astype(o_ref.dtype)


def _fox_gla_kernel(fox_ref, gla_ref, small_ref, bf_ref, qg_ref, kg_ref, fones_ref, rq_ref, rk_ref,
                    wa_ref, ba_ref, gain_ref, tri_ref, hmask_ref, bdmask_ref, gones_ref,
                    kaug_ref, qaugt_ref, vt_ref, oc_ref, cum_ref, st_ref):
    @pl.when(pl.program_id(1) == 0)
    def _():
        cum_ref[...] = jnp.zeros_like(cum_ref)
        st_ref[...] = jnp.zeros_like(st_ref)

    _fox_prep_body(fox_ref, small_ref, bf_ref, qg_ref, kg_ref, fones_ref, rq_ref, rk_ref,
                   kaug_ref, qaugt_ref, vt_ref, cum_ref)
    _gla_body(gla_ref, small_ref, wa_ref, ba_ref, gain_ref, tri_ref, hmask_ref, bdmask_ref,
              gones_ref, oc_ref, st_ref)


def _fox_prep_gla(fox, gla, small, fox_p, gla_p, layer, tm, tk):
    B, S, _ = fox.shape
    H, KW = FOX_HEADS, GLA_KW
    tok = lambda n: pl.BlockSpec((1, tm, n), lambda b, i: (b, i, 0))
    specs = lambda stacked, consts: ([_layer_spec(p.shape, layer) for p in stacked]
                                     + [_const_spec(p.shape) for p in consts])
    fox_specs, gla_specs = specs(*fox_p), specs(*gla_p)
    fox_p, gla_p = fox_p[0] + fox_p[1], gla_p[0] + gla_p[1]
    return pl.pallas_call(
        _fox_gla_kernel,
        grid=(B, S // tm),
        in_specs=[tok(FOX_COLS), tok(GLA_COLS), tok(SMALL_COLS)] + fox_specs + gla_specs,
        out_specs=[
            pl.BlockSpec((1, H, tm, AUG), lambda b, i: (b, 0, i, 0)),
            pl.BlockSpec((1, H, AUG, tm), lambda b, i: (b, 0, 0, i)),
            pl.BlockSpec((1, H, tm // tk, V_ROWS, tk), lambda b, i: (b, 0, i, 0, 0)),
            tok(KW),
        ],
        out_shape=[
            jax.ShapeDtypeStruct((B, H, S, AUG), BF16),
            jax.ShapeDtypeStruct((B, H, AUG, S), BF16),
            jax.ShapeDtypeStruct((B, H, S // tk, V_ROWS, tk), BF16),
            jax.ShapeDtypeStruct((B, S, KW), BF16),
        ],
        scratch_shapes=[pltpu.VMEM((SUBLANES, LANES), F32), pltpu.VMEM((KW, KW), F32)],
        compiler_params=_cparams(("arbitrary", "arbitrary")),
        name="fox_prep_gla",
    )(fox, gla, small, *fox_p, *gla_p)


def _ffn_chunks(d_ff):
    chunks, c0 = [], 0
    while c0 < d_ff:
        n = min(512, d_ff - c0)
        chunks.append((c0, n))
        c0 += n
    return chunks


def _out_ffn_kernel(x_ref, a_ref, b_ref, c_ref, mod_ref, g_ref, wo_ref, wgu_ref, wd_ref, o_ref,
                    acc_ref):
    d_ff = wd_ref.shape[0]
    mix = _dot(jnp.concatenate([a_ref[0], b_ref[0], c_ref[0]], axis=1), wo_ref[...])
    x1 = x_ref[0] + mod_ref[0, 2:3, :] * mix
    h = _rms_mod(x1, g_ref[...], mod_ref[0, 4:5, :], mod_ref[0, 3:4, :]).astype(BF16)
    for idx, (c0, n) in enumerate(_ffn_chunks(d_ff)):
        gt = _dot(h, wgu_ref[:, c0:c0 + n])
        up = _dot(h, wgu_ref[:, d_ff + c0:d_ff + c0 + n])
        act = (gt * _sigmoid(gt) * up).astype(BF16)
        y = _dot(act, wd_ref[c0:c0 + n, :])
        if idx == 0:
            acc_ref[...] = y
        else:
            acc_ref[...] += y
    o_ref[0] = x1 + mod_ref[0, 5:6, :] * acc_ref[...]


def _out_ffn(x, a, b, c, mod, gain, wo, wgu, wd, layer, tm):
    B, S, D = x.shape
    tok = lambda n: pl.BlockSpec((1, tm, n), lambda bb, i: (bb, i, 0))
    return pl.pallas_call(
        _out_ffn_kernel,
        grid=(B, S // tm),
        in_specs=[
            tok(D), tok(a.shape[2]), tok(b.shape[2]), tok(c.shape[2]),
            _mod_spec(mod.shape, layer), _layer_spec(gain.shape, layer),
            _layer_weight_spec(wo.shape, layer), _layer_weight_spec(wgu.shape, layer),
            _layer_weight_spec(wd.shape, layer),
        ],
        out_specs=tok(D),
        out_shape=jax.ShapeDtypeStruct((B, S, D), F32),
        scratch_shapes=[pltpu.VMEM((tm, D), F32)],
        compiler_params=_cparams(("arbitrary", "arbitrary")),
        name="out_proj_ffn",
    )(x, a, b, c, mod, gain, wo, wgu, wd)


def _block_diag(w):
    L, n, d, e = w.shape
    eye = jnp.eye(n, dtype=w.dtype)
    return jnp.einsum("lnde,nm->lndme", w, eye).reshape(L, n * d, n * e)


def _regroup_w_in(w_in):
    L, D, _ = w_in.shape
    sizes = (LRU_WIDTH, LRU_WIDTH, FOX_WIDTH, FOX_WIDTH, FOX_WIDTH, FOX_HEADS,
             GLA_KW, GLA_KW, GLA_VW, GLA_RANK, GLA_VW)
    offs = np.concatenate([[0], np.cumsum(sizes)])
    w = w_in.astype(BF16)
    seg = lambda i: w[:, :, offs[i]:offs[i + 1]]
    z = lambda n: jnp.zeros((L, D, n), BF16)
    q_scale = GLA_DK ** -0.5
    assert np.log2(q_scale) == np.round(np.log2(q_scale))
    cols = [seg(0), seg(1), seg(2), seg(3), seg(4), seg(6) * q_scale, seg(7), seg(8), seg(10),
            seg(5), z(GLOW_LANE0 - FOX_HEADS), seg(9), z(SMALL_COLS - GLOW_LANE0 - GLA_RANK)]
    return jnp.concatenate(cols, axis=-1)


def _head_ones(n_heads, dim):
    assert dim & (dim - 1) == 0
    return jnp.asarray(np.kron(np.eye(n_heads), np.full((dim, dim), 1.0 / dim)), BF16)


def _fox_constants():
    H, W = FOX_HEADS, FOX_WIDTH
    rq = np.zeros((LANES, H * AUG), np.float32)
    rk = np.zeros((LANES, H * AUG), np.float32)
    for h in range(H):
        for s in range(N_SPLIT):
            rq[s * H + h, h * AUG + HEAD_DIM + s] = 1.0
            rq[ONE_LANE, h * AUG + HEAD_DIM + N_SPLIT + s] = 1.0
            rk[ONE_LANE, h * AUG + HEAD_DIM + s] = 1.0
            rk[s * H + h, h * AUG + HEAD_DIM + N_SPLIT + s] = -1.0
    return jnp.asarray(rq, BF16), jnp.asarray(rk, BF16)


def _gla_constants():
    R, C, KW = GLA_ROWS, GLA_CHUNK, GLA_KW
    i = np.arange(R)
    tri = ((i[:, None] >= i[None, :]) & (i[:, None] // C == i[None, :] // C)).astype(np.float32)
    lane = np.arange(KW)
    hmask = (lane[None, :] // GLA_DK == np.arange(GLA_HEADS)[:, None]).astype(np.float32)
    bdmask = (lane[:, None] // GLA_DK == lane[None, :] // GLA_DK).astype(np.float32)
    return jnp.asarray(tri, BF16), jnp.asarray(hmask, F32), jnp.asarray(bdmask, F32)


def _pick_tile(n, want):
    t = min(n, want)
    while n % t:
        t //= 2
    return t


def kernel(x, c, norm1_gain, norm2_gain, w_mod, b_mod, w_in, conv_w, conv_b, lru_w_r, lru_b_r,
           lru_w_i, lru_b_i, lru_lambda, fox_b_f, fox_q_gain, fox_k_gain, gla_w_alpha, gla_b_alpha,
           gla_out_gain, w_out, ffn_w_gate_up, ffn_w_down):
    B, S, D = x.shape
    L = w_in.shape[0]
    tm = _pick_tile(S, 512)
    tp = _pick_tile(S, 1024)
    tq = _pick_tile(S, 256)

    mod = _modulation(c, w_mod, b_mod)
    w_in_p = _regroup_w_in(w_in)
    wr_bd = _block_diag(lru_w_r).astype(BF16)
    wi_bd = _block_diag(lru_w_i).astype(BF16)
    w_out_b = w_out.astype(BF16)
    wgu_b = ffn_w_gate_up.astype(BF16)
    wd_b = ffn_w_down.astype(BF16)
    row = lambda a: a.reshape(L, 1, a.shape[-1])
    bf_pad = jnp.pad(fox_b_f, ((0, 0), (0, LANES - FOX_HEADS))).reshape(L, 1, LANES)
    qg = jnp.tile(fox_q_gain * (HEAD_DIM ** -0.5 * LOG2E), (1, FOX_HEADS)).reshape(L, 1, FOX_WIDTH)
    kg = jnp.tile(fox_k_gain, (1, FOX_HEADS)).reshape(L, 1, FOX_WIDTH)
    wa_pad = jnp.pad(gla_w_alpha, ((0, 0), (GLOW_LANE0, LANES - GLOW_LANE0 - GLA_RANK), (0, 0))).astype(BF16)
    og = jnp.tile(gla_out_gain, (1, GLA_HEADS)).reshape(L, 1, GLA_VW)
    fox_ones = _head_ones(FOX_HEADS, HEAD_DIM)
    gla_ones = _head_ones(GLA_HEADS, GLA_DK)
    rq, rk = _fox_constants()
    tri, hmask, bdmask = _gla_constants()

    g1, g2 = row(norm1_gain), row(norm2_gain)
    lru_p = (conv_w, row(conv_b), wr_bd, row(lru_b_r), wi_bd, row(lru_b_i), row(lru_lambda))
    fox_p = ((bf_pad, qg, kg), (fox_ones, rq, rk))
    gla_p = ((wa_pad, row(gla_b_alpha), og), (tri, hmask, bdmask, gla_ones))
    for l in range(L):
        out_a, fox, gla, small = _in_proj(x, mod, g1, w_in_p, lru_p, l, tp)
        kaug, qaugt, vt, out_c = _fox_prep_gla(fox, gla, small, fox_p, gla_p, l, tp, tq)
        out_b = _fox_attention(kaug, qaugt, vt, tq)
        x = _out_ffn(x, out_a, out_b, out_c, mod, g2, w_out_b, wgu_b, wd_b, l, tm)
    return x
```

```python
import functools

import numpy as np
import jax
import jax.numpy as jnp
from jax import lax
from jax.experimental import pallas as pl
from jax.experimental.pallas import tpu as pltpu

F32 = jnp.float32
BF16 = jnp.bfloat16

EPS = 1e-6
HEAD_DIM = 64
LRU_WIDTH = 384
LRU_BLOCKS = 6
CONV_WIDTH = 4
LRU_C = 8.0
FOX_HEADS = 6
FOX_WIDTH = FOX_HEADS * HEAD_DIM
GLA_HEADS = 4
GLA_DK = 64
GLA_KW = GLA_HEADS * GLA_DK
GLA_VW = GLA_KW
GLA_RANK = 16
GLA_TAU = 16.0
GLA_CHUNK = 64
N_MOD = 6

LANES = 128
SUBLANES = 8
VMEM_LIMIT_BYTES = 56 * 1024 * 1024

LRU_COLS = 2 * LRU_WIDTH
FOX_COLS = 3 * FOX_WIDTH
GLA_COLS = 3 * GLA_KW + GLA_VW
SMALL_COLS = LANES
GLOW_LANE0 = 8
NP_COLS = LRU_COLS + FOX_COLS + GLA_COLS + SMALL_COLS
MXU_COLS = 256
LRU_ROWS = 128

AUG = LANES
N_SPLIT = 3
ONE_LANE = N_SPLIT * FOX_HEADS
NEG_BIG = -1e30
V_ROWS = HEAD_DIM + 16
NQ_SUB = 2
LOG2E = float(np.log2(np.e))


def _sigmoid(x):
    return 0.5 * jnp.tanh(0.5 * x) + 0.5


def _log_sigmoid(x):
    return jnp.minimum(x, 0.0) - jnp.log(1.0 + jnp.exp(-jnp.abs(x)))


def _softplus(x):
    return jnp.maximum(x, 0.0) + jnp.log(1.0 + jnp.exp(-jnp.abs(x)))


def _gelu_tanh(x):
    c = np.float32(np.sqrt(2.0 / np.pi))
    return 0.5 * x * (1.0 + jnp.tanh(c * (x + 0.044715 * (x * x * x))))


def _dot(a, b):
    return jnp.dot(a, b, preferred_element_type=F32)


def _cparams(sem):
    return pltpu.CompilerParams(dimension_semantics=sem, vmem_limit_bytes=VMEM_LIMIT_BYTES)


def _const_spec(shape):
    nd = len(shape)
    return pl.BlockSpec(shape, lambda *_: (0,) * nd)


def _layer_spec(stacked_shape, layer):
    tail = tuple(stacked_shape[1:])
    return pl.BlockSpec((None,) + tail, lambda *_: (layer,) + (0,) * len(tail))


def _layer_weight_spec(stacked_shape, layer):
    tail = tuple(stacked_shape[1:])
    return pl.BlockSpec((None,) + tail, lambda *_: (layer,) + (0,) * len(tail),
                        pipeline_mode=pl.Buffered(1))


def _mod_spec(mod_shape, layer):
    return pl.BlockSpec((None, 1) + tuple(mod_shape[2:]), lambda b, i: (layer, b, 0, 0))


def _mod_kernel(c_ref, w_ref, b_ref, o_ref):
    c = c_ref[...]
    ca = (c * _sigmoid(c)).astype(BF16)
    o_ref[0] = _dot(ca, w_ref[0].astype(BF16)) + b_ref[0]


def _modulation(c, w_mod, b_mod):
    L, D, N = w_mod.shape
    B = c.shape[0]
    bp = -(-B // SUBLANES) * SUBLANES
    cp = jnp.pad(c, ((0, bp - B), (0, 0)))
    tn = 1024
    out = pl.pallas_call(
        _mod_kernel,
        grid=(L, N // tn),
        in_specs=[
            pl.BlockSpec((bp, D), lambda l, n: (0, 0)),
            pl.BlockSpec((1, D, tn), lambda l, n: (l, 0, n)),
            pl.BlockSpec((1, 1, tn), lambda l, n: (l, 0, n)),
        ],
        out_specs=pl.BlockSpec((1, bp, tn), lambda l, n: (l, 0, n)),
        out_shape=jax.ShapeDtypeStruct((L, bp, N), F32),
        compiler_params=_cparams(("arbitrary", "arbitrary")),
        name="adaln_mod",
    )(cp, w_mod, b_mod.reshape(L, 1, N))
    return out.reshape(L, bp, N_MOD, D)


def _rms_mod(x, gain, scale, shift):
    ms = jnp.mean(x * x, axis=-1, keepdims=True)
    y = x * lax.rsqrt(ms + EPS) * gain
    return y * (1.0 + scale) + shift


def _in_proj_kernel(x_ref, mod_ref, g_ref, w_ref, cw_ref, cb_ref, wr_ref, br_ref, wi_ref, bi_ref,
                    lam_ref, oa_ref, fox_ref, gla_ref, small_ref, xs_ref, h_ref):
    tm, W = x_ref.shape[1], LRU_WIDTH

    @pl.when(pl.program_id(1) == 0)
    def _():
        xs_ref[0:SUBLANES, :] = jnp.zeros((SUBLANES, W), F32)
        h_ref[...] = jnp.zeros_like(h_ref)

    h = _rms_mod(x_ref[0], g_ref[...], mod_ref[0, 1:2, :], mod_ref[0, 0:1, :]).astype(BF16)
    segs, c0 = [], LRU_COLS
    for ref, n in ((fox_ref, FOX_COLS), (gla_ref, GLA_COLS), (small_ref, SMALL_COLS)):
        segs.append((ref, c0, c0 + n))
        c0 += n

    def project(g0):
        g1 = g0 + MXU_COLS
        y = _dot(h, w_ref[:, g0:g1])
        for ref, s0, s1 in segs:
            lo, hi = max(g0, s0), min(g1, s1)
            if lo < hi:
                ref[0, :, lo - s0:hi - s0] = y[:, lo - g0:hi - g0].astype(ref.dtype)

    y0 = _dot(h, w_ref[:, 0:LRU_COLS])
    xs_ref[SUBLANES:SUBLANES + tm, :] = y0[:, 0:W]
    pieces = [(s, r0) for s in range(W // LANES) for r0 in range(0, tm, LRU_ROWS)]
    gates = [_lru_gates(y0[r0:r0 + LRU_ROWS, s * LANES:(s + 1) * LANES], r0, slice(s * LANES, (s + 1) * LANES),
                        cw_ref, cb_ref, wr_ref, wi_ref, xs_ref) for s, r0 in pieces]
    xs_ref[0:SUBLANES, :] = xs_ref[tm:tm + SUBLANES, :]
    chunks = list(range(LRU_COLS, NP_COLS, MXU_COLS))
    carry = None
    for (s, r0), g in zip(pieces, gates):
        sl = slice(s * LANES, (s + 1) * LANES)
        rows = slice(r0, r0 + LRU_ROWS)
        if chunks:
            project(chunks.pop(0))
        if r0 == 0:
            carry = h_ref[0:1, sl]
        out, carry = _lru_finish(*g, y0[rows, W + s * LANES:W + (s + 1) * LANES], sl, carry,
                                 br_ref, bi_ref, lam_ref)
        oa_ref[0, rows, sl] = out.astype(oa_ref.dtype)
        if r0 + LRU_ROWS == tm:
            h_ref[:, sl] = jnp.broadcast_to(carry, (h_ref.shape[0], LANES))
    for g0 in chunks:
        project(g0)


def _in_proj(x, mod, gain, w, lru_p, layer, tm):
    B, S, D = x.shape
    W = LRU_WIDTH
    tok = lambda n: pl.BlockSpec((1, tm, n), lambda b, i: (b, i, 0))
    return pl.pallas_call(
        _in_proj_kernel,
        grid=(B, S // tm),
        in_specs=[tok(D), _mod_spec(mod.shape, layer), _layer_spec(gain.shape, layer),
                  _layer_weight_spec(w.shape, layer)] + [_layer_spec(p.shape, layer) for p in lru_p],
        out_specs=[tok(W), tok(FOX_COLS), tok(GLA_COLS), tok(SMALL_COLS)],
        out_shape=[
            jax.ShapeDtypeStruct((B, S, W), BF16),
            jax.ShapeDtypeStruct((B, S, FOX_COLS), BF16),
            jax.ShapeDtypeStruct((B, S, GLA_COLS), BF16),
            jax.ShapeDtypeStruct((B, S, SMALL_COLS), F32),
        ],
        scratch_shapes=[pltpu.VMEM((tm + SUBLANES, W), F32), pltpu.VMEM((SUBLANES, W), F32)],
        compiler_params=_cparams(("arbitrary", "arbitrary")),
        name="in_proj_lru",
    )(x, mod, gain, w, *lru_p)


def _scan_add(v):
    n = v.shape[0]
    row = lax.broadcasted_iota(jnp.int32, v.shape, 0)
    d = 1
    while d < n:
        v = v + jnp.where(row >= d, pltpu.roll(v, d, axis=0), 0.0)
        d *= 2
    return v


def _scan_linear(a, u, h0):
    n, w = a.shape
    g = n // SUBLANES
    a = a.reshape(g, SUBLANES, w)
    u = u.reshape(g, SUBLANES, w)
    sub = lax.broadcasted_iota(jnp.int32, a.shape, 1)
    d = 1
    while d < SUBLANES:
        m = sub >= d
        u = u + jnp.where(m, a * pltpu.roll(u, d, axis=1), 0.0)
        a = jnp.where(m, a * pltpu.roll(a, d, axis=1), a)
        d *= 2
    hs, h = [], h0
    for v in range(g):
        hv = a[v] * h + u[v]
        h = hv[SUBLANES - 1:SUBLANES, :]
        hs.append(hv)
    return jnp.concatenate(hs, axis=0), h


def _lru_gates(xa, r0, sl, cw_ref, cb_ref, wr_ref, wi_ref, xs_ref):
    n = xa.shape[0]
    xc = xa * cw_ref[CONV_WIDTH - 1:CONV_WIDTH, sl] + cb_ref[:, sl]
    for k in range(1, CONV_WIDTH):
        j = CONV_WIDTH - 1 - k
        xc = xc + xs_ref[pl.ds(SUBLANES - k + r0, n), sl] * cw_ref[j:j + 1, sl]
    xcb = xc.astype(BF16)
    return xc, _dot(xcb, wr_ref[sl, sl]), _dot(xcb, wi_ref[sl, sl])


def _lru_finish(xc, r_pre, i_pre, ya, sl, h0, br_ref, bi_ref, lam_ref):
    r = _sigmoid(r_pre + br_ref[:, sl])
    i = _sigmoid(i_pre + bi_ref[:, sl])
    a = jnp.exp2(r * ((-LRU_C * LOG2E) * _softplus(-lam_ref[:, sl])))
    mult = jnp.sqrt(1.0 - a * a)
    u = mult * (i * xc)
    h, h_last = _scan_linear(a, u, h0)
    return h * _gelu_tanh(ya), h_last


def _fox_prep_body(fox_ref, small_ref, bf_ref, qg_ref, kg_ref, ones_ref,
                   rq_ref, rk_ref, kaug_ref, qaugt_ref, vt_ref, cum_ref):
    tm = fox_ref.shape[1]
    W = FOX_WIDTH
    q = fox_ref[0, :, 0:W].astype(F32)
    k = fox_ref[0, :, W:2 * W].astype(F32)
    v = fox_ref[0, :, 2 * W:3 * W].astype(F32)

    def head_norm(t, gain):
        ms = _dot((t * t).astype(BF16), ones_ref[...])
        return t * lax.rsqrt(ms + EPS) * gain

    qn = head_norm(q, qg_ref[...])
    kn = head_norm(k, kg_ref[...])

    lf = _log_sigmoid(small_ref[0] + bf_ref[...])
    cum = _scan_add(lf) + cum_ref[0:1, :]
    cum_ref[...] = jnp.broadcast_to(cum[tm - 1:tm, :], cum_ref.shape)

    cs = cum * LOG2E
    c1 = cs.astype(BF16).astype(F32)
    r1 = cs - c1
    c2 = r1.astype(BF16).astype(F32)
    c3 = (r1 - c2).astype(BF16).astype(F32)
    lane = lax.broadcasted_iota(jnp.int32, cum.shape, 1)
    H = FOX_HEADS
    packed = jnp.where(lane < H, c1, 0.0)
    packed = jnp.where((lane >= H) & (lane < 2 * H), pltpu.roll(c2, H, axis=1), packed)
    packed = jnp.where((lane >= 2 * H) & (lane < 3 * H), pltpu.roll(c3, 2 * H, axis=1), packed)
    packed = jnp.where(lane == ONE_LANE, 1.0, packed).astype(BF16)

    kbias = _dot(packed, rk_ref[...])
    qbias = _dot(packed, rq_ref[...])
    feat = lane < HEAD_DIM

    def head_aug(t, bias, h):
        src = t[:, (h // 2) * LANES:(h // 2 + 1) * LANES]
        if h % 2:
            src = pltpu.roll(src, HEAD_DIM, axis=1)
        return jnp.where(feat, src, bias[:, h * AUG:(h + 1) * AUG])

    vt = v.T
    tk = vt_ref.shape[4]
    for h in range(H):
        kaug_ref[0, h] = head_aug(kn, kbias, h).astype(BF16)
        qaugt_ref[0, h] = head_aug(qn, qbias, h).T.astype(BF16)
        for cb in range(tm // tk):
            vt_ref[0, h, cb, 0:HEAD_DIM, :] = (
                vt[h * HEAD_DIM:(h + 1) * HEAD_DIM, cb * tk:(cb + 1) * tk].astype(BF16))
            vt_ref[0, h, cb, HEAD_DIM:V_ROWS, :] = jnp.ones((V_ROWS - HEAD_DIM, tk), BF16)


def _fox_attn_kernel(k_ref, qt_ref, vt_ref, o_ref, acc_ref, sa_ref, sb_ref, *, tk):
    qi = pl.program_id(1)
    units = [(h, c) for h in range(FOX_HEADS) for c in range(NQ_SUB)]
    acc_ref[...] = jnp.zeros_like(acc_ref)

    def qk(h, c, j):
        start = pl.multiple_of(j * tk, tk)
        return _dot(k_ref[0, h, pl.ds(start, tk), :], qt_ref[0, h, :, c * tk:(c + 1) * tk])

    def update(h, c, j, s, m_old, m_new):
        p = jnp.exp2(s - m_new).astype(BF16)
        acc_ref[h, c] = jnp.exp2(m_old - m_new) * acc_ref[h, c] + _dot(vt_ref[0, h, j], p)

    def step(j, cur_ref, c_cur, nxt_ref, ms):
        new_ms, c_nxt = [], []
        for i, (h, c) in enumerate(units):
            s_n = qk(h, c, j + 1)
            nxt_ref[h, c] = s_n
            c_nxt.append(jnp.max(s_n, axis=0, keepdims=True))
            m_new = jnp.maximum(ms[i], c_cur[i])
            new_ms.append(m_new)
            update(h, c, j, cur_ref[h, c], ms[i], m_new)
        return tuple(new_ms), tuple(c_nxt)

    def pair(t, carry):
        ms, ca = carry
        ms, cb = step(2 * t, sa_ref, ca, sb_ref, ms)
        ms, ca = step(2 * t + 1, sb_ref, cb, sa_ref, ms)
        return ms, ca

    c0 = []
    for h, c in units:
        s = qk(h, c, 0)
        sa_ref[h, c] = s
        c0.append(jnp.max(s, axis=0, keepdims=True))
    carry = (tuple(jnp.full((1, tk), NEG_BIG, F32) for _ in units), tuple(c0))
    assert NQ_SUB % 2 == 0
    n_pairs = (NQ_SUB // 2) * qi
    n_quads = lax.div(n_pairs, 2)
    carry = lax.fori_loop(0, n_quads, lambda u, cr: pair(2 * u + 1, pair(2 * u, cr)), carry)
    carry = lax.fori_loop(2 * n_quads, n_pairs, pair, carry)

    diag = (lax.broadcasted_iota(jnp.int32, (tk, tk), 0)
            <= lax.broadcasted_iota(jnp.int32, (tk, tk), 1))
    ms, cmax = list(carry[0]), list(carry[1])
    bufs = (sa_ref, sb_ref)
    for d in range(NQ_SUB):
        j = NQ_SUB * qi + d
        cur_ref, nxt_ref = bufs[d % 2], bufs[(d + 1) % 2]
        for i, (h, c) in enumerate(units):
            if c < d:
                continue
            if c > d:
                s_n = qk(h, c, j + 1)
                nxt_ref[h, c] = s_n
                c_next = jnp.max(s_n, axis=0, keepdims=True)
            if c == d:
                s = jnp.where(diag, cur_ref[h, c], NEG_BIG)
                m_new = jnp.maximum(ms[i], jnp.max(s, axis=0, keepdims=True))
            else:
                s = cur_ref[h, c]
                m_new = jnp.maximum(ms[i], cmax[i])
                cmax[i] = c_next
            update(h, c, j, s, ms[i], m_new)
            ms[i] = m_new

    for c in range(NQ_SUB):
        out = jnp.concatenate(
            [acc_ref[h, c, 0:HEAD_DIM, :] * (1.0 / acc_ref[h, c, HEAD_DIM:HEAD_DIM + 1, :])
             for h in range(FOX_HEADS)], axis=0)
        o_ref[0, c * tk:(c + 1) * tk, :] = out.T.astype(o_ref.dtype)


def _fox_attention(kaug, qaugt, vt, tk):
    B, H, S, _ = kaug.shape
    tq = NQ_SUB * tk
    assert S % tq == 0
    return pl.pallas_call(
        functools.partial(_fox_attn_kernel, tk=tk),
        grid=(B, S // tq),
        in_specs=[
            pl.BlockSpec((1, H, S, AUG), lambda b, i: (b, 0, 0, 0)),
            pl.BlockSpec((1, H, AUG, tq), lambda b, i: (b, 0, 0, i)),
            pl.BlockSpec((1, H, S // tk, V_ROWS, tk), lambda b, i: (b, 0, 0, 0, 0)),
        ],
        out_specs=pl.BlockSpec((1, tq, H * HEAD_DIM), lambda b, i: (b, i, 0)),
        out_shape=jax.ShapeDtypeStruct((B, S, H * HEAD_DIM), BF16),
        scratch_shapes=[pltpu.VMEM((H, NQ_SUB, V_ROWS, tk), F32), pltpu.VMEM((H, NQ_SUB, tk, tk), F32),
                        pltpu.VMEM((H, NQ_SUB, tk, tk), F32)],
        compiler_params=_cparams(("arbitrary", "arbitrary")),
        name="fox_attention",
    )(kaug, qaugt, vt)


GLA_ROWS = 256


def _gla_body(gla_ref, small_ref, wa_ref, ba_ref, gain_ref, tri_ref, hmask_ref, bdmask_ref,
              ones_ref, o_ref, st_ref):
    ts = o_ref.shape[1]
    KW, C, R = GLA_KW, GLA_CHUNK, GLA_ROWS
    q = gla_ref[0, :, 0:KW].astype(F32)
    k = gla_ref[0, :, KW:2 * KW].astype(F32)
    v = gla_ref[0, :, 2 * KW:3 * KW]
    low = small_ref[0].astype(BF16)
    la = _log_sigmoid(_dot(low, wa_ref[...]) + ba_ref[...]) * (LOG2E / GLA_TAU)
    la_hi = la.astype(BF16)
    la_lo = (la - la_hi.astype(F32)).astype(BF16)
    tri = tri_ref[...]
    bcum = jnp.concatenate(
        [_dot(tri, la_hi[g0:g0 + R]) + _dot(tri, la_lo[g0:g0 + R]) for g0 in range(0, ts, R)], axis=0)
    ends = [bcum[c0 + C - 1:c0 + C, :] for c0 in range(0, ts, C)]
    b_last = jnp.concatenate([jnp.broadcast_to(e, (C, KW)) for e in ends], axis=0)
    q_dec = (q * jnp.exp2(bcum)).astype(BF16)
    k_dec = (k * jnp.exp2(-bcum)).astype(BF16)
    k_end = (k * jnp.exp2(b_last - bcum)).astype(BF16)

    o_groups = []
    for g0 in range(0, ts, R):
        rows = slice(g0, g0 + R)
        o_acc = jnp.zeros((R, KW), F32)
        for h in range(GLA_HEADS):
            hm = hmask_ref[h:h + 1, :].astype(BF16)
            att = lax.dot_general(q_dec[rows] * hm, k_dec[rows], (((1,), (1,)), ((), ())),
                                  preferred_element_type=F32)
            o_acc = o_acc + _dot(att.astype(BF16) * tri, v[rows] * hm)
        o_groups.append(o_acc)

    o_parts = []
    for ci, c0 in enumerate(range(0, ts, C)):
        cs = slice(c0, c0 + C)
        st = st_ref[...]
        o_parts.append(lax.dot_general(q_dec[cs], st.astype(BF16), (((1,), (1,)), ((), ())),
                                       preferred_element_type=F32))
        kv_t = lax.dot_general(v[cs], k_end[cs], (((0,), (0,)), ((), ())),
                               preferred_element_type=F32)
        st_ref[...] = st * jnp.exp2(ends[ci]) + kv_t * bdmask_ref[...]
    o = jnp.concatenate(o_groups, axis=0) + jnp.concatenate(o_parts, axis=0)

    ms = _dot((o * o).astype(BF16), ones_ref[...])
    o = o * lax.rsqrt(ms + EPS) * gain_ref[...]
    g = gla_ref[0, :, 3 * KW:4 * KW].astype(F32)
    o_ref[0] = (o * (g * _sigmoid(g))).astype(o_ref.dtype)


def _fox_gla_kernel(fox_ref, gla_ref, small_ref, bf_ref, qg_ref, kg_ref, fones_ref, rq_ref, rk_ref,
                    wa_ref, ba_ref, gain_ref, tri_ref, hmask_ref, bdmask_ref, gones_ref,
                    kaug_ref, qaugt_ref, vt_ref, oc_ref, cum_ref, st_ref):
    @pl.when(pl.program_id(1) == 0)
    def _():
        cum_ref[...] = jnp.zeros_like(cum_ref)
        st_ref[...] = jnp.zeros_like(st_ref)

    _fox_prep_body(fox_ref, small_ref, bf_ref, qg_ref, kg_ref, fones_ref, rq_ref, rk_ref,
                   kaug_ref, qaugt_ref, vt_ref, cum_ref)
    _gla_body(gla_ref, small_ref, wa_ref, ba_ref, gain_ref, tri_ref, hmask_ref, bdmask_ref,
              gones_ref, oc_ref, st_ref)


def _fox_prep_gla(fox, gla, small, fox_p, gla_p, layer, tm, tk):
    B, S, _ = fox.shape
    H, KW = FOX_HEADS, GLA_KW
    tok = lambda n: pl.BlockSpec((1, tm, n), lambda b, i: (b, i, 0))
    specs = lambda stacked, consts: ([_layer_spec(p.shape, layer) for p in stacked]
                                     + [_const_spec(p.shape) for p in consts])
    fox_specs, gla_specs = specs(*fox_p), specs(*gla_p)
    fox_p, gla_p = fox_p[0] + fox_p[1], gla_p[0] + gla_p[1]
    return pl.pallas_call(
        _fox_gla_kernel,
        grid=(B, S // tm),
        in_specs=[tok(FOX_COLS), tok(GLA_COLS), tok(SMALL_COLS)] + fox_specs + gla_specs,
        out_specs=[
            pl.BlockSpec((1, H, tm, AUG), lambda b, i: (b, 0, i, 0)),
            pl.BlockSpec((1, H, AUG, tm), lambda b, i: (b, 0, 0, i)),
            pl.BlockSpec((1, H, tm // tk, V_ROWS, tk), lambda b, i: (b, 0, i, 0, 0)),
            tok(KW),
        ],
        out_shape=[
            jax.ShapeDtypeStruct((B, H, S, AUG), BF16),
            jax.ShapeDtypeStruct((B, H, AUG, S), BF16),
            jax.ShapeDtypeStruct((B, H, S // tk, V_ROWS, tk), BF16),
            jax.ShapeDtypeStruct((B, S, KW), BF16),
        ],
        scratch_shapes=[pltpu.VMEM((SUBLANES, LANES), F32), pltpu.VMEM((KW, KW), F32)],
        compiler_params=_cparams(("arbitrary", "arbitrary")),
        name="fox_prep_gla",
    )(fox, gla, small, *fox_p, *gla_p)


def _ffn_chunks(d_ff):
    chunks, c0 = [], 0
    while c0 < d_ff:
        n = min(512, d_ff - c0)
        chunks.append((c0, n))
        c0 += n
    return chunks


def _out_ffn_kernel(x_ref, a_ref, b_ref, c_ref, mod_ref, g_ref, wo_ref, wgu_ref, wd_ref, o_ref,
                    acc_ref):
    d_ff = wd_ref.shape[0]
    mix = _dot(jnp.concatenate([a_ref[0], b_ref[0], c_ref[0]], axis=1), wo_ref[...])
    x1 = x_ref[0] + mod_ref[0, 2:3, :] * mix
    h = _rms_mod(x1, g_ref[...], mod_ref[0, 4:5, :], mod_ref[0, 3:4, :]).astype(BF16)
    for idx, (c0, n) in enumerate(_ffn_chunks(d_ff)):
        gt = _dot(h, wgu_ref[:, c0:c0 + n])
        up = _dot(h, wgu_ref[:, d_ff + c0:d_ff + c0 + n])
        act = (gt * _sigmoid(gt) * up).astype(BF16)
        y = _dot(act, wd_ref[c0:c0 + n, :])
        if idx == 0:
            acc_ref[...] = y
        else:
            acc_ref[...] += y
    o_ref[0] = x1 + mod_ref[0, 5:6, :] * acc_ref[...]


def _out_ffn(x, a, b, c, mod, gain, wo, wgu, wd, layer, tm):
    B, S, D = x.shape
    tok = lambda n: pl.BlockSpec((1, tm, n), lambda bb, i: (bb, i, 0))
    return pl.pallas_call(
        _out_ffn_kernel,
        grid=(B, S // tm),
        in_specs=[
            tok(D), tok(a.shape[2]), tok(b.shape[2]), tok(c.shape[2]),
            _mod_spec(mod.shape, layer), _layer_spec(gain.shape, layer),
            _layer_weight_spec(wo.shape, layer), _layer_weight_spec(wgu.shape, layer),
            _layer_weight_spec(wd.shape, layer),
        ],
        out_specs=tok(D),
        out_shape=jax.ShapeDtypeStruct((B, S, D), F32),
        scratch_shapes=[pltpu.VMEM((tm, D), F32)],
        compiler_params=_cparams(("arbitrary", "arbitrary")),
        name="out_proj_ffn",
    )(x, a, b, c, mod, gain, wo, wgu, wd)


def _block_diag(w):
    L, n, d, e = w.shape
    eye = jnp.eye(n, dtype=w.dtype)
    return jnp.einsum("lnde,nm->lndme", w, eye).reshape(L, n * d, n * e)


def _regroup_w_in(w_in):
    L, D, _ = w_in.shape
    sizes = (LRU_WIDTH, LRU_WIDTH, FOX_WIDTH, FOX_WIDTH, FOX_WIDTH, FOX_HEADS,
             GLA_KW, GLA_KW, GLA_VW, GLA_RANK, GLA_VW)
    offs = np.concatenate([[0], np.cumsum(sizes)])
    w = w_in.astype(BF16)
    seg = lambda i: w[:, :, offs[i]:offs[i + 1]]
    z = lambda n: jnp.zeros((L, D, n), BF16)
    q_scale = GLA_DK ** -0.5
    assert np.log2(q_scale) == np.round(np.log2(q_scale))
    cols = [seg(0), seg(1), seg(2), seg(3), seg(4), seg(6) * q_scale, seg(7), seg(8), seg(10),
            seg(5), z(GLOW_LANE0 - FOX_HEADS), seg(9), z(SMALL_COLS - GLOW_LANE0 - GLA_RANK)]
    return jnp.concatenate(cols, axis=-1)


def _head_ones(n_heads, dim):
    assert dim & (dim - 1) == 0
    return jnp.asarray(np.kron(np.eye(n_heads), np.full((dim, dim), 1.0 / dim)), BF16)


def _fox_constants():
    H, W = FOX_HEADS, FOX_WIDTH
    rq = np.zeros((LANES, H * AUG), np.float32)
    rk = np.zeros((LANES, H * AUG), np.float32)
    for h in range(H):
        for s in range(N_SPLIT):
            rq[s * H + h, h * AUG + HEAD_DIM + s] = 1.0
            rq[ONE_LANE, h * AUG + HEAD_DIM + N_SPLIT + s] = 1.0
            rk[ONE_LANE, h * AUG + HEAD_DIM + s] = 1.0
            rk[s * H + h, h * AUG + HEAD_DIM + N_SPLIT + s] = -1.0
    return jnp.asarray(rq, BF16), jnp.asarray(rk, BF16)


def _gla_constants():
    R, C, KW = GLA_ROWS, GLA_CHUNK, GLA_KW
    i = np.arange(R)
    tri = ((i[:, None] >= i[None, :]) & (i[:, None] // C == i[None, :] // C)).astype(np.float32)
    lane = np.arange(KW)
    hmask = (lane[None, :] // GLA_DK == np.arange(GLA_HEADS)[:, None]).astype(np.float32)
    bdmask = (lane[:, None] // GLA_DK == lane[None, :] // GLA_DK).astype(np.float32)
    return jnp.asarray(tri, BF16), jnp.asarray(hmask, F32), jnp.asarray(bdmask, F32)


def _pick_tile(n, want):
    t = min(n, want)
    while n % t:
        t //= 2
    return t


def kernel(x, c, norm1_gain, norm2_gain, w_mod, b_mod, w_in, conv_w, conv_b, lru_w_r, lru_b_r,
           lru_w_i, lru_b_i, lru_lambda, fox_b_f, fox_q_gain, fox_k_gain, gla_w_alpha, gla_b_alpha,
           gla_out_gain, w_out, ffn_w_gate_up, ffn_w_down):
    B, S, D = x.shape
    L = w_in.shape[0]
    tm = _pick_tile(S, 1024)
    tp = _pick_tile(S, 1024)
    tq = _pick_tile(S, 256)

    mod = _modulation(c, w_mod, b_mod)
    w_in_p = _regroup_w_in(w_in)
    wr_bd = _block_diag(lru_w_r).astype(BF16)
    wi_bd = _block_diag(lru_w_i).astype(BF16)
    w_out_b = w_out.astype(BF16)
    wgu_b = ffn_w_gate_up.astype(BF16)
    wd_b = ffn_w_down.astype(BF16)
    row = lambda a: a.reshape(L, 1, a.shape[-1])
    bf_pad = jnp.pad(fox_b_f, ((0, 0), (0, LANES - FOX_HEADS))).reshape(L, 1, LANES)
    qg = jnp.tile(fox_q_gain * (HEAD_DIM ** -0.5 * LOG2E), (1, FOX_HEADS)).reshape(L, 1, FOX_WIDTH)
    kg = jnp.tile(fox_k_gain, (1, FOX_HEADS)).reshape(L, 1, FOX_WIDTH)
    wa_pad = jnp.pad(gla_w_alpha, ((0, 0), (GLOW_LANE0, LANES - GLOW_LANE0 - GLA_RANK), (0, 0))).astype(BF16)
    og = jnp.tile(gla_out_gain, (1, GLA_HEADS)).reshape(L, 1, GLA_VW)
    fox_ones = _head_ones(FOX_HEADS, HEAD_DIM)
    gla_ones = _head_ones(GLA_HEADS, GLA_DK)
    rq, rk = _fox_constants()
    tri, hmask, bdmask = _gla_constants()

    g1, g2 = row(norm1_gain), row(norm2_gain)
    lru_p = (conv_w, row(conv_b), wr_bd, row(lru_b_r), wi_bd, row(lru_b_i), row(lru_lambda))
    fox_p = ((bf_pad, qg, kg), (fox_ones, rq, rk))
    gla_p = ((wa_pad, row(gla_b_alpha), og), (tri, hmask, bdmask, gla_ones))
    for l in range(L):
        out_a, fox, gla, small = _in_proj(x, mod, g1, w_in_p, lru_p, l, tp)
        kaug, qaugt, vt, out_c = _fox_prep_gla(fox, gla, small, fox_p, gla_p, l, tp, tq)
        out_b = _fox_attention(kaug, qaugt, vt, tq)
        x = _out_ffn(x, out_a, out_b, out_c, mod, g2, w_out_b, wgu_b, wd_b, l, tm)
    return x
```

```python
import functools

import numpy as np
import jax
import jax.numpy as jnp
from jax import lax
from jax.experimental import pallas as pl
from jax.experimental.pallas import tpu as pltpu

F32 = jnp.float32
BF16 = jnp.bfloat16

EPS = 1e-6
HEAD_DIM = 64
LRU_WIDTH = 384
CONV_WIDTH = 4
LRU_C = 8.0
FOX_HEADS = 6
FOX_WIDTH = FOX_HEADS * HEAD_DIM
GLA_HEADS = 4
GLA_DK = 64
GLA_KW = GLA_HEADS * GLA_DK
GLA_VW = GLA_KW
GLA_RANK = 16
GLA_TAU = 16.0
GLA_CHUNK = 64
N_MOD = 6

LANES = 128
SUBLANES = 8
BF16_ROWS = 16
VMEM_LIMIT_BYTES = 56 * 1024 * 1024
FFN_CHUNK = 512

LRU_COLS = 2 * LRU_WIDTH
FOX_COLS = 3 * FOX_WIDTH
GLA_COLS = 3 * GLA_KW + GLA_VW
SMALL_COLS = LANES
GLOW_LANE0 = 8
NP_COLS = LRU_COLS + FOX_COLS + GLA_COLS + SMALL_COLS
MXU_COLS = 256
LRU_ROWS = 128

AUG = LANES
N_SPLIT = 3
ONE_LANE = N_SPLIT * FOX_HEADS
NEG_BIG = -1e30
V_ROWS = HEAD_DIM + BF16_ROWS
NQ_SUB = 2
LOG2E = float(np.log2(np.e))


def _sigmoid(x):
    return 0.5 * jnp.tanh(0.5 * x) + 0.5


def _log_sigmoid(x):
    return jnp.minimum(x, 0.0) - jnp.log(1.0 + jnp.exp(-jnp.abs(x)))


def _softplus(x):
    return jnp.maximum(x, 0.0) + jnp.log(1.0 + jnp.exp(-jnp.abs(x)))


def _gelu_tanh(x):
    c = np.float32(np.sqrt(2.0 / np.pi))
    return 0.5 * x * (1.0 + jnp.tanh(c * (x + 0.044715 * (x * x * x))))


def _dot(a, b):
    return jnp.dot(a, b, preferred_element_type=F32)


def _cparams(sem):
    return pltpu.CompilerParams(dimension_semantics=sem, vmem_limit_bytes=VMEM_LIMIT_BYTES)


def _const_spec(shape):
    nd = len(shape)
    return pl.BlockSpec(shape, lambda *_: (0,) * nd)


def _layer_spec(stacked_shape, layer):
    tail = tuple(stacked_shape[1:])
    return pl.BlockSpec((None,) + tail, lambda *_: (layer,) + (0,) * len(tail))


def _layer_weight_spec(stacked_shape, layer):
    tail = tuple(stacked_shape[1:])
    return pl.BlockSpec((None,) + tail, lambda *_: (layer,) + (0,) * len(tail),
                        pipeline_mode=pl.Buffered(1))


def _mod_spec(mod_shape, layer):
    return pl.BlockSpec((None, 1) + tuple(mod_shape[2:]), lambda b, i: (layer, b, 0, 0))


def _mod_kernel(c_ref, w_ref, b_ref, o_ref):
    c = c_ref[...]
    ca = (c * _sigmoid(c)).astype(BF16)
    o_ref[0] = _dot(ca, w_ref[0].astype(BF16)) + b_ref[0]


def _modulation(c, w_mod, b_mod):
    L, D, N = w_mod.shape
    B = c.shape[0]
    bp = -(-B // SUBLANES) * SUBLANES
    cp = jnp.pad(c, ((0, bp - B), (0, 0)))
    tn = 1024
    out = pl.pallas_call(
        _mod_kernel,
        grid=(L, N // tn),
        in_specs=[
            pl.BlockSpec((bp, D), lambda l, n: (0, 0)),
            pl.BlockSpec((1, D, tn), lambda l, n: (l, 0, n)),
            pl.BlockSpec((1, 1, tn), lambda l, n: (l, 0, n)),
        ],
        out_specs=pl.BlockSpec((1, bp, tn), lambda l, n: (l, 0, n)),
        out_shape=jax.ShapeDtypeStruct((L, bp, N), F32),
        compiler_params=_cparams(("arbitrary", "arbitrary")),
        name="adaln_mod",
    )(cp, w_mod, b_mod.reshape(L, 1, N))
    return out.reshape(L, bp, N_MOD, D)


def _rms_mod(x, gain, scale, shift):
    ms = jnp.mean(x * x, axis=-1, keepdims=True)
    y = x * lax.rsqrt(ms + EPS) * gain
    return y * (1.0 + scale) + shift


def _in_proj_kernel(x_ref, mod_ref, g_ref, w_ref, cw_ref, cb_ref, wr_ref, br_ref, wi_ref, bi_ref,
                    lam_ref, oa_ref, fox_ref, gla_ref, small_ref, xs_ref, h_ref):
    tm, W = x_ref.shape[1], LRU_WIDTH

    @pl.when(pl.program_id(1) == 0)
    def _():
        xs_ref[0:SUBLANES, :] = jnp.zeros((SUBLANES, W), F32)
        h_ref[...] = jnp.zeros_like(h_ref)

    h = _rms_mod(x_ref[0], g_ref[...], mod_ref[0, 1:2, :], mod_ref[0, 0:1, :]).astype(BF16)
    segs, c0 = [], LRU_COLS
    for ref, n in ((fox_ref, FOX_COLS), (gla_ref, GLA_COLS), (small_ref, SMALL_COLS)):
        segs.append((ref, c0, c0 + n))
        c0 += n

    def project(g0):
        g1 = g0 + MXU_COLS
        y = _dot(h, w_ref[:, g0:g1])
        for ref, s0, s1 in segs:
            lo, hi = max(g0, s0), min(g1, s1)
            if lo < hi:
                ref[0, :, lo - s0:hi - s0] = y[:, lo - g0:hi - g0].astype(ref.dtype)

    y0 = _dot(h, w_ref[:, 0:LRU_COLS])
    xs_ref[SUBLANES:SUBLANES + tm, :] = y0[:, 0:W]
    pieces = [(s, r0) for s in range(W // LANES) for r0 in range(0, tm, LRU_ROWS)]
    gates = [_lru_gates(y0[r0:r0 + LRU_ROWS, s * LANES:(s + 1) * LANES], r0, slice(s * LANES, (s + 1) * LANES),
                        cw_ref, cb_ref, wr_ref, wi_ref, xs_ref) for s, r0 in pieces]
    xs_ref[0:SUBLANES, :] = xs_ref[tm:tm + SUBLANES, :]
    chunks = list(range(LRU_COLS, NP_COLS, MXU_COLS))
    carry = None
    for (s, r0), g in zip(pieces, gates):
        sl = slice(s * LANES, (s + 1) * LANES)
        rows = slice(r0, r0 + LRU_ROWS)
        if chunks:
            project(chunks.pop(0))
        if r0 == 0:
            carry = h_ref[0:1, sl]
        out, carry = _lru_finish(*g, y0[rows, W + s * LANES:W + (s + 1) * LANES], sl, carry,
                                 br_ref, bi_ref, lam_ref)
        oa_ref[0, rows, sl] = out.astype(oa_ref.dtype)
        if r0 + LRU_ROWS == tm:
            h_ref[:, sl] = jnp.broadcast_to(carry, (h_ref.shape[0], LANES))
    for g0 in chunks:
        project(g0)


def _in_proj(x, mod, gain, w, lru_p, layer, tm):
    B, S, D = x.shape
    W = LRU_WIDTH
    tok = lambda n: pl.BlockSpec((1, tm, n), lambda b, i: (b, i, 0))
    return pl.pallas_call(
        _in_proj_kernel,
        grid=(B, S // tm),
        in_specs=[tok(D), _mod_spec(mod.shape, layer), _layer_spec(gain.shape, layer),
                  _layer_weight_spec(w.shape, layer)] + [_layer_spec(p.shape, layer) for p in lru_p],
        out_specs=[tok(W), tok(FOX_COLS), tok(GLA_COLS), tok(SMALL_COLS)],
        out_shape=[
            jax.ShapeDtypeStruct((B, S, W), BF16),
            jax.ShapeDtypeStruct((B, S, FOX_COLS), BF16),
            jax.ShapeDtypeStruct((B, S, GLA_COLS), BF16),
            jax.ShapeDtypeStruct((B, S, SMALL_COLS), F32),
        ],
        scratch_shapes=[pltpu.VMEM((tm + SUBLANES, W), F32), pltpu.VMEM((SUBLANES, W), F32)],
        compiler_params=_cparams(("arbitrary", "arbitrary")),
        name="in_proj_lru",
    )(x, mod, gain, w, *lru_p)


def _scan_add(v):
    n = v.shape[0]
    row = lax.broadcasted_iota(jnp.int32, v.shape, 0)
    d = 1
    while d < n:
        v = v + jnp.where(row >= d, pltpu.roll(v, d, axis=0), 0.0)
        d *= 2
    return v


def _scan_linear(a, u, h0):
    n, w = a.shape
    g = n // SUBLANES
    a = a.reshape(g, SUBLANES, w)
    u = u.reshape(g, SUBLANES, w)
    sub = lax.broadcasted_iota(jnp.int32, a.shape, 1)
    d = 1
    while d < SUBLANES:
        m = sub >= d
        u = u + jnp.where(m, a * pltpu.roll(u, d, axis=1), 0.0)
        a = jnp.where(m, a * pltpu.roll(a, d, axis=1), a)
        d *= 2
    hs, h = [], h0
    for v in range(g):
        hv = a[v] * h + u[v]
        h = hv[SUBLANES - 1:SUBLANES, :]
        hs.append(hv)
    return jnp.concatenate(hs, axis=0), h


def _lru_gates(xa, r0, sl, cw_ref, cb_ref, wr_ref, wi_ref, xs_ref):
    n = xa.shape[0]
    xc = xa * cw_ref[CONV_WIDTH - 1:CONV_WIDTH, sl] + cb_ref[:, sl]
    for k in range(1, CONV_WIDTH):
        j = CONV_WIDTH - 1 - k
        xc = xc + xs_ref[pl.ds(SUBLANES - k + r0, n), sl] * cw_ref[j:j + 1, sl]
    xcb = xc.astype(BF16)
    return xc, _dot(xcb, wr_ref[sl, sl]), _dot(xcb, wi_ref[sl, sl])


def _lru_finish(xc, r_pre, i_pre, ya, sl, h0, br_ref, bi_ref, lam_ref):
    r = _sigmoid(r_pre + br_ref[:, sl])
    i = _sigmoid(i_pre + bi_ref[:, sl])
    a = jnp.exp2(r * ((-LRU_C * LOG2E) * _softplus(-lam_ref[:, sl])))
    mult = jnp.sqrt(1.0 - a * a)
    u = mult * (i * xc)
    h, h_last = _scan_linear(a, u, h0)
    return h * _gelu_tanh(ya), h_last


def _fox_prep_body(fox_ref, small_ref, bf_ref, qg_ref, kg_ref, ones_ref,
                   rq_ref, rk_ref, kaug_ref, qaugt_ref, vt_ref, cum_ref):
    tm = fox_ref.shape[1]
    W = FOX_WIDTH
    q = fox_ref[0, :, 0:W].astype(F32)
    k = fox_ref[0, :, W:2 * W].astype(F32)
    v = fox_ref[0, :, 2 * W:3 * W].astype(F32)

    def head_norm(t, gain):
        ms = _dot((t * t).astype(BF16), ones_ref[...])
        return t * lax.rsqrt(ms + EPS) * gain

    qn = head_norm(q, qg_ref[...])
    kn = head_norm(k, kg_ref[...])

    lf = _log_sigmoid(small_ref[0] + bf_ref[...])
    cum = _scan_add(lf) + cum_ref[0:1, :]
    cum_ref[...] = jnp.broadcast_to(cum[tm - 1:tm, :], cum_ref.shape)

    cs = cum * LOG2E
    c1 = cs.astype(BF16).astype(F32)
    r1 = cs - c1
    c2 = r1.astype(BF16).astype(F32)
    c3 = (r1 - c2).astype(BF16).astype(F32)
    lane = lax.broadcasted_iota(jnp.int32, cum.shape, 1)
    H = FOX_HEADS
    packed = jnp.where(lane < H, c1, 0.0)
    packed = jnp.where((lane >= H) & (lane < 2 * H), pltpu.roll(c2, H, axis=1), packed)
    packed = jnp.where((lane >= 2 * H) & (lane < 3 * H), pltpu.roll(c3, 2 * H, axis=1), packed)
    packed = jnp.where(lane == ONE_LANE, 1.0, packed).astype(BF16)

    kbias = _dot(packed, rk_ref[...])
    qbias = _dot(packed, rq_ref[...])
    feat = lane < HEAD_DIM

    def head_aug(t, bias, h):
        src = t[:, (h // 2) * LANES:(h // 2 + 1) * LANES]
        if h % 2:
            src = pltpu.roll(src, HEAD_DIM, axis=1)
        return jnp.where(feat, src, bias[:, h * AUG:(h + 1) * AUG])

    vt = v.T
    tk = vt_ref.shape[4]
    for h in range(H):
        kaug_ref[0, h] = head_aug(kn, kbias, h).astype(BF16)
        qaugt_ref[0, h] = head_aug(qn, qbias, h).T.astype(BF16)
        for cb in range(tm // tk):
            vt_ref[0, h, cb, 0:HEAD_DIM, :] = (
                vt[h * HEAD_DIM:(h + 1) * HEAD_DIM, cb * tk:(cb + 1) * tk].astype(BF16))
            vt_ref[0, h, cb, HEAD_DIM:V_ROWS, :] = jnp.ones((V_ROWS - HEAD_DIM, tk), BF16)


def _fox_attn_kernel(k_ref, qt_ref, vt_ref, o_ref, acc_ref, sa_ref, sb_ref, *, tk):
    qi = pl.program_id(1)
    units = [(h, c) for h in range(FOX_HEADS) for c in range(NQ_SUB)]
    acc_ref[...] = jnp.zeros_like(acc_ref)

    def qk(h, c, j):
        start = pl.multiple_of(j * tk, tk)
        return _dot(k_ref[0, h, pl.ds(start, tk), :], qt_ref[0, h, :, c * tk:(c + 1) * tk])

    def update(h, c, j, s, m_old, m_new):
        p = jnp.exp2(s - m_new).astype(BF16)
        acc_ref[h, c] = jnp.exp2(m_old - m_new) * acc_ref[h, c] + _dot(vt_ref[0, h, j], p)

    def step(j, cur_ref, c_cur, nxt_ref, ms):
        new_ms, c_nxt = [], []
        for i, (h, c) in enumerate(units):
            s_n = qk(h, c, j + 1)
            nxt_ref[h, c] = s_n
            c_nxt.append(jnp.max(s_n, axis=0, keepdims=True))
            m_new = jnp.maximum(ms[i], c_cur[i])
            new_ms.append(m_new)
            update(h, c, j, cur_ref[h, c], ms[i], m_new)
        return tuple(new_ms), tuple(c_nxt)

    def pair(t, carry):
        ms, ca = carry
        ms, cb = step(2 * t, sa_ref, ca, sb_ref, ms)
        ms, ca = step(2 * t + 1, sb_ref, cb, sa_ref, ms)
        return ms, ca

    c0 = []
    for h, c in units:
        s = qk(h, c, 0)
        sa_ref[h, c] = s
        c0.append(jnp.max(s, axis=0, keepdims=True))
    carry = (tuple(jnp.full((1, tk), NEG_BIG, F32) for _ in units), tuple(c0))
    assert NQ_SUB % 2 == 0
    n_pairs = (NQ_SUB // 2) * qi
    n_quads = lax.div(n_pairs, 2)
    carry = lax.fori_loop(0, n_quads, lambda u, cr: pair(2 * u + 1, pair(2 * u, cr)), carry)
    carry = lax.fori_loop(2 * n_quads, n_pairs, pair, carry)

    diag = (lax.broadcasted_iota(jnp.int32, (tk, tk), 0)
            <= lax.broadcasted_iota(jnp.int32, (tk, tk), 1))
    ms, cmax = list(carry[0]), list(carry[1])
    bufs = (sa_ref, sb_ref)
    for d in range(NQ_SUB):
        j = NQ_SUB * qi + d
        cur_ref, nxt_ref = bufs[d % 2], bufs[(d + 1) % 2]
        for i, (h, c) in enumerate(units):
            if c < d:
                continue
            if c > d:
                s_n = qk(h, c, j + 1)
                nxt_ref[h, c] = s_n
                c_next = jnp.max(s_n, axis=0, keepdims=True)
            if c == d:
                s = jnp.where(diag, cur_ref[h, c], NEG_BIG)
                m_new = jnp.maximum(ms[i], jnp.max(s, axis=0, keepdims=True))
            else:
                s = cur_ref[h, c]
                m_new = jnp.maximum(ms[i], cmax[i])
                cmax[i] = c_next
            update(h, c, j, s, ms[i], m_new)
            ms[i] = m_new

    for c in range(NQ_SUB):
        out = jnp.concatenate(
            [acc_ref[h, c, 0:HEAD_DIM, :] * (1.0 / acc_ref[h, c, HEAD_DIM:HEAD_DIM + 1, :])
             for h in range(FOX_HEADS)], axis=0)
        o_ref[0, c * tk:(c + 1) * tk, :] = out.T.astype(o_ref.dtype)


def _fox_attention(kaug, qaugt, vt, tk):
    B, H, S, _ = kaug.shape
    tq = NQ_SUB * tk
    assert S % tq == 0
    return pl.pallas_call(
        functools.partial(_fox_attn_kernel, tk=tk),
        grid=(B, S // tq),
        in_specs=[
            pl.BlockSpec((1, H, S, AUG), lambda b, i: (b, 0, 0, 0)),
            pl.BlockSpec((1, H, AUG, tq), lambda b, i: (b, 0, 0, i)),
            pl.BlockSpec((1, H, S // tk, V_ROWS, tk), lambda b, i: (b, 0, 0, 0, 0)),
        ],
        out_specs=pl.BlockSpec((1, tq, H * HEAD_DIM), lambda b, i: (b, i, 0)),
        out_shape=jax.ShapeDtypeStruct((B, S, H * HEAD_DIM), BF16),
        scratch_shapes=[pltpu.VMEM((H, NQ_SUB, V_ROWS, tk), F32), pltpu.VMEM((H, NQ_SUB, tk, tk), F32),
                        pltpu.VMEM((H, NQ_SUB, tk, tk), F32)],
        compiler_params=_cparams(("arbitrary", "arbitrary")),
        name="fox_attention",
    )(kaug, qaugt, vt)


GLA_ROWS = 256


def _gla_body(gla_ref, small_ref, wa_ref, ba_ref, gain_ref, tri_ref, hmask_ref, bdmask_ref,
              ones_ref, o_ref, st_ref):
    ts = o_ref.shape[1]
    KW, C, R = GLA_KW, GLA_CHUNK, GLA_ROWS
    q = gla_ref[0, :, 0:KW].astype(F32)
    k = gla_ref[0, :, KW:2 * KW].astype(F32)
    v = gla_ref[0, :, 2 * KW:3 * KW]
    low = small_ref[0].astype(BF16)
    la = _log_sigmoid(_dot(low, wa_ref[...]) + ba_ref[...]) * (LOG2E / GLA_TAU)
    la_hi = la.astype(BF16)
    la_lo = (la - la_hi.astype(F32)).astype(BF16)
    tri = tri_ref[...]
    bcum = jnp.concatenate(
        [_dot(tri, la_hi[g0:g0 + R]) + _dot(tri, la_lo[g0:g0 + R]) for g0 in range(0, ts, R)], axis=0)
    ends = [bcum[c0 + C - 1:c0 + C, :] for c0 in range(0, ts, C)]
    b_last = jnp.concatenate([jnp.broadcast_to(e, (C, KW)) for e in ends], axis=0)
    q_dec = (q * jnp.exp2(bcum)).astype(BF16)
    k_dec = (k * jnp.exp2(-bcum)).astype(BF16)
    k_end = (k * jnp.exp2(b_last - bcum)).astype(BF16)

    o_groups = []
    for g0 in range(0, ts, R):
        rows = slice(g0, g0 + R)
        o_acc = jnp.zeros((R, KW), F32)
        for h in range(GLA_HEADS):
            hm = hmask_ref[h:h + 1, :].astype(BF16)
            att = lax.dot_general(q_dec[rows] * hm, k_dec[rows], (((1,), (1,)), ((), ())),
                                  preferred_element_type=F32)
            o_acc = o_acc + _dot(att.astype(BF16) * tri, v[rows] * hm)
        o_groups.append(o_acc)

    o_parts = []
    for ci, c0 in enumerate(range(0, ts, C)):
        cs = slice(c0, c0 + C)
        st = st_ref[...]
        o_parts.append(lax.dot_general(q_dec[cs], st.astype(BF16), (((1,), (1,)), ((), ())),
                                       preferred_element_type=F32))
        kv_t = lax.dot_general(v[cs], k_end[cs], (((0,), (0,)), ((), ())),
                               preferred_element_type=F32)
        st_ref[...] = st * jnp.exp2(ends[ci]) + kv_t * bdmask_ref[...]
    o = jnp.concatenate(o_groups, axis=0) + jnp.concatenate(o_parts, axis=0)

    ms = _dot((o * o).astype(BF16), ones_ref[...])
    o = o * lax.rsqrt(ms + EPS) * gain_ref[...]
    g = gla_ref[0, :, 3 * KW:4 * KW].astype(F32)
    o_ref[0] = (o * (g * _sigmoid(g))).astype(o_ref.dtype)


def _fox_gla_kernel(fox_ref, gla_ref, small_ref, bf_ref, qg_ref, kg_ref, fones_ref, rq_ref, rk_ref,
                    wa_ref, ba_ref, gain_ref, tri_ref, hmask_ref, bdmask_ref, gones_ref,
                    kaug_ref, qaugt_ref, vt_ref, oc_ref, cum_ref, st_ref):
    @pl.when(pl.program_id(1) == 0)
    def _():
        cum_ref[...] = jnp.zeros_like(cum_ref)
        st_ref[...] = jnp.zeros_like(st_ref)

    _fox_prep_body(fox_ref, small_ref, bf_ref, qg_ref, kg_ref, fones_ref, rq_ref, rk_ref,
                   kaug_ref, qaugt_ref, vt_ref, cum_ref)
    _gla_body(gla_ref, small_ref, wa_ref, ba_ref, gain_ref, tri_ref, hmask_ref, bdmask_ref,
              gones_ref, oc_ref, st_ref)


def _fox_prep_gla(fox, gla, small, fox_p, gla_p, layer, tm, tk):
    B, S, _ = fox.shape
    H, KW = FOX_HEADS, GLA_KW
    tok = lambda n: pl.BlockSpec((1, tm, n), lambda b, i: (b, i, 0))
    specs = lambda stacked, consts: ([_layer_spec(p.shape, layer) for p in stacked]
                                     + [_const_spec(p.shape) for p in consts])
    fox_specs, gla_specs = specs(*fox_p), specs(*gla_p)
    fox_p, gla_p = fox_p[0] + fox_p[1], gla_p[0] + gla_p[1]
    return pl.pallas_call(
        _fox_gla_kernel,
        grid=(B, S // tm),
        in_specs=[tok(FOX_COLS), tok(GLA_COLS), tok(SMALL_COLS)] + fox_specs + gla_specs,
        out_specs=[
            pl.BlockSpec((1, H, tm, AUG), lambda b, i: (b, 0, i, 0)),
            pl.BlockSpec((1, H, AUG, tm), lambda b, i: (b, 0, 0, i)),
            pl.BlockSpec((1, H, tm // tk, V_ROWS, tk), lambda b, i: (b, 0, i, 0, 0)),
            tok(KW),
        ],
        out_shape=[
            jax.ShapeDtypeStruct((B, H, S, AUG), BF16),
            jax.ShapeDtypeStruct((B, H, AUG, S), BF16),
            jax.ShapeDtypeStruct((B, H, S // tk, V_ROWS, tk), BF16),
            jax.ShapeDtypeStruct((B, S, KW), BF16),
        ],
        scratch_shapes=[pltpu.VMEM((SUBLANES, LANES), F32), pltpu.VMEM((KW, KW), F32)],
        compiler_params=_cparams(("arbitrary", "arbitrary")),
        name="fox_prep_gla",
    )(fox, gla, small, *fox_p, *gla_p)


def _ffn_chunks(d_ff):
    chunks, c0 = [], 0
    while c0 < d_ff:
        n = min(FFN_CHUNK, d_ff - c0)
        chunks.append((c0, n))
        c0 += n
    return chunks


def _out_ffn_kernel(x_ref, a_ref, b_ref, c_ref, mod_ref, g_ref, wo_ref, wgu_ref, wd_ref, o_ref,
                    acc_ref):
    d_ff = wd_ref.shape[0]
    mix = _dot(jnp.concatenate([a_ref[0], b_ref[0], c_ref[0]], axis=1), wo_ref[...])
    x1 = x_ref[0] + mod_ref[0, 2:3, :] * mix
    h = _rms_mod(x1, g_ref[...], mod_ref[0, 4:5, :], mod_ref[0, 3:4, :]).astype(BF16)
    for idx, (c0, n) in enumerate(_ffn_chunks(d_ff)):
        gt = _dot(h, wgu_ref[:, c0:c0 + n])
        up = _dot(h, wgu_ref[:, d_ff + c0:d_ff + c0 + n])
        act = (gt * _sigmoid(gt) * up).astype(BF16)
        y = _dot(act, wd_ref[c0:c0 + n, :])
        if idx == 0:
            acc_ref[...] = y
        else:
            acc_ref[...] += y
    o_ref[0] = x1 + mod_ref[0, 5:6, :] * acc_ref[...]


def _out_ffn(x, a, b, c, mod, gain, wo, wgu, wd, layer, tm):
    B, S, D = x.shape
    tok = lambda n: pl.BlockSpec((1, tm, n), lambda bb, i: (bb, i, 0))
    return pl.pallas_call(
        _out_ffn_kernel,
        grid=(B, S // tm),
        in_specs=[
            tok(D), tok(a.shape[2]), tok(b.shape[2]), tok(c.shape[2]),
            _mod_spec(mod.shape, layer), _layer_spec(gain.shape, layer),
            _layer_weight_spec(wo.shape, layer), _layer_weight_spec(wgu.shape, layer),
            _layer_weight_spec(wd.shape, layer),
        ],
        out_specs=tok(D),
        out_shape=jax.ShapeDtypeStruct((B, S, D), F32),
        scratch_shapes=[pltpu.VMEM((tm, D), F32)],
        compiler_params=_cparams(("arbitrary", "arbitrary")),
        name="out_proj_ffn",
    )(x, a, b, c, mod, gain, wo, wgu, wd)


def _block_diag(w):
    L, n, d, e = w.shape
    eye = jnp.eye(n, dtype=w.dtype)
    return jnp.einsum("lnde,nm->lndme", w, eye).reshape(L, n * d, n * e)


def _regroup_w_in(w_in):
    L, D, _ = w_in.shape
    sizes = (LRU_WIDTH, LRU_WIDTH, FOX_WIDTH, FOX_WIDTH, FOX_WIDTH, FOX_HEADS,
             GLA_KW, GLA_KW, GLA_VW, GLA_RANK, GLA_VW)
    offs = np.concatenate([[0], np.cumsum(sizes)])
    w = w_in.astype(BF16)
    seg = lambda i: w[:, :, offs[i]:offs[i + 1]]
    z = lambda n: jnp.zeros((L, D, n), BF16)
    q_scale = GLA_DK ** -0.5
    assert np.log2(q_scale) == np.round(np.log2(q_scale))
    cols = [seg(0), seg(1), seg(2), seg(3), seg(4), seg(6) * q_scale, seg(7), seg(8), seg(10),
            seg(5), z(GLOW_LANE0 - FOX_HEADS), seg(9), z(SMALL_COLS - GLOW_LANE0 - GLA_RANK)]
    return jnp.concatenate(cols, axis=-1)


def _head_ones(n_heads, dim):
    assert dim & (dim - 1) == 0
    return jnp.asarray(np.kron(np.eye(n_heads), np.full((dim, dim), 1.0 / dim)), BF16)


def _fox_constants():
    H, W = FOX_HEADS, FOX_WIDTH
    rq = np.zeros((LANES, H * AUG), np.float32)
    rk = np.zeros((LANES, H * AUG), np.float32)
    for h in range(H):
        for s in range(N_SPLIT):
            rq[s * H + h, h * AUG + HEAD_DIM + s] = 1.0
            rq[ONE_LANE, h * AUG + HEAD_DIM + N_SPLIT + s] = 1.0
            rk[ONE_LANE, h * AUG + HEAD_DIM + s] = 1.0
            rk[s * H + h, h * AUG + HEAD_DIM + N_SPLIT + s] = -1.0
    return jnp.asarray(rq, BF16), jnp.asarray(rk, BF16)


def _gla_constants():
    R, C, KW = GLA_ROWS, GLA_CHUNK, GLA_KW
    i = np.arange(R)
    tri = ((i[:, None] >= i[None, :]) & (i[:, None] // C == i[None, :] // C)).astype(np.float32)
    lane = np.arange(KW)
    hmask = (lane[None, :] // GLA_DK == np.arange(GLA_HEADS)[:, None]).astype(np.float32)
    bdmask = (lane[:, None] // GLA_DK == lane[None, :] // GLA_DK).astype(np.float32)
    return jnp.asarray(tri, BF16), jnp.asarray(hmask, F32), jnp.asarray(bdmask, F32)


def _pick_tile(n, want):
    t = min(n, want)
    while n % t:
        t //= 2
    return t


def kernel(x, c, norm1_gain, norm2_gain, w_mod, b_mod, w_in, conv_w, conv_b, lru_w_r, lru_b_r,
           lru_w_i, lru_b_i, lru_lambda, fox_b_f, fox_q_gain, fox_k_gain, gla_w_alpha, gla_b_alpha,
           gla_out_gain, w_out, ffn_w_gate_up, ffn_w_down):
    B, S, D = x.shape
    L = w_in.shape[0]
    tm = _pick_tile(S, 1024)
    tp = _pick_tile(S, 1024)
    tq = _pick_tile(S, 256)

    mod = _modulation(c, w_mod, b_mod)
    w_in_p = _regroup_w_in(w_in)
    wr_bd = _block_diag(lru_w_r).astype(BF16)
    wi_bd = _block_diag(lru_w_i).astype(BF16)
    w_out_b = w_out.astype(BF16)
    wgu_b = ffn_w_gate_up.astype(BF16)
    wd_b = ffn_w_down.astype(BF16)
    row = lambda a: a.reshape(L, 1, a.shape[-1])
    bf_pad = jnp.pad(fox_b_f, ((0, 0), (0, LANES - FOX_HEADS))).reshape(L, 1, LANES)
    qg = jnp.tile(fox_q_gain * (HEAD_DIM ** -0.5 * LOG2E), (1, FOX_HEADS)).reshape(L, 1, FOX_WIDTH)
    kg = jnp.tile(fox_k_gain, (1, FOX_HEADS)).reshape(L, 1, FOX_WIDTH)
    wa_pad = jnp.pad(gla_w_alpha, ((0, 0), (GLOW_LANE0, LANES - GLOW_LANE0 - GLA_RANK), (0, 0))).astype(BF16)
    og = jnp.tile(gla_out_gain, (1, GLA_HEADS)).reshape(L, 1, GLA_VW)
    fox_ones = _head_ones(FOX_HEADS, HEAD_DIM)
    gla_ones = _head_ones(GLA_HEADS, GLA_DK)
    rq, rk = _fox_constants()
    tri, hmask, bdmask = _gla_constants()

    g1, g2 = row(norm1_gain), row(norm2_gain)
    lru_p = (conv_w, row(conv_b), wr_bd, row(lru_b_r), wi_bd, row(lru_b_i), row(lru_lambda))
    fox_p = ((bf_pad, qg, kg), (fox_ones, rq, rk))
    gla_p = ((wa_pad, row(gla_b_alpha), og), (tri, hmask, bdmask, gla_ones))
    for l in range(L):
        out_a, fox, gla, small = _in_proj(x, mod, g1, w_in_p, lru_p, l, tp)
        kaug, qaugt, vt, out_c = _fox_prep_gla(fox, gla, small, fox_p, gla_p, l, tp, tq)
        out_b = _fox_attention(kaug, qaugt, vt, tq)
        x = _out_ffn(x, out_a, out_b, out_c, mod, g2, w_out_b, wgu_b, wd_b, l, tm)
    return x
```

```python
import functools

import numpy as np
import jax
import jax.numpy as jnp
from jax import lax
from jax.experimental import pallas as pl
from jax.experimental.pallas import tpu as pltpu

F32 = jnp.float32
BF16 = jnp.bfloat16

EPS = 1e-6
HEAD_DIM = 64
LRU_WIDTH = 384
CONV_WIDTH = 4
LRU_C = 8.0
FOX_HEADS = 6
FOX_WIDTH = FOX_HEADS * HEAD_DIM
GLA_HEADS = 4
GLA_DK = 64
GLA_KW = GLA_HEADS * GLA_DK
GLA_VW = GLA_KW
GLA_RANK = 16
GLA_TAU = 16.0
GLA_CHUNK = 64
N_MOD = 6

LANES = 128
SUBLANES = 8
BF16_ROWS = 16
VMEM_LIMIT_BYTES = 56 * 1024 * 1024
FFN_CHUNK = 512

LRU_COLS = 2 * LRU_WIDTH
FOX_COLS = 3 * FOX_WIDTH
GLA_COLS = 3 * GLA_KW + GLA_VW
SMALL_COLS = LANES
GLOW_LANE0 = 8
NP_COLS = LRU_COLS + FOX_COLS + GLA_COLS + SMALL_COLS
MXU_COLS = 256
LRU_ROWS = 128

AUG = LANES
N_SPLIT = 3
ONE_LANE = N_SPLIT * FOX_HEADS
NEG_BIG = -1e30
V_ROWS = HEAD_DIM + BF16_ROWS
NQ_SUB = 2
LOG2E = float(np.log2(np.e))


def _sigmoid(x):
    return 0.5 * jnp.tanh(0.5 * x) + 0.5


def _log_sigmoid(x):
    return jnp.minimum(x, 0.0) - jnp.log(1.0 + jnp.exp(-jnp.abs(x)))


def _softplus(x):
    return jnp.maximum(x, 0.0) + jnp.log(1.0 + jnp.exp(-jnp.abs(x)))


def _gelu_tanh(x):
    c = np.float32(np.sqrt(2.0 / np.pi))
    return 0.5 * x * (1.0 + jnp.tanh(c * (x + 0.044715 * (x * x * x))))


def _dot(a, b):
    return jnp.dot(a, b, preferred_element_type=F32)


def _cparams(sem):
    return pltpu.CompilerParams(dimension_semantics=sem, vmem_limit_bytes=VMEM_LIMIT_BYTES)


def _const_spec(shape):
    nd = len(shape)
    return pl.BlockSpec(shape, lambda *_: (0,) * nd)


def _layer_spec(stacked_shape, layer):
    tail = tuple(stacked_shape[1:])
    return pl.BlockSpec((None,) + tail, lambda *_: (layer,) + (0,) * len(tail))


def _layer_weight_spec(stacked_shape, layer):
    tail = tuple(stacked_shape[1:])
    return pl.BlockSpec((None,) + tail, lambda *_: (layer,) + (0,) * len(tail),
                        pipeline_mode=pl.Buffered(1))


def _mod_spec(mod_shape, layer):
    return pl.BlockSpec((None, 1) + tuple(mod_shape[2:]), lambda b, i: (layer, b, 0, 0))


def _mod_kernel(c_ref, w_ref, b_ref, o_ref):
    c = c_ref[...]
    ca = (c * _sigmoid(c)).astype(BF16)
    o_ref[0] = _dot(ca, w_ref[0].astype(BF16)) + b_ref[0]


def _modulation(c, w_mod, b_mod):
    L, D, N = w_mod.shape
    B = c.shape[0]
    bp = -(-B // SUBLANES) * SUBLANES
    cp = jnp.pad(c, ((0, bp - B), (0, 0)))
    tn = 1024
    out = pl.pallas_call(
        _mod_kernel,
        grid=(L, N // tn),
        in_specs=[
            pl.BlockSpec((bp, D), lambda l, n: (0, 0)),
            pl.BlockSpec((1, D, tn), lambda l, n: (l, 0, n)),
            pl.BlockSpec((1, 1, tn), lambda l, n: (l, 0, n)),
        ],
        out_specs=pl.BlockSpec((1, bp, tn), lambda l, n: (l, 0, n)),
        out_shape=jax.ShapeDtypeStruct((L, bp, N), F32),
        compiler_params=_cparams(("arbitrary", "arbitrary")),
        name="adaln_mod",
    )(cp, w_mod, b_mod.reshape(L, 1, N))
    return out.reshape(L, bp, N_MOD, D)


def _rms_mod(x, gain, scale, shift):
    ms = jnp.mean(x * x, axis=-1, keepdims=True)
    y = x * lax.rsqrt(ms + EPS) * gain
    return y * (1.0 + scale) + shift


def _in_proj_kernel(x_ref, mod_ref, g_ref, w_ref, cw_ref, cb_ref, wr_ref, br_ref, wi_ref, bi_ref,
                    lam_ref, oa_ref, fox_ref, gla_ref, small_ref, xs_ref, h_ref):
    tm, W = x_ref.shape[1], LRU_WIDTH

    @pl.when(pl.program_id(1) == 0)
    def _():
        xs_ref[0:SUBLANES, :] = jnp.zeros((SUBLANES, W), F32)
        h_ref[...] = jnp.zeros_like(h_ref)

    h = _rms_mod(x_ref[0], g_ref[...], mod_ref[0, 1:2, :], mod_ref[0, 0:1, :]).astype(BF16)
    segs, c0 = [], LRU_COLS
    for ref, n in ((fox_ref, FOX_COLS), (gla_ref, GLA_COLS), (small_ref, SMALL_COLS)):
        segs.append((ref, c0, c0 + n))
        c0 += n

    def project(g0):
        g1 = g0 + MXU_COLS
        y = _dot(h, w_ref[:, g0:g1])
        for ref, s0, s1 in segs:
            lo, hi = max(g0, s0), min(g1, s1)
            if lo < hi:
                ref[0, :, lo - s0:hi - s0] = y[:, lo - g0:hi - g0].astype(ref.dtype)

    y0 = _dot(h, w_ref[:, 0:LRU_COLS])
    xs_ref[SUBLANES:SUBLANES + tm, :] = y0[:, 0:W]
    pieces = [(s, r0) for s in range(W // LANES) for r0 in range(0, tm, LRU_ROWS)]
    gates = [_lru_gates(y0[r0:r0 + LRU_ROWS, s * LANES:(s + 1) * LANES], r0, slice(s * LANES, (s + 1) * LANES),
                        cw_ref, cb_ref, wr_ref, wi_ref, xs_ref) for s, r0 in pieces]
    xs_ref[0:SUBLANES, :] = xs_ref[tm:tm + SUBLANES, :]
    chunks = list(range(LRU_COLS, NP_COLS, MXU_COLS))
    carry = None
    for (s, r0), g in zip(pieces, gates):
        sl = slice(s * LANES, (s + 1) * LANES)
        rows = slice(r0, r0 + LRU_ROWS)
        if chunks:
            project(chunks.pop(0))
        if r0 == 0:
            carry = h_ref[0:1, sl]
        out, carry = _lru_finish(*g, y0[rows, W + s * LANES:W + (s + 1) * LANES], sl, carry,
                                 br_ref, bi_ref, lam_ref)
        oa_ref[0, rows, sl] = out.astype(oa_ref.dtype)
        if r0 + LRU_ROWS == tm:
            h_ref[:, sl] = jnp.broadcast_to(carry, (h_ref.shape[0], LANES))
    for g0 in chunks:
        project(g0)


def _in_proj(x, mod, gain, w, lru_p, layer, tm):
    B, S, D = x.shape
    W = LRU_WIDTH
    tok = lambda n: pl.BlockSpec((1, tm, n), lambda b, i: (b, i, 0))
    return pl.pallas_call(
        _in_proj_kernel,
        grid=(B, S // tm),
        in_specs=[tok(D), _mod_spec(mod.shape, layer), _layer_spec(gain.shape, layer),
                  _layer_weight_spec(w.shape, layer)] + [_layer_spec(p.shape, layer) for p in lru_p],
        out_specs=[tok(W), tok(FOX_COLS), tok(GLA_COLS), tok(SMALL_COLS)],
        out_shape=[
            jax.ShapeDtypeStruct((B, S, W), BF16),
            jax.ShapeDtypeStruct((B, S, FOX_COLS), BF16),
            jax.ShapeDtypeStruct((B, S, GLA_COLS), BF16),
            jax.ShapeDtypeStruct((B, S, SMALL_COLS), F32),
        ],
        scratch_shapes=[pltpu.VMEM((tm + SUBLANES, W), F32), pltpu.VMEM((SUBLANES, W), F32)],
        compiler_params=_cparams(("arbitrary", "arbitrary")),
        name="in_proj_lru",
    )(x, mod, gain, w, *lru_p)


def _scan_add(v):
    n = v.shape[0]
    row = lax.broadcasted_iota(jnp.int32, v.shape, 0)
    d = 1
    while d < n:
        v = v + jnp.where(row >= d, pltpu.roll(v, d, axis=0), 0.0)
        d *= 2
    return v


def _scan_linear(a, u, h0):
    n, w = a.shape
    g = n // SUBLANES
    a = a.reshape(g, SUBLANES, w)
    u = u.reshape(g, SUBLANES, w)
    sub = lax.broadcasted_iota(jnp.int32, a.shape, 1)
    d = 1
    while d < SUBLANES:
        m = sub >= d
        u = u + jnp.where(m, a * pltpu.roll(u, d, axis=1), 0.0)
        a = jnp.where(m, a * pltpu.roll(a, d, axis=1), a)
        d *= 2
    hs, h = [], h0
    for v in range(g):
        hv = a[v] * h + u[v]
        h = hv[SUBLANES - 1:SUBLANES, :]
        hs.append(hv)
    return jnp.concatenate(hs, axis=0), h


def _lru_gates(xa, r0, sl, cw_ref, cb_ref, wr_ref, wi_ref, xs_ref):
    n = xa.shape[0]
    xc = xa * cw_ref[CONV_WIDTH - 1:CONV_WIDTH, sl] + cb_ref[:, sl]
    for k in range(1, CONV_WIDTH):
        j = CONV_WIDTH - 1 - k
        xc = xc + xs_ref[pl.ds(SUBLANES - k + r0, n), sl] * cw_ref[j:j + 1, sl]
    xcb = xc.astype(BF16)
    return xc, _dot(xcb, wr_ref[sl, sl]), _dot(xcb, wi_ref[sl, sl])


def _lru_finish(xc, r_pre, i_pre, ya, sl, h0, br_ref, bi_ref, lam_ref):
    r = _sigmoid(r_pre + br_ref[:, sl])
    i = _sigmoid(i_pre + bi_ref[:, sl])
    a = jnp.exp2(r * ((-LRU_C * LOG2E) * _softplus(-lam_ref[:, sl])))
    mult = jnp.sqrt(1.0 - a * a)
    u = mult * (i * xc)
    h, h_last = _scan_linear(a, u, h0)
    return h * _gelu_tanh(ya), h_last


def _fox_prep_body(fox_ref, small_ref, bf_ref, qg_ref, kg_ref, ones_ref,
                   rq_ref, rk_ref, kaug_ref, qaugt_ref, vt_ref, cum_ref):
    tm = fox_ref.shape[1]
    W = FOX_WIDTH
    q = fox_ref[0, :, 0:W].astype(F32)
    k = fox_ref[0, :, W:2 * W].astype(F32)
    v = fox_ref[0, :, 2 * W:3 * W].astype(F32)

    def head_norm(t, gain):
        ms = _dot((t * t).astype(BF16), ones_ref[...])
        return t * lax.rsqrt(ms + EPS) * gain

    qn = head_norm(q, qg_ref[...])
    kn = head_norm(k, kg_ref[...])

    lf = _log_sigmoid(small_ref[0] + bf_ref[...])
    cum = _scan_add(lf) + cum_ref[0:1, :]
    cum_ref[...] = jnp.broadcast_to(cum[tm - 1:tm, :], cum_ref.shape)

    cs = cum * LOG2E
    c1 = cs.astype(BF16).astype(F32)
    r1 = cs - c1
    c2 = r1.astype(BF16).astype(F32)
    c3 = (r1 - c2).astype(BF16).astype(F32)
    lane = lax.broadcasted_iota(jnp.int32, cum.shape, 1)
    H = FOX_HEADS
    packed = jnp.where(lane < H, c1, 0.0)
    packed = jnp.where((lane >= H) & (lane < 2 * H), pltpu.roll(c2, H, axis=1), packed)
    packed = jnp.where((lane >= 2 * H) & (lane < 3 * H), pltpu.roll(c3, 2 * H, axis=1), packed)
    packed = jnp.where(lane == ONE_LANE, 1.0, packed).astype(BF16)

    kbias = _dot(packed, rk_ref[...])
    qbias = _dot(packed, rq_ref[...])
    feat = lane < HEAD_DIM

    def head_aug(t, bias, h):
        src = t[:, (h // 2) * LANES:(h // 2 + 1) * LANES]
        if h % 2:
            src = pltpu.roll(src, HEAD_DIM, axis=1)
        return jnp.where(feat, src, bias[:, h * AUG:(h + 1) * AUG])

    vt = v.T
    tk = vt_ref.shape[4]
    for h in range(H):
        kaug_ref[0, h] = head_aug(kn, kbias, h).astype(BF16)
        qaugt_ref[0, h] = head_aug(qn, qbias, h).T.astype(BF16)
        for cb in range(tm // tk):
            vt_ref[0, h, cb, 0:HEAD_DIM, :] = (
                vt[h * HEAD_DIM:(h + 1) * HEAD_DIM, cb * tk:(cb + 1) * tk].astype(BF16))
            vt_ref[0, h, cb, HEAD_DIM:V_ROWS, :] = jnp.ones((V_ROWS - HEAD_DIM, tk), BF16)


def _fox_attn_kernel(k_ref, qt_ref, vt_ref, o_ref, acc_ref, sa_ref, sb_ref, *, tk):
    qi = pl.program_id(1)
    units = [(h, c) for h in range(FOX_HEADS) for c in range(NQ_SUB)]
    acc_ref[...] = jnp.zeros_like(acc_ref)

    def qk(h, c, j):
        start = pl.multiple_of(j * tk, tk)
        return _dot(k_ref[0, h, pl.ds(start, tk), :], qt_ref[0, h, :, c * tk:(c + 1) * tk])

    def update(h, c, j, s, m_old, m_new):
        p = jnp.exp2(s - m_new).astype(BF16)
        acc_ref[h, c] = jnp.exp2(m_old - m_new) * acc_ref[h, c] + _dot(vt_ref[0, h, j], p)

    def step(j, cur_ref, c_cur, nxt_ref, ms):
        new_ms, c_nxt = [], []
        for i, (h, c) in enumerate(units):
            s_n = qk(h, c, j + 1)
            nxt_ref[h, c] = s_n
            c_nxt.append(jnp.max(s_n, axis=0, keepdims=True))
            m_new = jnp.maximum(ms[i], c_cur[i])
            new_ms.append(m_new)
            update(h, c, j, cur_ref[h, c], ms[i], m_new)
        return tuple(new_ms), tuple(c_nxt)

    def pair(t, carry):
        ms, ca = carry
        ms, cb = step(2 * t, sa_ref, ca, sb_ref, ms)
        ms, ca = step(2 * t + 1, sb_ref, cb, sa_ref, ms)
        return ms, ca

    c0 = []
    for h, c in units:
        s = qk(h, c, 0)
        sa_ref[h, c] = s
        c0.append(jnp.max(s, axis=0, keepdims=True))
    carry = (tuple(jnp.full((1, tk), NEG_BIG, F32) for _ in units), tuple(c0))
    assert NQ_SUB % 2 == 0
    n_pairs = (NQ_SUB // 2) * qi
    n_quads = lax.div(n_pairs, 2)
    carry = lax.fori_loop(0, n_quads, lambda u, cr: pair(2 * u + 1, pair(2 * u, cr)), carry)
    carry = lax.fori_loop(2 * n_quads, n_pairs, pair, carry)

    diag = (lax.broadcasted_iota(jnp.int32, (tk, tk), 0)
            <= lax.broadcasted_iota(jnp.int32, (tk, tk), 1))
    ms, cmax = list(carry[0]), list(carry[1])
    bufs = (sa_ref, sb_ref)
    for d in range(NQ_SUB):
        j = NQ_SUB * qi + d
        cur_ref, nxt_ref = bufs[d % 2], bufs[(d + 1) % 2]
        for i, (h, c) in enumerate(units):
            if c < d:
                continue
            if c > d:
                s_n = qk(h, c, j + 1)
                nxt_ref[h, c] = s_n
                c_next = jnp.max(s_n, axis=0, keepdims=True)
            if c == d:
                s = jnp.where(diag, cur_ref[h, c], NEG_BIG)
                m_new = jnp.maximum(ms[i], jnp.max(s, axis=0, keepdims=True))
            else:
                s = cur_ref[h, c]
                m_new = jnp.maximum(ms[i], cmax[i])
                cmax[i] = c_next
            update(h, c, j, s, ms[i], m_new)
            ms[i] = m_new

    for c in range(NQ_SUB):
        out = jnp.concatenate(
            [acc_ref[h, c, 0:HEAD_DIM, :] * (1.0 / acc_ref[h, c, HEAD_DIM:HEAD_DIM + 1, :])
             for h in range(FOX_HEADS)], axis=0)
        o_ref[0, c * tk:(c + 1) * tk, :] = out.T.astype(o_ref.dtype)


def _fox_attention(kaug, qaugt, vt, tk):
    B, H, S, _ = kaug.shape
    tq = NQ_SUB * tk
    assert S % tq == 0
    return pl.pallas_call(
        functools.partial(_fox_attn_kernel, tk=tk),
        grid=(B, S // tq),
        in_specs=[
            pl.BlockSpec((1, H, S, AUG), lambda b, i: (b, 0, 0, 0)),
            pl.BlockSpec((1, H, AUG, tq), lambda b, i: (b, 0, 0, i)),
            pl.BlockSpec((1, H, S // tk, V_ROWS, tk), lambda b, i: (b, 0, 0, 0, 0)),
        ],
        out_specs=pl.BlockSpec((1, tq, H * HEAD_DIM), lambda b, i: (b, i, 0)),
        out_shape=jax.ShapeDtypeStruct((B, S, H * HEAD_DIM), BF16),
        scratch_shapes=[pltpu.VMEM((H, NQ_SUB, V_ROWS, tk), F32), pltpu.VMEM((H, NQ_SUB, tk, tk), F32),
                        pltpu.VMEM((H, NQ_SUB, tk, tk), F32)],
        compiler_params=_cparams(("arbitrary", "arbitrary")),
        name="fox_attention",
    )(kaug, qaugt, vt)


GLA_ROWS = 256


def _gla_body(gla_ref, small_ref, wa_ref, ba_ref, gain_ref, tri_ref, hmask_ref, bdmask_ref,
              ones_ref, o_ref, st_ref):
    ts = o_ref.shape[1]
    KW, C, R = GLA_KW, GLA_CHUNK, GLA_ROWS
    q = gla_ref[0, :, 0:KW].astype(F32)
    k = gla_ref[0, :, KW:2 * KW].astype(F32)
    v = gla_ref[0, :, 2 * KW:3 * KW]
    low = small_ref[0].astype(BF16)
    la = _log_sigmoid(_dot(low, wa_ref[...]) + ba_ref[...]) * (LOG2E / GLA_TAU)
    la_hi = la.astype(BF16)
    la_lo = (la - la_hi.astype(F32)).astype(BF16)
    tri = tri_ref[...]
    bcum = jnp.concatenate(
        [_dot(tri, la_hi[g0:g0 + R]) + _dot(tri, la_lo[g0:g0 + R]) for g0 in range(0, ts, R)], axis=0)
    ends = [bcum[c0 + C - 1:c0 + C, :] for c0 in range(0, ts, C)]
    b_last = jnp.concatenate([jnp.broadcast_to(e, (C, KW)) for e in ends], axis=0)
    q_dec = (q * jnp.exp2(bcum)).astype(BF16)
    k_dec = (k * jnp.exp2(-bcum)).astype(BF16)
    k_end = (k * jnp.exp2(b_last - bcum)).astype(BF16)

    o_groups = []
    for g0 in range(0, ts, R):
        rows = slice(g0, g0 + R)
        o_acc = jnp.zeros((R, KW), F32)
        for h in range(GLA_HEADS):
            hm = hmask_ref[h:h + 1, :].astype(BF16)
            att = lax.dot_general(q_dec[rows] * hm, k_dec[rows], (((1,), (1,)), ((), ())),
                                  preferred_element_type=F32)
            o_acc = o_acc + _dot(att.astype(BF16) * tri, v[rows] * hm)
        o_groups.append(o_acc)

    o_parts = []
    for ci, c0 in enumerate(range(0, ts, C)):
        cs = slice(c0, c0 + C)
        st = st_ref[...]
        o_parts.append(lax.dot_general(q_dec[cs], st.astype(BF16), (((1,), (1,)), ((), ())),
                                       preferred_element_type=F32))
        kv_t = lax.dot_general(v[cs], k_end[cs], (((0,), (0,)), ((), ())),
                               preferred_element_type=F32)
        st_ref[...] = st * jnp.exp2(ends[ci]) + kv_t * bdmask_ref[...]
    o = jnp.concatenate(o_groups, axis=0) + jnp.concatenate(o_parts, axis=0)

    ms = _dot((o * o).astype(BF16), ones_ref[...])
    o = o * lax.rsqrt(ms + EPS) * gain_ref[...]
    g = gla_ref[0, :, 3 * KW:4 * KW].astype(F32)
    o_ref[0] = (o * (g * _sigmoid(g))).astype(o_ref.dtype)


def _fox_gla_kernel(fox_ref, gla_ref, small_ref, bf_ref, qg_ref, kg_ref, fones_ref, rq_ref, rk_ref,
                    wa_ref, ba_ref, gain_ref, tri_ref, hmask_ref, bdmask_ref, gones_ref,
                    kaug_ref, qaugt_ref, vt_ref, oc_ref, cum_ref, st_ref):
    @pl.when(pl.program_id(1) == 0)
    def _():
        cum_ref[...] = jnp.zeros_like(cum_ref)
        st_ref[...] = jnp.zeros_like(st_ref)

    _fox_prep_body(fox_ref, small_ref, bf_ref, qg_ref, kg_ref, fones_ref, rq_ref, rk_ref,
                   kaug_ref, qaugt_ref, vt_ref, cum_ref)
    _gla_body(gla_ref, small_ref, wa_ref, ba_ref, gain_ref, tri_ref, hmask_ref, bdmask_ref,
              gones_ref, oc_ref, st_ref)


def _fox_prep_gla(fox, gla, small, fox_p, gla_p, layer, tm, tk):
    B, S, _ = fox.shape
    H, KW = FOX_HEADS, GLA_KW
    tok = lambda n: pl.BlockSpec((1, tm, n), lambda b, i: (b, i, 0))
    specs = lambda stacked, consts: ([_layer_spec(p.shape, layer) for p in stacked]
                                     + [_const_spec(p.shape) for p in consts])
    fox_specs, gla_specs = specs(*fox_p), specs(*gla_p)
    fox_p, gla_p = fox_p[0] + fox_p[1], gla_p[0] + gla_p[1]
    return pl.pallas_call(
        _fox_gla_kernel,
        grid=(B, S // tm),
        in_specs=[tok(FOX_COLS), tok(GLA_COLS), tok(SMALL_COLS)] + fox_specs + gla_specs,
        out_specs=[
            pl.BlockSpec((1, H, tm, AUG), lambda b, i: (b, 0, i, 0)),
            pl.BlockSpec((1, H, AUG, tm), lambda b, i: (b, 0, 0, i)),
            pl.BlockSpec((1, H, tm // tk, V_ROWS, tk), lambda b, i: (b, 0, i, 0, 0)),
            tok(KW),
        ],
        out_shape=[
            jax.ShapeDtypeStruct((B, H, S, AUG), BF16),
            jax.ShapeDtypeStruct((B, H, AUG, S), BF16),
            jax.ShapeDtypeStruct((B, H, S // tk, V_ROWS, tk), BF16),
            jax.ShapeDtypeStruct((B, S, KW), BF16),
        ],
        scratch_shapes=[pltpu.VMEM((SUBLANES, LANES), F32), pltpu.VMEM((KW, KW), F32)],
        compiler_params=_cparams(("arbitrary", "arbitrary")),
        name="fox_prep_gla",
    )(fox, gla, small, *fox_p, *gla_p)


def _ffn_chunks(d_ff):
    chunks, c0 = [], 0
    while c0 < d_ff:
        n = min(FFN_CHUNK, d_ff - c0)
        chunks.append((c0, n))
        c0 += n
    return chunks


def _out_ffn_kernel(x_ref, a_ref, b_ref, c_ref, mod_ref, g_ref, wo_ref, wgu_ref, wd_ref, o_ref,
                    acc_ref):
    d_ff = wd_ref.shape[0]
    mix = _dot(jnp.concatenate([a_ref[0], b_ref[0], c_ref[0]], axis=1), wo_ref[...])
    x1 = x_ref[0] + mod_ref[0, 2:3, :] * mix
    h = _rms_mod(x1, g_ref[...], mod_ref[0, 4:5, :], mod_ref[0, 3:4, :]).astype(BF16)
    for idx, (c0, n) in enumerate(_ffn_chunks(d_ff)):
        gt = _dot(h, wgu_ref[:, c0:c0 + n])
        up = _dot(h, wgu_ref[:, d_ff + c0:d_ff + c0 + n])
        act = (gt * _sigmoid(gt) * up).astype(BF16)
        y = _dot(act, wd_ref[c0:c0 + n, :])
        if idx == 0:
            acc_ref[...] = y
        else:
            acc_ref[...] += y
    o_ref[0] = x1 + mod_ref[0, 5:6, :] * acc_ref[...]


def _out_ffn(x, a, b, c, mod, gain, wo, wgu, wd, layer, tm):
    B, S, D = x.shape
    tok = lambda n: pl.BlockSpec((1, tm, n), lambda bb, i: (bb, i, 0))
    return pl.pallas_call(
        _out_ffn_kernel,
        grid=(B, S // tm),
        in_specs=[
            tok(D), tok(a.shape[2]), tok(b.shape[2]), tok(c.shape[2]),
            _mod_spec(mod.shape, layer), _layer_spec(gain.shape, layer),
            _layer_weight_spec(wo.shape, layer), _layer_weight_spec(wgu.shape, layer),
            _layer_weight_spec(wd.shape, layer),
        ],
        out_specs=tok(D),
        out_shape=jax.ShapeDtypeStruct((B, S, D), F32),
        scratch_shapes=[pltpu.VMEM((tm, D), F32)],
        compiler_params=_cparams(("arbitrary", "arbitrary")),
        name="out_proj_ffn",
    )(x, a, b, c, mod, gain, wo, wgu, wd)


def _block_diag(w):
    L, n, d, e = w.shape
    eye = jnp.eye(n, dtype=w.dtype)
    return jnp.einsum("lnde,nm->lndme", w, eye).reshape(L, n * d, n * e)


_W_IN_SIZES = (LRU_WIDTH, LRU_WIDTH, FOX_WIDTH, FOX_WIDTH, FOX_WIDTH, FOX_HEADS,
               GLA_KW, GLA_KW, GLA_VW, GLA_RANK, GLA_VW)
_W_IN_OFFS = [int(o) for o in np.concatenate([[0], np.cumsum(_W_IN_SIZES)])]


def _regroup_kernel(w_ref, o_ref):
    x = w_ref[0]
    seg = lambda i: x[:, _W_IN_OFFS[i]:_W_IN_OFFS[i + 1]]
    q_scale = GLA_DK ** -0.5
    assert np.log2(q_scale) == np.round(np.log2(q_scale))
    c0 = 0
    for part in (seg(0), seg(1), seg(2), seg(3), seg(4), seg(6) * q_scale, seg(7), seg(8), seg(10)):
        n = part.shape[1]
        o_ref[0, :, c0:c0 + n] = part.astype(o_ref.dtype)
        c0 += n
    zeros = lambda n: jnp.zeros((x.shape[0], n), x.dtype)
    small = jnp.concatenate([seg(5), zeros(GLOW_LANE0 - FOX_HEADS), seg(9),
                             zeros(SMALL_COLS - GLOW_LANE0 - GLA_RANK)], axis=1)
    o_ref[0, :, c0:c0 + SMALL_COLS] = small.astype(o_ref.dtype)


def _regroup_w_in(w_in):
    L, D, N = w_in.shape
    tr = _pick_tile(D, 256)
    return pl.pallas_call(
        _regroup_kernel,
        grid=(L, D // tr),
        in_specs=[pl.BlockSpec((1, tr, N), lambda l, i: (l, i, 0))],
        out_specs=pl.BlockSpec((1, tr, NP_COLS), lambda l, i: (l, i, 0)),
        out_shape=jax.ShapeDtypeStruct((L, D, NP_COLS), BF16),
        compiler_params=_cparams(("arbitrary", "arbitrary")),
        name="regroup_w_in",
    )(w_in)


def _head_ones(n_heads, dim):
    assert dim & (dim - 1) == 0
    return jnp.asarray(np.kron(np.eye(n_heads), np.full((dim, dim), 1.0 / dim)), BF16)


def _fox_constants():
    H, W = FOX_HEADS, FOX_WIDTH
    rq = np.zeros((LANES, H * AUG), np.float32)
    rk = np.zeros((LANES, H * AUG), np.float32)
    for h in range(H):
        for s in range(N_SPLIT):
            rq[s * H + h, h * AUG + HEAD_DIM + s] = 1.0
            rq[ONE_LANE, h * AUG + HEAD_DIM + N_SPLIT + s] = 1.0
            rk[ONE_LANE, h * AUG + HEAD_DIM + s] = 1.0
            rk[s * H + h, h * AUG + HEAD_DIM + N_SPLIT + s] = -1.0
    return jnp.asarray(rq, BF16), jnp.asarray(rk, BF16)


def _gla_constants():
    R, C, KW = GLA_ROWS, GLA_CHUNK, GLA_KW
    i = np.arange(R)
    tri = ((i[:, None] >= i[None, :]) & (i[:, None] // C == i[None, :] // C)).astype(np.float32)
    lane = np.arange(KW)
    hmask = (lane[None, :] // GLA_DK == np.arange(GLA_HEADS)[:, None]).astype(np.float32)
    bdmask = (lane[:, None] // GLA_DK == lane[None, :] // GLA_DK).astype(np.float32)
    return jnp.asarray(tri, BF16), jnp.asarray(hmask, F32), jnp.asarray(bdmask, F32)


def _pick_tile(n, want):
    t = min(n, want)
    while n % t:
        t //= 2
    return t


def kernel(x, c, norm1_gain, norm2_gain, w_mod, b_mod, w_in, conv_w, conv_b, lru_w_r, lru_b_r,
           lru_w_i, lru_b_i, lru_lambda, fox_b_f, fox_q_gain, fox_k_gain, gla_w_alpha, gla_b_alpha,
           gla_out_gain, w_out, ffn_w_gate_up, ffn_w_down):
    B, S, D = x.shape
    L = w_in.shape[0]
    tm = _pick_tile(S, 1024)
    tp = _pick_tile(S, 1024)
    tq = _pick_tile(S, 256)

    mod = _modulation(c, w_mod, b_mod)
    w_in_p = _regroup_w_in(w_in)
    wr_bd = _block_diag(lru_w_r).astype(BF16)
    wi_bd = _block_diag(lru_w_i).astype(BF16)
    w_out_b = w_out.astype(BF16)
    wgu_b = ffn_w_gate_up.astype(BF16)
    wd_b = ffn_w_down.astype(BF16)
    row = lambda a: a.reshape(L, 1, a.shape[-1])
    bf_pad = jnp.pad(fox_b_f, ((0, 0), (0, LANES - FOX_HEADS))).reshape(L, 1, LANES)
    qg = jnp.tile(fox_q_gain * (HEAD_DIM ** -0.5 * LOG2E), (1, FOX_HEADS)).reshape(L, 1, FOX_WIDTH)
    kg = jnp.tile(fox_k_gain, (1, FOX_HEADS)).reshape(L, 1, FOX_WIDTH)
    wa_pad = jnp.pad(gla_w_alpha, ((0, 0), (GLOW_LANE0, LANES - GLOW_LANE0 - GLA_RANK), (0, 0))).astype(BF16)
    og = jnp.tile(gla_out_gain, (1, GLA_HEADS)).reshape(L, 1, GLA_VW)
    fox_ones = _head_ones(FOX_HEADS, HEAD_DIM)
    gla_ones = _head_ones(GLA_HEADS, GLA_DK)
    rq, rk = _fox_constants()
    tri, hmask, bdmask = _gla_constants()

    g1, g2 = row(norm1_gain), row(norm2_gain)
    lru_p = (conv_w, row(conv_b), wr_bd, row(lru_b_r), wi_bd, row(lru_b_i), row(lru_lambda))
    fox_p = ((bf_pad, qg, kg), (fox_ones, rq, rk))
    gla_p = ((wa_pad, row(gla_b_alpha), og), (tri, hmask, bdmask, gla_ones))
    for l in range(L):
        out_a, fox, gla, small = _in_proj(x, mod, g1, w_in_p, lru_p, l, tp)
        kaug, qaugt, vt, out_c = _fox_prep_gla(fox, gla, small, fox_p, gla_p, l, tp, tq)
        out_b = _fox_attention(kaug, qaugt, vt, tq)
        x = _out_ffn(x, out_a, out_b, out_c, mod, g2, w_out_b, wgu_b, wd_b, l, tm)
    return x
```

```python
import functools

import numpy as np
import jax
import jax.numpy as jnp
from jax import lax
from jax.experimental import pallas as pl
from jax.experimental.pallas import tpu as pltpu

F32 = jnp.float32
BF16 = jnp.bfloat16

EPS = 1e-6
HEAD_DIM = 64
LRU_WIDTH = 384
CONV_WIDTH = 4
LRU_C = 8.0
FOX_HEADS = 6
FOX_WIDTH = FOX_HEADS * HEAD_DIM
GLA_HEADS = 4
GLA_DK = 64
GLA_KW = GLA_HEADS * GLA_DK
GLA_VW = GLA_KW
GLA_RANK = 16
GLA_TAU = 16.0
GLA_CHUNK = 64
N_MOD = 6

LANES = 128
SUBLANES = 8
BF16_ROWS = 16
VMEM_LIMIT_BYTES = 56 * 1024 * 1024
FFN_CHUNK = 512

LRU_COLS = 2 * LRU_WIDTH
FOX_COLS = 3 * FOX_WIDTH
GLA_COLS = 3 * GLA_KW + GLA_VW
SMALL_COLS = LANES
GLOW_LANE0 = 8
NP_COLS = LRU_COLS + FOX_COLS + GLA_COLS + SMALL_COLS
MXU_COLS = 256
LRU_ROWS = 128

AUG = LANES
N_SPLIT = 3
ONE_LANE = N_SPLIT * FOX_HEADS
NEG_BIG = -1e30
V_ROWS = HEAD_DIM + BF16_ROWS
NQ_SUB = 2
LOG2E = float(np.log2(np.e))


def _sigmoid(x):
    return 0.5 * jnp.tanh(0.5 * x) + 0.5


def _log_sigmoid(x):
    return jnp.minimum(x, 0.0) - jnp.log(1.0 + jnp.exp(-jnp.abs(x)))


def _softplus(x):
    return jnp.maximum(x, 0.0) + jnp.log(1.0 + jnp.exp(-jnp.abs(x)))


def _gelu_tanh(x):
    c = np.float32(np.sqrt(2.0 / np.pi))
    return 0.5 * x * (1.0 + jnp.tanh(c * (x + 0.044715 * (x * x * x))))


def _dot(a, b):
    return jnp.dot(a, b, preferred_element_type=F32)


def _cparams(sem):
    return pltpu.CompilerParams(dimension_semantics=sem, vmem_limit_bytes=VMEM_LIMIT_BYTES)


def _const_spec(shape):
    nd = len(shape)
    return pl.BlockSpec(shape, lambda *_: (0,) * nd)


def _layer_spec(stacked_shape, layer):
    tail = tuple(stacked_shape[1:])
    return pl.BlockSpec((None,) + tail, lambda *_: (layer,) + (0,) * len(tail))


def _layer_weight_spec(stacked_shape, layer):
    tail = tuple(stacked_shape[1:])
    return pl.BlockSpec((None,) + tail, lambda *_: (layer,) + (0,) * len(tail),
                        pipeline_mode=pl.Buffered(1))


def _mod_spec(mod_shape, layer):
    return pl.BlockSpec((None, 1) + tuple(mod_shape[2:]), lambda b, i: (layer, b, 0, 0))


def _mod_kernel(c_ref, w_ref, b_ref, o_ref):
    c = c_ref[...]
    ca = (c * _sigmoid(c)).astype(BF16)
    o_ref[0] = _dot(ca, w_ref[0].astype(BF16)) + b_ref[0]


def _modulation(c, w_mod, b_mod):
    L, D, N = w_mod.shape
    B = c.shape[0]
    bp = -(-B // SUBLANES) * SUBLANES
    cp = jnp.pad(c, ((0, bp - B), (0, 0)))
    tn = 1024
    out = pl.pallas_call(
        _mod_kernel,
        grid=(L, N // tn),
        in_specs=[
            pl.BlockSpec((bp, D), lambda l, n: (0, 0)),
            pl.BlockSpec((1, D, tn), lambda l, n: (l, 0, n)),
            pl.BlockSpec((1, 1, tn), lambda l, n: (l, 0, n)),
        ],
        out_specs=pl.BlockSpec((1, bp, tn), lambda l, n: (l, 0, n)),
        out_shape=jax.ShapeDtypeStruct((L, bp, N), F32),
        compiler_params=_cparams(("arbitrary", "arbitrary")),
        name="adaln_mod",
    )(cp, w_mod, b_mod.reshape(L, 1, N))
    return out.reshape(L, bp, N_MOD, D)


def _rms_mod(x, gain, scale, shift):
    ms = jnp.mean(x * x, axis=-1, keepdims=True)
    y = x * lax.rsqrt(ms + EPS) * gain
    return y * (1.0 + scale) + shift


def _in_proj_kernel(x_ref, mod_ref, g_ref, w_ref, cw_ref, cb_ref, wr_ref, br_ref, wi_ref, bi_ref,
                    lam_ref, oa_ref, fox_ref, gla_ref, small_ref, xs_ref, h_ref):
    tm, W = x_ref.shape[1], LRU_WIDTH

    @pl.when(pl.program_id(1) == 0)
    def _():
        xs_ref[0:SUBLANES, :] = jnp.zeros((SUBLANES, W), F32)
        h_ref[...] = jnp.zeros_like(h_ref)

    h = _rms_mod(x_ref[0], g_ref[...], mod_ref[0, 1:2, :], mod_ref[0, 0:1, :]).astype(BF16)
    segs, c0 = [], LRU_COLS
    for ref, n in ((fox_ref, FOX_COLS), (gla_ref, GLA_COLS), (small_ref, SMALL_COLS)):
        segs.append((ref, c0, c0 + n))
        c0 += n

    def project(g0):
        g1 = g0 + MXU_COLS
        y = _dot(h, w_ref[:, g0:g1])
        for ref, s0, s1 in segs:
            lo, hi = max(g0, s0), min(g1, s1)
            if lo < hi:
                ref[0, :, lo - s0:hi - s0] = y[:, lo - g0:hi - g0].astype(ref.dtype)

    y0 = _dot(h, w_ref[:, 0:LRU_COLS])
    xs_ref[SUBLANES:SUBLANES + tm, :] = y0[:, 0:W]
    pieces = [(s, r0) for s in range(W // LANES) for r0 in range(0, tm, LRU_ROWS)]
    gates = [_lru_gates(y0[r0:r0 + LRU_ROWS, s * LANES:(s + 1) * LANES], r0, slice(s * LANES, (s + 1) * LANES),
                        cw_ref, cb_ref, wr_ref, wi_ref, xs_ref) for s, r0 in pieces]
    xs_ref[0:SUBLANES, :] = xs_ref[tm:tm + SUBLANES, :]
    chunks = list(range(LRU_COLS, NP_COLS, MXU_COLS))
    carry = None
    for (s, r0), g in zip(pieces, gates):
        sl = slice(s * LANES, (s + 1) * LANES)
        rows = slice(r0, r0 + LRU_ROWS)
        if chunks:
            project(chunks.pop(0))
        if r0 == 0:
            carry = h_ref[0:1, sl]
        out, carry = _lru_finish(*g, y0[rows, W + s * LANES:W + (s + 1) * LANES], sl, carry,
                                 br_ref, bi_ref, lam_ref)
        oa_ref[0, rows, sl] = out.astype(oa_ref.dtype)
        if r0 + LRU_ROWS == tm:
            h_ref[:, sl] = jnp.broadcast_to(carry, (h_ref.shape[0], LANES))
    for g0 in chunks:
        project(g0)


def _in_proj(x, mod, gain, w, lru_p, layer, tm):
    B, S, D = x.shape
    W = LRU_WIDTH
    tok = lambda n: pl.BlockSpec((1, tm, n), lambda b, i: (b, i, 0))
    return pl.pallas_call(
        _in_proj_kernel,
        grid=(B, S // tm),
        in_specs=[tok(D), _mod_spec(mod.shape, layer), _layer_spec(gain.shape, layer),
                  _layer_weight_spec(w.shape, layer)] + [_layer_spec(p.shape, layer) for p in lru_p],
        out_specs=[tok(W), tok(FOX_COLS), tok(GLA_COLS), tok(SMALL_COLS)],
        out_shape=[
            jax.ShapeDtypeStruct((B, S, W), BF16),
            jax.ShapeDtypeStruct((B, S, FOX_COLS), BF16),
            jax.ShapeDtypeStruct((B, S, GLA_COLS), BF16),
            jax.ShapeDtypeStruct((B, S, SMALL_COLS), F32),
        ],
        scratch_shapes=[pltpu.VMEM((tm + SUBLANES, W), F32), pltpu.VMEM((SUBLANES, W), F32)],
        compiler_params=_cparams(("arbitrary", "arbitrary")),
        name="in_proj_lru",
    )(x, mod, gain, w, *lru_p)


def _scan_add(v):
    n = v.shape[0]
    row = lax.broadcasted_iota(jnp.int32, v.shape, 0)
    d = 1
    while d < n:
        v = v + jnp.where(row >= d, pltpu.roll(v, d, axis=0), 0.0)
        d *= 2
    return v


def _scan_linear(a, u, h0):
    n, w = a.shape
    g = n // SUBLANES
    a = a.reshape(g, SUBLANES, w)
    u = u.reshape(g, SUBLANES, w)
    sub = lax.broadcasted_iota(jnp.int32, a.shape, 1)
    d = 1
    while d < SUBLANES:
        m = sub >= d
        u = u + jnp.where(m, a * pltpu.roll(u, d, axis=1), 0.0)
        a = jnp.where(m, a * pltpu.roll(a, d, axis=1), a)
        d *= 2
    hs, h = [], h0
    for v in range(g):
        hv = a[v] * h + u[v]
        h = hv[SUBLANES - 1:SUBLANES, :]
        hs.append(hv)
    return jnp.concatenate(hs, axis=0), h


def _lru_gates(xa, r0, sl, cw_ref, cb_ref, wr_ref, wi_ref, xs_ref):
    n = xa.shape[0]
    xc = xa * cw_ref[CONV_WIDTH - 1:CONV_WIDTH, sl] + cb_ref[:, sl]
    for k in range(1, CONV_WIDTH):
        j = CONV_WIDTH - 1 - k
        xc = xc + xs_ref[pl.ds(SUBLANES - k + r0, n), sl] * cw_ref[j:j + 1, sl]
    xcb = xc.astype(BF16)
    return xc, _dot(xcb, wr_ref[sl, sl]), _dot(xcb, wi_ref[sl, sl])


def _lru_finish(xc, r_pre, i_pre, ya, sl, h0, br_ref, bi_ref, lam_ref):
    r = _sigmoid(r_pre + br_ref[:, sl])
    i = _sigmoid(i_pre + bi_ref[:, sl])
    a = jnp.exp2(r * ((-LRU_C * LOG2E) * _softplus(-lam_ref[:, sl])))
    mult = jnp.sqrt(1.0 - a * a)
    u = mult * (i * xc)
    h, h_last = _scan_linear(a, u, h0)
    return h * _gelu_tanh(ya), h_last


def _fox_prep_body(fox_ref, small_ref, bf_ref, qg_ref, kg_ref, ones_ref,
                   rq_ref, rk_ref, kaug_ref, qaugt_ref, vt_ref, cum_ref):
    tm = fox_ref.shape[1]
    W = FOX_WIDTH
    q = fox_ref[0, :, 0:W].astype(F32)
    k = fox_ref[0, :, W:2 * W].astype(F32)
    v = fox_ref[0, :, 2 * W:3 * W].astype(F32)

    def head_norm(t, gain):
        ms = _dot((t * t).astype(BF16), ones_ref[...])
        return t * lax.rsqrt(ms + EPS) * gain

    qn = head_norm(q, qg_ref[...])
    kn = head_norm(k, kg_ref[...])

    lf = _log_sigmoid(small_ref[0] + bf_ref[...])
    cum = _scan_add(lf) + cum_ref[0:1, :]
    cum_ref[...] = jnp.broadcast_to(cum[tm - 1:tm, :], cum_ref.shape)

    cs = cum * LOG2E
    c1 = cs.astype(BF16).astype(F32)
    r1 = cs - c1
    c2 = r1.astype(BF16).astype(F32)
    c3 = (r1 - c2).astype(BF16).astype(F32)
    lane = lax.broadcasted_iota(jnp.int32, cum.shape, 1)
    H = FOX_HEADS
    packed = jnp.where(lane < H, c1, 0.0)
    packed = jnp.where((lane >= H) & (lane < 2 * H), pltpu.roll(c2, H, axis=1), packed)
    packed = jnp.where((lane >= 2 * H) & (lane < 3 * H), pltpu.roll(c3, 2 * H, axis=1), packed)
    packed = jnp.where(lane == ONE_LANE, 1.0, packed).astype(BF16)

    kbias = _dot(packed, rk_ref[...])
    qbias = _dot(packed, rq_ref[...])
    feat = lane < HEAD_DIM

    def head_aug(t, bias, h):
        src = t[:, (h // 2) * LANES:(h // 2 + 1) * LANES]
        if h % 2:
            src = pltpu.roll(src, HEAD_DIM, axis=1)
        return jnp.where(feat, src, bias[:, h * AUG:(h + 1) * AUG])

    vt = v.T
    tk = vt_ref.shape[4]
    for h in range(H):
        kaug_ref[0, h] = head_aug(kn, kbias, h).astype(BF16)
        qaugt_ref[0, h] = head_aug(qn, qbias, h).T.astype(BF16)
        for cb in range(tm // tk):
            vt_ref[0, h, cb, 0:HEAD_DIM, :] = (
                vt[h * HEAD_DIM:(h + 1) * HEAD_DIM, cb * tk:(cb + 1) * tk].astype(BF16))
            vt_ref[0, h, cb, HEAD_DIM:V_ROWS, :] = jnp.ones((V_ROWS - HEAD_DIM, tk), BF16)


def _fox_attn_kernel(k_ref, qt_ref, vt_ref, o_ref, acc_ref, sa_ref, sb_ref, *, tk):
    qi = pl.program_id(1)
    units = [(h, c) for h in range(FOX_HEADS) for c in range(NQ_SUB)]
    acc_ref[...] = jnp.zeros_like(acc_ref)

    def qk(h, c, j):
        start = pl.multiple_of(j * tk, tk)
        return _dot(k_ref[0, h, pl.ds(start, tk), :], qt_ref[0, h, :, c * tk:(c + 1) * tk])

    def update(h, c, j, s, m_old, m_new):
        p = jnp.exp2(s - m_new).astype(BF16)
        acc_ref[h, c] = jnp.exp2(m_old - m_new) * acc_ref[h, c] + _dot(vt_ref[0, h, j], p)

    def step(j, cur_ref, c_cur, nxt_ref, ms):
        new_ms, c_nxt = [], []
        for i, (h, c) in enumerate(units):
            s_n = qk(h, c, j + 1)
            nxt_ref[h, c] = s_n
            c_nxt.append(jnp.max(s_n, axis=0, keepdims=True))
            m_new = jnp.maximum(ms[i], c_cur[i])
            new_ms.append(m_new)
            update(h, c, j, cur_ref[h, c], ms[i], m_new)
        return tuple(new_ms), tuple(c_nxt)

    def pair(t, carry):
        ms, ca = carry
        ms, cb = step(2 * t, sa_ref, ca, sb_ref, ms)
        ms, ca = step(2 * t + 1, sb_ref, cb, sa_ref, ms)
        return ms, ca

    c0 = []
    for h, c in units:
        s = qk(h, c, 0)
        sa_ref[h, c] = s
        c0.append(jnp.max(s, axis=0, keepdims=True))
    carry = (tuple(jnp.full((1, tk), NEG_BIG, F32) for _ in units), tuple(c0))
    assert NQ_SUB % 2 == 0
    n_pairs = (NQ_SUB // 2) * qi
    n_quads = lax.div(n_pairs, 2)
    carry = lax.fori_loop(0, n_quads, lambda u, cr: pair(2 * u + 1, pair(2 * u, cr)), carry)
    carry = lax.fori_loop(2 * n_quads, n_pairs, pair, carry)

    diag = (lax.broadcasted_iota(jnp.int32, (tk, tk), 0)
            <= lax.broadcasted_iota(jnp.int32, (tk, tk), 1))
    ms, cmax = list(carry[0]), list(carry[1])
    bufs = (sa_ref, sb_ref)
    for d in range(NQ_SUB):
        j = NQ_SUB * qi + d
        cur_ref, nxt_ref = bufs[d % 2], bufs[(d + 1) % 2]
        for i, (h, c) in enumerate(units):
            if c < d:
                continue
            if c > d:
                s_n = qk(h, c, j + 1)
                nxt_ref[h, c] = s_n
                c_next = jnp.max(s_n, axis=0, keepdims=True)
            if c == d:
                s = jnp.where(diag, cur_ref[h, c], NEG_BIG)
                m_new = jnp.maximum(ms[i], jnp.max(s, axis=0, keepdims=True))
            else:
                s = cur_ref[h, c]
                m_new = jnp.maximum(ms[i], cmax[i])
                cmax[i] = c_next
            update(h, c, j, s, ms[i], m_new)
            ms[i] = m_new

    for c in range(NQ_SUB):
        out = jnp.concatenate(
            [acc_ref[h, c, 0:HEAD_DIM, :] * (1.0 / acc_ref[h, c, HEAD_DIM:HEAD_DIM + 1, :])
             for h in range(FOX_HEADS)], axis=0)
        o_ref[0, c * tk:(c + 1) * tk, :] = out.T.astype(o_ref.dtype)


def _fox_attention(kaug, qaugt, vt, tk):
    B, H, S, _ = kaug.shape
    tq = NQ_SUB * tk
    assert S % tq == 0
    return pl.pallas_call(
        functools.partial(_fox_attn_kernel, tk=tk),
        grid=(B, S // tq),
        in_specs=[
            pl.BlockSpec((1, H, S, AUG), lambda b, i: (b, 0, 0, 0)),
            pl.BlockSpec((1, H, AUG, tq), lambda b, i: (b, 0, 0, i)),
            pl.BlockSpec((1, H, S // tk, V_ROWS, tk), lambda b, i: (b, 0, 0, 0, 0)),
        ],
        out_specs=pl.BlockSpec((1, tq, H * HEAD_DIM), lambda b, i: (b, i, 0)),
        out_shape=jax.ShapeDtypeStruct((B, S, H * HEAD_DIM), BF16),
        scratch_shapes=[pltpu.VMEM((H, NQ_SUB, V_ROWS, tk), F32), pltpu.VMEM((H, NQ_SUB, tk, tk), F32),
                        pltpu.VMEM((H, NQ_SUB, tk, tk), F32)],
        compiler_params=_cparams(("arbitrary", "arbitrary")),
        name="fox_attention",
    )(kaug, qaugt, vt)


GLA_ROWS = 256


def _gla_body(gla_ref, small_ref, wa_ref, ba_ref, gain_ref, tri_ref, hmask_ref, bdmask_ref,
              ones_ref, o_ref, st_ref):
    ts = o_ref.shape[1]
    KW, C, R = GLA_KW, GLA_CHUNK, GLA_ROWS
    q = gla_ref[0, :, 0:KW].astype(F32)
    k = gla_ref[0, :, KW:2 * KW].astype(F32)
    v = gla_ref[0, :, 2 * KW:3 * KW]
    low = small_ref[0].astype(BF16)
    la = _log_sigmoid(_dot(low, wa_ref[...]) + ba_ref[...]) * (LOG2E / GLA_TAU)
    la_hi = la.astype(BF16)
    la_lo = (la - la_hi.astype(F32)).astype(BF16)
    tri, pair = tri_ref[0], tri_ref[1]
    bcum = jnp.concatenate(
        [_dot(tri, la_hi[g0:g0 + R]) + _dot(tri, la_lo[g0:g0 + R]) for g0 in range(0, ts, R)], axis=0)
    ends = [bcum[c0 + C - 1:c0 + C, :] for c0 in range(0, ts, C)]
    b_last = jnp.concatenate([jnp.broadcast_to(e, (C, KW)) for e in ends], axis=0)
    q_dec = (q * jnp.exp2(bcum)).astype(BF16)
    k_dec = (k * jnp.exp2(-bcum)).astype(BF16)
    k_end = (k * jnp.exp2(b_last - bcum)).astype(BF16)

    o_groups = []
    for g0 in range(0, ts, R):
        rows = slice(g0, g0 + R)
        o_acc = jnp.zeros((R, KW), F32)
        for h in range(GLA_HEADS):
            hm = hmask_ref[h:h + 1, :].astype(BF16)
            qh = q_dec[rows] * hm
            nt = (((1,), (1,)), ((), ()))
            att = lax.dot_general(qh, k_dec[rows], nt, preferred_element_type=F32)
            att2 = lax.dot_general(qh, k_end[rows], nt, preferred_element_type=F32)
            o_acc = o_acc + _dot(att.astype(BF16) * tri + att2.astype(BF16) * pair, v[rows] * hm)
        o_groups.append(o_acc)

    o_parts = []
    for pi, p0 in enumerate(range(0, ts, 2 * C)):
        ra, rb, rp = slice(p0, p0 + C), slice(p0 + C, p0 + 2 * C), slice(p0, p0 + 2 * C)
        d_a, d_b = jnp.exp2(ends[2 * pi]), jnp.exp2(ends[2 * pi + 1])
        st = st_ref[...]
        q_pair = jnp.concatenate([q_dec[ra], (q_dec[rb].astype(F32) * d_a).astype(BF16)], axis=0)
        k_pair = jnp.concatenate([(k_end[ra].astype(F32) * d_b).astype(BF16), k_end[rb]], axis=0)
        o_parts.append(lax.dot_general(q_pair, st.astype(BF16), (((1,), (1,)), ((), ())),
                                       preferred_element_type=F32))
        kv_t = lax.dot_general(v[rp], k_pair, (((0,), (0,)), ((), ())),
                               preferred_element_type=F32)
        st_ref[...] = st * (d_a * d_b) + kv_t * bdmask_ref[...]
    o = jnp.concatenate(o_groups, axis=0) + jnp.concatenate(o_parts, axis=0)

    ms = _dot((o * o).astype(BF16), ones_ref[...])
    o = o * lax.rsqrt(ms + EPS) * gain_ref[...]
    g = gla_ref[0, :, 3 * KW:4 * KW].astype(F32)
    o_ref[0] = (o * (g * _sigmoid(g))).astype(o_ref.dtype)


def _fox_gla_kernel(fox_ref, gla_ref, small_ref, bf_ref, qg_ref, kg_ref, fones_ref, rq_ref, rk_ref,
                    wa_ref, ba_ref, gain_ref, tri_ref, hmask_ref, bdmask_ref, gones_ref,
                    kaug_ref, qaugt_ref, vt_ref, oc_ref, cum_ref, st_ref):
    @pl.when(pl.program_id(1) == 0)
    def _():
        cum_ref[...] = jnp.zeros_like(cum_ref)
        st_ref[...] = jnp.zeros_like(st_ref)

    _fox_prep_body(fox_ref, small_ref, bf_ref, qg_ref, kg_ref, fones_ref, rq_ref, rk_ref,
                   kaug_ref, qaugt_ref, vt_ref, cum_ref)
    _gla_body(gla_ref, small_ref, wa_ref, ba_ref, gain_ref, tri_ref, hmask_ref, bdmask_ref,
              gones_ref, oc_ref, st_ref)


def _fox_prep_gla(fox, gla, small, fox_p, gla_p, layer, tm, tk):
    B, S, _ = fox.shape
    H, KW = FOX_HEADS, GLA_KW
    tok = lambda n: pl.BlockSpec((1, tm, n), lambda b, i: (b, i, 0))
    specs = lambda stacked, consts: ([_layer_spec(p.shape, layer) for p in stacked]
                                     + [_const_spec(p.shape) for p in consts])
    fox_specs, gla_specs = specs(*fox_p), specs(*gla_p)
    fox_p, gla_p = fox_p[0] + fox_p[1], gla_p[0] + gla_p[1]
    return pl.pallas_call(
        _fox_gla_kernel,
        grid=(B, S // tm),
        in_specs=[tok(FOX_COLS), tok(GLA_COLS), tok(SMALL_COLS)] + fox_specs + gla_specs,
        out_specs=[
            pl.BlockSpec((1, H, tm, AUG), lambda b, i: (b, 0, i, 0)),
            pl.BlockSpec((1, H, AUG, tm), lambda b, i: (b, 0, 0, i)),
            pl.BlockSpec((1, H, tm // tk, V_ROWS, tk), lambda b, i: (b, 0, i, 0, 0)),
            tok(KW),
        ],
        out_shape=[
            jax.ShapeDtypeStruct((B, H, S, AUG), BF16),
            jax.ShapeDtypeStruct((B, H, AUG, S), BF16),
            jax.ShapeDtypeStruct((B, H, S // tk, V_ROWS, tk), BF16),
            jax.ShapeDtypeStruct((B, S, KW), BF16),
        ],
        scratch_shapes=[pltpu.VMEM((SUBLANES, LANES), F32), pltpu.VMEM((KW, KW), F32)],
        compiler_params=_cparams(("arbitrary", "arbitrary")),
        name="fox_prep_gla",
    )(fox, gla, small, *fox_p, *gla_p)


def _ffn_chunks(d_ff):
    chunks, c0 = [], 0
    while c0 < d_ff:
        n = min(FFN_CHUNK, d_ff - c0)
        chunks.append((c0, n))
        c0 += n
    return chunks


def _out_ffn_kernel(x_ref, a_ref, b_ref, c_ref, mod_ref, g_ref, wo_ref, wgu_ref, wd_ref, o_ref,
                    acc_ref):
    d_ff = wd_ref.shape[0]
    mix = _dot(jnp.concatenate([a_ref[0], b_ref[0], c_ref[0]], axis=1), wo_ref[...])
    x1 = x_ref[0] + mod_ref[0, 2:3, :] * mix
    h = _rms_mod(x1, g_ref[...], mod_ref[0, 4:5, :], mod_ref[0, 3:4, :]).astype(BF16)
    for idx, (c0, n) in enumerate(_ffn_chunks(d_ff)):
        gt = _dot(h, wgu_ref[:, c0:c0 + n])
        up = _dot(h, wgu_ref[:, d_ff + c0:d_ff + c0 + n])
        act = (gt * _sigmoid(gt) * up).astype(BF16)
        y = _dot(act, wd_ref[c0:c0 + n, :])
        if idx == 0:
            acc_ref[...] = y
        else:
            acc_ref[...] += y
    o_ref[0] = x1 + mod_ref[0, 5:6, :] * acc_ref[...]


def _out_ffn(x, a, b, c, mod, gain, wo, wgu, wd, layer, tm):
    B, S, D = x.shape
    tok = lambda n: pl.BlockSpec((1, tm, n), lambda bb, i: (bb, i, 0))
    return pl.pallas_call(
        _out_ffn_kernel,
        grid=(B, S // tm),
        in_specs=[
            tok(D), tok(a.shape[2]), tok(b.shape[2]), tok(c.shape[2]),
            _mod_spec(mod.shape, layer), _layer_spec(gain.shape, layer),
            _layer_weight_spec(wo.shape, layer), _layer_weight_spec(wgu.shape, layer),
            _layer_weight_spec(wd.shape, layer),
        ],
        out_specs=tok(D),
        out_shape=jax.ShapeDtypeStruct((B, S, D), F32),
        scratch_shapes=[pltpu.VMEM((tm, D), F32)],
        compiler_params=_cparams(("arbitrary", "arbitrary")),
        name="out_proj_ffn",
    )(x, a, b, c, mod, gain, wo, wgu, wd)


def _block_diag(w):
    L, n, d, e = w.shape
    eye = jnp.eye(n, dtype=w.dtype)
    return jnp.einsum("lnde,nm->lndme", w, eye).reshape(L, n * d, n * e)


def _regroup_w_in(w_in):
    L, D, _ = w_in.shape
    sizes = (LRU_WIDTH, LRU_WIDTH, FOX_WIDTH, FOX_WIDTH, FOX_WIDTH, FOX_HEADS,
             GLA_KW, GLA_KW, GLA_VW, GLA_RANK, GLA_VW)
    offs = np.concatenate([[0], np.cumsum(sizes)])
    w = w_in.astype(BF16)
    seg = lambda i: w[:, :, offs[i]:offs[i + 1]]
    z = lambda n: jnp.zeros((L, D, n), BF16)
    q_scale = GLA_DK ** -0.5
    assert np.log2(q_scale) == np.round(np.log2(q_scale))
    cols = [seg(0), seg(1), seg(2), seg(3), seg(4), seg(6) * q_scale, seg(7), seg(8), seg(10),
            seg(5), z(GLOW_LANE0 - FOX_HEADS), seg(9), z(SMALL_COLS - GLOW_LANE0 - GLA_RANK)]
    return jnp.concatenate(cols, axis=-1)


def _head_ones(n_heads, dim):
    assert dim & (dim - 1) == 0
    return jnp.asarray(np.kron(np.eye(n_heads), np.full((dim, dim), 1.0 / dim)), BF16)


def _fox_constants():
    H, W = FOX_HEADS, FOX_WIDTH
    rq = np.zeros((LANES, H * AUG), np.float32)
    rk = np.zeros((LANES, H * AUG), np.float32)
    for h in range(H):
        for s in range(N_SPLIT):
            rq[s * H + h, h * AUG + HEAD_DIM + s] = 1.0
            rq[ONE_LANE, h * AUG + HEAD_DIM + N_SPLIT + s] = 1.0
            rk[ONE_LANE, h * AUG + HEAD_DIM + s] = 1.0
            rk[s * H + h, h * AUG + HEAD_DIM + N_SPLIT + s] = -1.0
    return jnp.asarray(rq, BF16), jnp.asarray(rk, BF16)


def _gla_constants():
    R, C, KW = GLA_ROWS, GLA_CHUNK, GLA_KW
    i = np.arange(R)
    ci = i // C
    same = (i[:, None] >= i[None, :]) & (ci[:, None] == ci[None, :])
    prev = (ci[:, None] % 2 == 1) & (ci[None, :] == ci[:, None] - 1)
    tri = np.stack([same, prev]).astype(np.float32)
    lane = np.arange(KW)
    hmask = (lane[None, :] // GLA_DK == np.arange(GLA_HEADS)[:, None]).astype(np.float32)
    bdmask = (lane[:, None] // GLA_DK == lane[None, :] // GLA_DK).astype(np.float32)
    return jnp.asarray(tri, BF16), jnp.asarray(hmask, F32), jnp.asarray(bdmask, F32)


def _pick_tile(n, want):
    t = min(n, want)
    while n % t:
        t //= 2
    return t


def kernel(x, c, norm1_gain, norm2_gain, w_mod, b_mod, w_in, conv_w, conv_b, lru_w_r, lru_b_r,
           lru_w_i, lru_b_i, lru_lambda, fox_b_f, fox_q_gain, fox_k_gain, gla_w_alpha, gla_b_alpha,
           gla_out_gain, w_out, ffn_w_gate_up, ffn_w_down):
    B, S, D = x.shape
    L = w_in.shape[0]
    tm = _pick_tile(S, 1024)
    tp = _pick_tile(S, 1024)
    tq = _pick_tile(S, 256)

    mod = _modulation(c, w_mod, b_mod)
    w_in_p = _regroup_w_in(w_in)
    wr_bd = _block_diag(lru_w_r).astype(BF16)
    wi_bd = _block_diag(lru_w_i).astype(BF16)
    w_out_b = w_out.astype(BF16)
    wgu_b = ffn_w_gate_up.astype(BF16)
    wd_b = ffn_w_down.astype(BF16)
    row = lambda a: a.reshape(L, 1, a.shape[-1])
    bf_pad = jnp.pad(fox_b_f, ((0, 0), (0, LANES - FOX_HEADS))).reshape(L, 1, LANES)
    qg = jnp.tile(fox_q_gain * (HEAD_DIM ** -0.5 * LOG2E), (1, FOX_HEADS)).reshape(L, 1, FOX_WIDTH)
    kg = jnp.tile(fox_k_gain, (1, FOX_HEADS)).reshape(L, 1, FOX_WIDTH)
    wa_pad = jnp.pad(gla_w_alpha, ((0, 0), (GLOW_LANE0, LANES - GLOW_LANE0 - GLA_RANK), (0, 0))).astype(BF16)
    og = jnp.tile(gla_out_gain, (1, GLA_HEADS)).reshape(L, 1, GLA_VW)
    fox_ones = _head_ones(FOX_HEADS, HEAD_DIM)
    gla_ones = _head_ones(GLA_HEADS, GLA_DK)
    rq, rk = _fox_constants()
    tri, hmask, bdmask = _gla_constants()

    g1, g2 = row(norm1_gain), row(norm2_gain)
    lru_p = (conv_w, row(conv_b), wr_bd, row(lru_b_r), wi_bd, row(lru_b_i), row(lru_lambda))
    fox_p = ((bf_pad, qg, kg), (fox_ones, rq, rk))
    gla_p = ((wa_pad, row(gla_b_alpha), og), (tri, hmask, bdmask, gla_ones))
    for l in range(L):
        out_a, fox, gla, small = _in_proj(x, mod, g1, w_in_p, lru_p, l, tp)
        kaug, qaugt, vt, out_c = _fox_prep_gla(fox, gla, small, fox_p, gla_p, l, tp, tq)
        out_b = _fox_attention(kaug, qaugt, vt, tq)
        x = _out_ffn(x, out_a, out_b, out_c, mod, g2, w_out_b, wgu_b, wd_b, l, tm)
    return x
```

```python
import functools

import numpy as np
import jax
import jax.numpy as jnp
from jax import lax
from jax.experimental import pallas as pl
from jax.experimental.pallas import tpu as pltpu

F32 = jnp.float32
BF16 = jnp.bfloat16

EPS = 1e-6
HEAD_DIM = 64
LRU_WIDTH = 384
CONV_WIDTH = 4
LRU_C = 8.0
FOX_HEADS = 6
FOX_WIDTH = FOX_HEADS * HEAD_DIM
GLA_HEADS = 4
GLA_DK = 64
GLA_KW = GLA_HEADS * GLA_DK
GLA_VW = GLA_KW
GLA_RANK = 16
GLA_TAU = 16.0
GLA_CHUNK = 64
N_MOD = 6

LANES = 128
SUBLANES = 8
BF16_ROWS = 16
VMEM_LIMIT_BYTES = 56 * 1024 * 1024
FFN_CHUNK = 256

LRU_COLS = 2 * LRU_WIDTH
FOX_COLS = 3 * FOX_WIDTH
GLA_COLS = 3 * GLA_KW + GLA_VW
SMALL_COLS = LANES
GLOW_LANE0 = 8
NP_COLS = LRU_COLS + FOX_COLS + GLA_COLS + SMALL_COLS
MXU_COLS = 256
LRU_ROWS = 128

AUG = LANES
N_SPLIT = 3
ONE_LANE = N_SPLIT * FOX_HEADS
NEG_BIG = -1e30
V_ROWS = HEAD_DIM + BF16_ROWS
NQ_SUB = 2
LOG2E = float(np.log2(np.e))


def _sigmoid(x):
    return 0.5 * jnp.tanh(0.5 * x) + 0.5


def _log_sigmoid(x):
    return jnp.minimum(x, 0.0) - jnp.log(1.0 + jnp.exp(-jnp.abs(x)))


def _softplus(x):
    return jnp.maximum(x, 0.0) + jnp.log(1.0 + jnp.exp(-jnp.abs(x)))


def _gelu_tanh(x):
    c = np.float32(np.sqrt(2.0 / np.pi))
    return 0.5 * x * (1.0 + jnp.tanh(c * (x + 0.044715 * (x * x * x))))


def _dot(a, b):
    return jnp.dot(a, b, preferred_element_type=F32)


def _cparams(sem):
    return pltpu.CompilerParams(dimension_semantics=sem, vmem_limit_bytes=VMEM_LIMIT_BYTES)


def _const_spec(shape):
    nd = len(shape)
    return pl.BlockSpec(shape, lambda *_: (0,) * nd)


def _layer_spec(stacked_shape, layer):
    tail = tuple(stacked_shape[1:])
    return pl.BlockSpec((None,) + tail, lambda *_: (layer,) + (0,) * len(tail))


def _layer_weight_spec(stacked_shape, layer):
    tail = tuple(stacked_shape[1:])
    return pl.BlockSpec((None,) + tail, lambda *_: (layer,) + (0,) * len(tail),
                        pipeline_mode=pl.Buffered(1))


def _mod_spec(mod_shape, layer):
    return pl.BlockSpec((None, 1) + tuple(mod_shape[2:]), lambda b, i: (layer, b, 0, 0))


def _mod_kernel(c_ref, w_ref, b_ref, o_ref):
    c = c_ref[...]
    ca = (c * _sigmoid(c)).astype(BF16)
    o_ref[0] = _dot(ca, w_ref[0].astype(BF16)) + b_ref[0]


def _modulation(c, w_mod, b_mod):
    L, D, N = w_mod.shape
    B = c.shape[0]
    bp = -(-B // SUBLANES) * SUBLANES
    cp = jnp.pad(c, ((0, bp - B), (0, 0)))
    tn = 1024
    out = pl.pallas_call(
        _mod_kernel,
        grid=(L, N // tn),
        in_specs=[
            pl.BlockSpec((bp, D), lambda l, n: (0, 0)),
            pl.BlockSpec((1, D, tn), lambda l, n: (l, 0, n)),
            pl.BlockSpec((1, 1, tn), lambda l, n: (l, 0, n)),
        ],
        out_specs=pl.BlockSpec((1, bp, tn), lambda l, n: (l, 0, n)),
        out_shape=jax.ShapeDtypeStruct((L, bp, N), F32),
        compiler_params=_cparams(("arbitrary", "arbitrary")),
        name="adaln_mod",
    )(cp, w_mod, b_mod.reshape(L, 1, N))
    return out.reshape(L, bp, N_MOD, D)


def _rms_mod(x, gain, scale, shift):
    ms = jnp.mean(x * x, axis=-1, keepdims=True)
    y = x * lax.rsqrt(ms + EPS) * gain
    return y * (1.0 + scale) + shift


def _in_proj_kernel(x_ref, mod_ref, g_ref, w_ref, cw_ref, cb_ref, wr_ref, br_ref, wi_ref, bi_ref,
                    lam_ref, oa_ref, fox_ref, gla_ref, small_ref, xs_ref, h_ref):
    tm, W = x_ref.shape[1], LRU_WIDTH

    @pl.when(pl.program_id(1) == 0)
    def _():
        xs_ref[0:SUBLANES, :] = jnp.zeros((SUBLANES, W), F32)
        h_ref[...] = jnp.zeros_like(h_ref)

    h = _rms_mod(x_ref[0], g_ref[...], mod_ref[0, 1:2, :], mod_ref[0, 0:1, :]).astype(BF16)
    segs, c0 = [], LRU_COLS
    for ref, n in ((fox_ref, FOX_COLS), (gla_ref, GLA_COLS), (small_ref, SMALL_COLS)):
        segs.append((ref, c0, c0 + n))
        c0 += n

    def project(g0):
        g1 = g0 + MXU_COLS
        y = _dot(h, w_ref[:, g0:g1])
        for ref, s0, s1 in segs:
            lo, hi = max(g0, s0), min(g1, s1)
            if lo < hi:
                ref[0, :, lo - s0:hi - s0] = y[:, lo - g0:hi - g0].astype(ref.dtype)

    y0 = _dot(h, w_ref[:, 0:LRU_COLS])
    xs_ref[SUBLANES:SUBLANES + tm, :] = y0[:, 0:W]
    pieces = [(s, r0) for s in range(W // LANES) for r0 in range(0, tm, LRU_ROWS)]
    gates = [_lru_gates(y0[r0:r0 + LRU_ROWS, s * LANES:(s + 1) * LANES], r0, slice(s * LANES, (s + 1) * LANES),
                        cw_ref, cb_ref, wr_ref, wi_ref, xs_ref) for s, r0 in pieces]
    xs_ref[0:SUBLANES, :] = xs_ref[tm:tm + SUBLANES, :]
    chunks = list(range(LRU_COLS, NP_COLS, MXU_COLS))
    carry = None
    for (s, r0), g in zip(pieces, gates):
        sl = slice(s * LANES, (s + 1) * LANES)
        rows = slice(r0, r0 + LRU_ROWS)
        if chunks:
            project(chunks.pop(0))
        if r0 == 0:
            carry = h_ref[0:1, sl]
        out, carry = _lru_finish(*g, y0[rows, W + s * LANES:W + (s + 1) * LANES], sl, carry,
                                 br_ref, bi_ref, lam_ref)
        oa_ref[0, rows, sl] = out.astype(oa_ref.dtype)
        if r0 + LRU_ROWS == tm:
            h_ref[:, sl] = jnp.broadcast_to(carry, (h_ref.shape[0], LANES))
    for g0 in chunks:
        project(g0)


def _in_proj(x, mod, gain, w, lru_p, layer, tm):
    B, S, D = x.shape
    W = LRU_WIDTH
    tok = lambda n: pl.BlockSpec((1, tm, n), lambda b, i: (b, i, 0))
    return pl.pallas_call(
        _in_proj_kernel,
        grid=(B, S // tm),
        in_specs=[tok(D), _mod_spec(mod.shape, layer), _layer_spec(gain.shape, layer),
                  _layer_weight_spec(w.shape, layer)] + [_layer_spec(p.shape, layer) for p in lru_p],
        out_specs=[tok(W), tok(FOX_COLS), tok(GLA_COLS), tok(SMALL_COLS)],
        out_shape=[
            jax.ShapeDtypeStruct((B, S, W), BF16),
            jax.ShapeDtypeStruct((B, S, FOX_COLS), BF16),
            jax.ShapeDtypeStruct((B, S, GLA_COLS), BF16),
            jax.ShapeDtypeStruct((B, S, SMALL_COLS), F32),
        ],
        scratch_shapes=[pltpu.VMEM((tm + SUBLANES, W), F32), pltpu.VMEM((SUBLANES, W), F32)],
        compiler_params=_cparams(("arbitrary", "arbitrary")),
        name="in_proj_lru",
    )(x, mod, gain, w, *lru_p)


def _scan_add(v):
    n = v.shape[0]
    row = lax.broadcasted_iota(jnp.int32, v.shape, 0)
    d = 1
    while d < n:
        v = v + jnp.where(row >= d, pltpu.roll(v, d, axis=0), 0.0)
        d *= 2
    return v


def _scan_linear(a, u, h0):
    n, w = a.shape
    g = n // SUBLANES
    a = a.reshape(g, SUBLANES, w)
    u = u.reshape(g, SUBLANES, w)
    sub = lax.broadcasted_iota(jnp.int32, a.shape, 1)
    d = 1
    while d < SUBLANES:
        m = sub >= d
        u = u + jnp.where(m, a * pltpu.roll(u, d, axis=1), 0.0)
        a = jnp.where(m, a * pltpu.roll(a, d, axis=1), a)
        d *= 2
    hs, h = [], h0
    for v in range(g):
        hv = a[v] * h + u[v]
        h = hv[SUBLANES - 1:SUBLANES, :]
        hs.append(hv)
    return jnp.concatenate(hs, axis=0), h


def _lru_gates(xa, r0, sl, cw_ref, cb_ref, wr_ref, wi_ref, xs_ref):
    n = xa.shape[0]
    xc = xa * cw_ref[CONV_WIDTH - 1:CONV_WIDTH, sl] + cb_ref[:, sl]
    for k in range(1, CONV_WIDTH):
        j = CONV_WIDTH - 1 - k
        xc = xc + xs_ref[pl.ds(SUBLANES - k + r0, n), sl] * cw_ref[j:j + 1, sl]
    xcb = xc.astype(BF16)
    return xc, _dot(xcb, wr_ref[sl, sl]), _dot(xcb, wi_ref[sl, sl])


def _lru_finish(xc, r_pre, i_pre, ya, sl, h0, br_ref, bi_ref, lam_ref):
    r = _sigmoid(r_pre + br_ref[:, sl])
    i = _sigmoid(i_pre + bi_ref[:, sl])
    a = jnp.exp2(r * ((-LRU_C * LOG2E) * _softplus(-lam_ref[:, sl])))
    mult = jnp.sqrt(1.0 - a * a)
    u = mult * (i * xc)
    h, h_last = _scan_linear(a, u, h0)
    return h * _gelu_tanh(ya), h_last


def _fox_prep_body(fox_ref, small_ref, bf_ref, qg_ref, kg_ref, ones_ref,
                   rq_ref, rk_ref, kaug_ref, qaugt_ref, vt_ref, cum_ref):
    tm = fox_ref.shape[1]
    W = FOX_WIDTH
    q = fox_ref[0, :, 0:W].astype(F32)
    k = fox_ref[0, :, W:2 * W].astype(F32)
    v = fox_ref[0, :, 2 * W:3 * W].astype(F32)

    def head_norm(t, gain):
        ms = _dot((t * t).astype(BF16), ones_ref[...])
        return t * lax.rsqrt(ms + EPS) * gain

    qn = head_norm(q, qg_ref[...])
    kn = head_norm(k, kg_ref[...])

    lf = _log_sigmoid(small_ref[0] + bf_ref[...])
    cum = _scan_add(lf) + cum_ref[0:1, :]
    cum_ref[...] = jnp.broadcast_to(cum[tm - 1:tm, :], cum_ref.shape)

    cs = cum * LOG2E
    c1 = cs.astype(BF16).astype(F32)
    r1 = cs - c1
    c2 = r1.astype(BF16).astype(F32)
    c3 = (r1 - c2).astype(BF16).astype(F32)
    lane = lax.broadcasted_iota(jnp.int32, cum.shape, 1)
    H = FOX_HEADS
    packed = jnp.where(lane < H, c1, 0.0)
    packed = jnp.where((lane >= H) & (lane < 2 * H), pltpu.roll(c2, H, axis=1), packed)
    packed = jnp.where((lane >= 2 * H) & (lane < 3 * H), pltpu.roll(c3, 2 * H, axis=1), packed)
    packed = jnp.where(lane == ONE_LANE, 1.0, packed).astype(BF16)

    kbias = _dot(packed, rk_ref[...])
    qbias = _dot(packed, rq_ref[...])
    feat = lane < HEAD_DIM

    def head_aug(t, bias, h):
        src = t[:, (h // 2) * LANES:(h // 2 + 1) * LANES]
        if h % 2:
            src = pltpu.roll(src, HEAD_DIM, axis=1)
        return jnp.where(feat, src, bias[:, h * AUG:(h + 1) * AUG])

    vt = v.T
    tk = vt_ref.shape[4]
    for h in range(H):
        kaug_ref[0, h] = head_aug(kn, kbias, h).astype(BF16)
        qaugt_ref[0, h] = head_aug(qn, qbias, h).T.astype(BF16)
        for cb in range(tm // tk):
            vt_ref[0, h, cb, 0:HEAD_DIM, :] = (
                vt[h * HEAD_DIM:(h + 1) * HEAD_DIM, cb * tk:(cb + 1) * tk].astype(BF16))
            vt_ref[0, h, cb, HEAD_DIM:V_ROWS, :] = jnp.ones((V_ROWS - HEAD_DIM, tk), BF16)


def _fox_attn_kernel(k_ref, qt_ref, vt_ref, o_ref, acc_ref, sa_ref, sb_ref, *, tk):
    qi = pl.program_id(1)
    units = [(h, c) for c in range(NQ_SUB) for h in range(FOX_HEADS)]
    acc_ref[...] = jnp.zeros_like(acc_ref)

    def qk(h, c, j):
        start = pl.multiple_of(j * tk, tk)
        return _dot(k_ref[0, h, pl.ds(start, tk), :], qt_ref[0, h, :, c * tk:(c + 1) * tk])

    def update(h, c, j, s, m_old, m_new):
        p = jnp.exp2(s - m_new).astype(BF16)
        acc_ref[h, c] = jnp.exp2(m_old - m_new) * acc_ref[h, c] + _dot(vt_ref[0, h, j], p)

    def step(j, cur_ref, c_cur, nxt_ref, ms):
        new_ms, c_nxt = [], []
        for i, (h, c) in enumerate(units):
            s_n = qk(h, c, j + 1)
            nxt_ref[h, c] = s_n
            c_nxt.append(jnp.max(s_n, axis=0, keepdims=True))
            m_new = jnp.maximum(ms[i], c_cur[i])
            new_ms.append(m_new)
            update(h, c, j, cur_ref[h, c], ms[i], m_new)
        return tuple(new_ms), tuple(c_nxt)

    def pair(t, carry):
        ms, ca = carry
        ms, cb = step(2 * t, sa_ref, ca, sb_ref, ms)
        ms, ca = step(2 * t + 1, sb_ref, cb, sa_ref, ms)
        return ms, ca

    c0 = []
    for h, c in units:
        s = qk(h, c, 0)
        sa_ref[h, c] = s
        c0.append(jnp.max(s, axis=0, keepdims=True))
    carry = (tuple(jnp.full((1, tk), NEG_BIG, F32) for _ in units), tuple(c0))
    assert NQ_SUB % 2 == 0
    n_pairs = (NQ_SUB // 2) * qi
    n_quads = lax.div(n_pairs, 2)
    carry = lax.fori_loop(0, n_quads, lambda u, cr: pair(2 * u + 1, pair(2 * u, cr)), carry)
    carry = lax.fori_loop(2 * n_quads, n_pairs, pair, carry)

    diag = (lax.broadcasted_iota(jnp.int32, (tk, tk), 0)
            <= lax.broadcasted_iota(jnp.int32, (tk, tk), 1))
    ms, cmax = list(carry[0]), list(carry[1])
    bufs = (sa_ref, sb_ref)
    for d in range(NQ_SUB):
        j = NQ_SUB * qi + d
        cur_ref, nxt_ref = bufs[d % 2], bufs[(d + 1) % 2]
        for i, (h, c) in enumerate(units):
            if c < d:
                continue
            if c > d:
                s_n = qk(h, c, j + 1)
                nxt_ref[h, c] = s_n
                c_next = jnp.max(s_n, axis=0, keepdims=True)
            if c == d:
                s = jnp.where(diag, cur_ref[h, c], NEG_BIG)
                m_new = jnp.maximum(ms[i], jnp.max(s, axis=0, keepdims=True))
            else:
                s = cur_ref[h, c]
                m_new = jnp.maximum(ms[i], cmax[i])
                cmax[i] = c_next
            update(h, c, j, s, ms[i], m_new)
            ms[i] = m_new

    for c in range(NQ_SUB):
        out = jnp.concatenate(
            [acc_ref[h, c, 0:HEAD_DIM, :] * (1.0 / acc_ref[h, c, HEAD_DIM:HEAD_DIM + 1, :])
             for h in range(FOX_HEADS)], axis=0)
        o_ref[0, c * tk:(c + 1) * tk, :] = out.T.astype(o_ref.dtype)


def _fox_attention(kaug, qaugt, vt, tk):
    B, H, S, _ = kaug.shape
    tq = NQ_SUB * tk
    assert S % tq == 0
    return pl.pallas_call(
        functools.partial(_fox_attn_kernel, tk=tk),
        grid=(B, S // tq),
        in_specs=[
            pl.BlockSpec((1, H, S, AUG), lambda b, i: (b, 0, 0, 0)),
            pl.BlockSpec((1, H, AUG, tq), lambda b, i: (b, 0, 0, i)),
            pl.BlockSpec((1, H, S // tk, V_ROWS, tk), lambda b, i: (b, 0, 0, 0, 0)),
        ],
        out_specs=pl.BlockSpec((1, tq, H * HEAD_DIM), lambda b, i: (b, i, 0)),
        out_shape=jax.ShapeDtypeStruct((B, S, H * HEAD_DIM), BF16),
        scratch_shapes=[pltpu.VMEM((H, NQ_SUB, V_ROWS, tk), F32), pltpu.VMEM((H, NQ_SUB, tk, tk), F32),
                        pltpu.VMEM((H, NQ_SUB, tk, tk), F32)],
        compiler_params=_cparams(("arbitrary", "arbitrary")),
        name="fox_attention",
    )(kaug, qaugt, vt)


GLA_ROWS = 256


def _gla_body(gla_ref, small_ref, wa_ref, ba_ref, gain_ref, tri_ref, hmask_ref, bdmask_ref,
              ones_ref, o_ref, st_ref):
    ts = o_ref.shape[1]
    KW, C, R = GLA_KW, GLA_CHUNK, GLA_ROWS
    q = gla_ref[0, :, 0:KW].astype(F32)
    k = gla_ref[0, :, KW:2 * KW].astype(F32)
    v = gla_ref[0, :, 2 * KW:3 * KW]
    low = small_ref[0].astype(BF16)
    la = _log_sigmoid(_dot(low, wa_ref[...]) + ba_ref[...]) * (LOG2E / GLA_TAU)
    la_hi = la.astype(BF16)
    la_lo = (la - la_hi.astype(F32)).astype(BF16)
    tri = tri_ref[...]
    bcum = jnp.concatenate(
        [_dot(tri, la_hi[g0:g0 + R]) + _dot(tri, la_lo[g0:g0 + R]) for g0 in range(0, ts, R)], axis=0)
    ends = [bcum[c0 + C - 1:c0 + C, :] for c0 in range(0, ts, C)]
    b_last = jnp.concatenate([jnp.broadcast_to(e, (C, KW)) for e in ends], axis=0)
    q_dec = (q * jnp.exp2(bcum)).astype(BF16)
    k_dec = (k * jnp.exp2(-bcum)).astype(BF16)
    k_end = (k * jnp.exp2(b_last - bcum)).astype(BF16)

    o_groups = []
    for g0 in range(0, ts, R):
        rows = slice(g0, g0 + R)
        o_acc = jnp.zeros((R, KW), F32)
        for h in range(GLA_HEADS):
            hm = hmask_ref[h:h + 1, :].astype(BF16)
            att = lax.dot_general(q_dec[rows] * hm, k_dec[rows], (((1,), (1,)), ((), ())),
                                  preferred_element_type=F32)
            o_acc = o_acc + _dot(att.astype(BF16) * tri, v[rows] * hm)
        o_groups.append(o_acc)

    o_parts = []
    for ci, c0 in enumerate(range(0, ts, C)):
        cs = slice(c0, c0 + C)
        st = st_ref[...]
        o_parts.append(lax.dot_general(q_dec[cs], st.astype(BF16), (((1,), (1,)), ((), ())),
                                       preferred_element_type=F32))
        kv_t = lax.dot_general(v[cs], k_end[cs], (((0,), (0,)), ((), ())),
                               preferred_element_type=F32)
        st_ref[...] = st * jnp.exp2(ends[ci]) + kv_t * bdmask_ref[...]
    o = jnp.concatenate(o_groups, axis=0) + jnp.concatenate(o_parts, axis=0)

    ms = _dot((o * o).astype(BF16), ones_ref[...])
    o = o * lax.rsqrt(ms + EPS) * gain_ref[...]
    g = gla_ref[0, :, 3 * KW:4 * KW].astype(F32)
    o_ref[0] = (o * (g * _sigmoid(g))).astype(o_ref.dtype)


def _fox_gla_kernel(fox_ref, gla_ref, small_ref, bf_ref, qg_ref, kg_ref, fones_ref, rq_ref, rk_ref,
                    wa_ref, ba_ref, gain_ref, tri_ref, hmask_ref, bdmask_ref, gones_ref,
                    kaug_ref, qaugt_ref, vt_ref, oc_ref, cum_ref, st_ref):
    @pl.when(pl.program_id(1) == 0)
    def _():
        cum_ref[...] = jnp.zeros_like(cum_ref)
        st_ref[...] = jnp.zeros_like(st_ref)

    _fox_prep_body(fox_ref, small_ref, bf_ref, qg_ref, kg_ref, fones_ref, rq_ref, rk_ref,
                   kaug_ref, qaugt_ref, vt_ref, cum_ref)
    _gla_body(gla_ref, small_ref, wa_ref, ba_ref, gain_ref, tri_ref, hmask_ref, bdmask_ref,
              gones_ref, oc_ref, st_ref)


def _fox_prep_gla(fox, gla, small, fox_p, gla_p, layer, tm, tk):
    B, S, _ = fox.shape
    H, KW = FOX_HEADS, GLA_KW
    tok = lambda n: pl.BlockSpec((1, tm, n), lambda b, i: (b, i, 0))
    specs = lambda stacked, consts: ([_layer_spec(p.shape, layer) for p in stacked]
                                     + [_const_spec(p.shape) for p in consts])
    fox_specs, gla_specs = specs(*fox_p), specs(*gla_p)
    fox_p, gla_p = fox_p[0] + fox_p[1], gla_p[0] + gla_p[1]
    return pl.pallas_call(
        _fox_gla_kernel,
        grid=(B, S // tm),
        in_specs=[tok(FOX_COLS), tok(GLA_COLS), tok(SMALL_COLS)] + fox_specs + gla_specs,
        out_specs=[
            pl.BlockSpec((1, H, tm, AUG), lambda b, i: (b, 0, i, 0)),
            pl.BlockSpec((1, H, AUG, tm), lambda b, i: (b, 0, 0, i)),
            pl.BlockSpec((1, H, tm // tk, V_ROWS, tk), lambda b, i: (b, 0, i, 0, 0)),
            tok(KW),
        ],
        out_shape=[
            jax.ShapeDtypeStruct((B, H, S, AUG), BF16),
            jax.ShapeDtypeStruct((B, H, AUG, S), BF16),
            jax.ShapeDtypeStruct((B, H, S // tk, V_ROWS, tk), BF16),
            jax.ShapeDtypeStruct((B, S, KW), BF16),
        ],
        scratch_shapes=[pltpu.VMEM((SUBLANES, LANES), F32), pltpu.VMEM((KW, KW), F32)],
        compiler_params=_cparams(("arbitrary", "arbitrary")),
        name="fox_prep_gla",
    )(fox, gla, small, *fox_p, *gla_p)


def _ffn_chunks(d_ff):
    chunks, c0 = [], 0
    while c0 < d_ff:
        n = min(FFN_CHUNK, d_ff - c0)
        chunks.append((c0, n))
        c0 += n
    return chunks


def _out_ffn_kernel(x_ref, a_ref, b_ref, c_ref, mod_ref, g_ref, wo_ref, wgu_ref, wd_ref, o_ref,
                    acc_ref):
    d_ff = wd_ref.shape[0]
    mix = _dot(jnp.concatenate([a_ref[0], b_ref[0], c_ref[0]], axis=1), wo_ref[...])
    x1 = x_ref[0] + mod_ref[0, 2:3, :] * mix
    h = _rms_mod(x1, g_ref[...], mod_ref[0, 4:5, :], mod_ref[0, 3:4, :]).astype(BF16)
    for idx, (c0, n) in enumerate(_ffn_chunks(d_ff)):
        gt = _dot(h, wgu_ref[:, c0:c0 + n])
        up = _dot(h, wgu_ref[:, d_ff + c0:d_ff + c0 + n])
        act = (gt * _sigmoid(gt) * up).astype(BF16)
        y = _dot(act, wd_ref[c0:c0 + n, :])
        if idx == 0:
            acc_ref[...] = y
        else:
            acc_ref[...] += y
    o_ref[0] = x1 + mod_ref[0, 5:6, :] * acc_ref[...]


def _out_ffn(x, a, b, c, mod, gain, wo, wgu, wd, layer, tm):
    B, S, D = x.shape
    tok = lambda n: pl.BlockSpec((1, tm, n), lambda bb, i: (bb, i, 0))
    return pl.pallas_call(
        _out_ffn_kernel,
        grid=(B, S // tm),
        in_specs=[
            tok(D), tok(a.shape[2]), tok(b.shape[2]), tok(c.shape[2]),
            _mod_spec(mod.shape, layer), _layer_spec(gain.shape, layer),
            _layer_weight_spec(wo.shape, layer), _layer_weight_spec(wgu.shape, layer),
            _layer_weight_spec(wd.shape, layer),
        ],
        out_specs=tok(D),
        out_shape=jax.ShapeDtypeStruct((B, S, D), F32),
        scratch_shapes=[pltpu.VMEM((tm, D), F32)],
        compiler_params=_cparams(("arbitrary", "arbitrary")),
        name="out_proj_ffn",
    )(x, a, b, c, mod, gain, wo, wgu, wd)


def _block_diag(w):
    L, n, d, e = w.shape
    eye = jnp.eye(n, dtype=w.dtype)
    return jnp.einsum("lnde,nm->lndme", w, eye).reshape(L, n * d, n * e)


def _regroup_w_in(w_in):
    L, D, _ = w_in.shape
    sizes = (LRU_WIDTH, LRU_WIDTH, FOX_WIDTH, FOX_WIDTH, FOX_WIDTH, FOX_HEADS,
             GLA_KW, GLA_KW, GLA_VW, GLA_RANK, GLA_VW)
    offs = np.concatenate([[0], np.cumsum(sizes)])
    w = w_in.astype(BF16)
    seg = lambda i: w[:, :, offs[i]:offs[i + 1]]
    z = lambda n: jnp.zeros((L, D, n), BF16)
    q_scale = GLA_DK ** -0.5
    assert np.log2(q_scale) == np.round(np.log2(q_scale))
    cols = [seg(0), seg(1), seg(2), seg(3), seg(4), seg(6) * q_scale, seg(7), seg(8), seg(10),
            seg(5), z(GLOW_LANE0 - FOX_HEADS), seg(9), z(SMALL_COLS - GLOW_LANE0 - GLA_RANK)]
    return jnp.concatenate(cols, axis=-1)


def _head_ones(n_heads, dim):
    assert dim & (dim - 1) == 0
    return jnp.asarray(np.kron(np.eye(n_heads), np.full((dim, dim), 1.0 / dim)), BF16)


def _fox_constants():
    H, W = FOX_HEADS, FOX_WIDTH
    rq = np.zeros((LANES, H * AUG), np.float32)
    rk = np.zeros((LANES, H * AUG), np.float32)
    for h in range(H):
        for s in range(N_SPLIT):
            rq[s * H + h, h * AUG + HEAD_DIM + s] = 1.0
            rq[ONE_LANE, h * AUG + HEAD_DIM + N_SPLIT + s] = 1.0
            rk[ONE_LANE, h * AUG + HEAD_DIM + s] = 1.0
            rk[s * H + h, h * AUG + HEAD_DIM + N_SPLIT + s] = -1.0
    return jnp.asarray(rq, BF16), jnp.asarray(rk, BF16)


def _gla_constants():
    R, C, KW = GLA_ROWS, GLA_CHUNK, GLA_KW
    i = np.arange(R)
    tri = ((i[:, None] >= i[None, :]) & (i[:, None] // C == i[None, :] // C)).astype(np.float32)
    lane = np.arange(KW)
    hmask = (lane[None, :] // GLA_DK == np.arange(GLA_HEADS)[:, None]).astype(np.float32)
    bdmask = (lane[:, None] // GLA_DK == lane[None, :] // GLA_DK).astype(np.float32)
    return jnp.asarray(tri, BF16), jnp.asarray(hmask, F32), jnp.asarray(bdmask, F32)


def _pick_tile(n, want):
    t = min(n, want)
    while n % t:
        t //= 2
    return t


def kernel(x, c, norm1_gain, norm2_gain, w_mod, b_mod, w_in, conv_w, conv_b, lru_w_r, lru_b_r,
           lru_w_i, lru_b_i, lru_lambda, fox_b_f, fox_q_gain, fox_k_gain, gla_w_alpha, gla_b_alpha,
           gla_out_gain, w_out, ffn_w_gate_up, ffn_w_down):
    B, S, D = x.shape
    L = w_in.shape[0]
    tm = _pick_tile(S, 1024)
    tp = _pick_tile(S, 1024)
    tq = _pick_tile(S, 256)

    mod = _modulation(c, w_mod, b_mod)
    w_in_p = _regroup_w_in(w_in)
    wr_bd = _block_diag(lru_w_r).astype(BF16)
    wi_bd = _block_diag(lru_w_i).astype(BF16)
    w_out_b = w_out.astype(BF16)
    wgu_b = ffn_w_gate_up.astype(BF16)
    wd_b = ffn_w_down.astype(BF16)
    row = lambda a: a.reshape(L, 1, a.shape[-1])
    bf_pad = jnp.pad(fox_b_f, ((0, 0), (0, LANES - FOX_HEADS))).reshape(L, 1, LANES)
    qg = jnp.tile(fox_q_gain * (HEAD_DIM ** -0.5 * LOG2E), (1, FOX_HEADS)).reshape(L, 1, FOX_WIDTH)
    kg = jnp.tile(fox_k_gain, (1, FOX_HEADS)).reshape(L, 1, FOX_WIDTH)
    wa_pad = jnp.pad(gla_w_alpha, ((0, 0), (GLOW_LANE0, LANES - GLOW_LANE0 - GLA_RANK), (0, 0))).astype(BF16)
    og = jnp.tile(gla_out_gain, (1, GLA_HEADS)).reshape(L, 1, GLA_VW)
    fox_ones = _head_ones(FOX_HEADS, HEAD_DIM)
    gla_ones = _head_ones(GLA_HEADS, GLA_DK)
    rq, rk = _fox_constants()
    tri, hmask, bdmask = _gla_constants()

    g1, g2 = row(norm1_gain), row(norm2_gain)
    lru_p = (conv_w, row(conv_b), wr_bd, row(lru_b_r), wi_bd, row(lru_b_i), row(lru_lambda))
    fox_p = ((bf_pad, qg, kg), (fox_ones, rq, rk))
    gla_p = ((wa_pad, row(gla_b_alpha), og), (tri, hmask, bdmask, gla_ones))
    for l in range(L):
        out_a, fox, gla, small = _in_proj(x, mod, g1, w_in_p, lru_p, l, tp)
        kaug, qaugt, vt, out_c = _fox_prep_gla(fox, gla, small, fox_p, gla_p, l, tp, tq)
        out_b = _fox_attention(kaug, qaugt, vt, tq)
        x = _out_ffn(x, out_a, out_b, out_c, mod, g2, w_out_b, wgu_b, wd_b, l, tm)
    return x
```
